```python
import math
import jax, jax.numpy as jnp
from jax import lax
import numpy as np

D_MODEL = 1024
BATCH = 8
SEQ = 2048
DEPTH = 2

F32 = jnp.float32
CHUNK = 64
N_BRANCH = 4
BRANCH_WIDTH = D_MODEL
EPS = 1e-6

SSM_HEAD_DIM = 64
SSM_HEADS = BRANCH_WIDTH // SSM_HEAD_DIM
SSM_GROUPS = 4
SSM_STATE = 128
SSM_CONV = 4
SSM_XBC = BRANCH_WIDTH + 2 * SSM_GROUPS * SSM_STATE

LRU_BLOCKS = 16
LRU_BLOCK = BRANCH_WIDTH // LRU_BLOCKS
LRU_CONV = 4
LRU_C = 8.0

RET_HEADS = 8
RET_QK_DIM = 64
RET_V_DIM = BRANCH_WIDTH // RET_HEADS
RET_QK = RET_HEADS * RET_QK_DIM
ROPE_BASE = 10000.0

RWKV_HEAD = 64
RWKV_HEADS = BRANCH_WIDTH // RWKV_HEAD
RWKV_W_LORA = 64
RWKV_A_LORA = 64
RWKV_G_LORA = 128
RWKV_IN = 3 * BRANCH_WIDTH + RWKV_W_LORA + RWKV_A_LORA + RWKV_G_LORA
RWKV_LN_EPS = 64e-5

D_FF = 256 * ((8 * D_MODEL // 3 + 255) // 256)
FFN_CONV = 3

IN_SPLITS = (BRANCH_WIDTH, SSM_XBC, SSM_HEADS,
             BRANCH_WIDTH, BRANCH_WIDTH,
             RET_QK, RET_QK, BRANCH_WIDTH, BRANCH_WIDTH,
             RWKV_IN,
             N_BRANCH * D_MODEL)
N_IN = sum(IN_SPLITS)

kernel_name = "hybrid_ssd_rglru_retnet_rwkv7_block"


def rms_norm(x, w):
    xf = x.astype(F32)
    y = xf * lax.rsqrt(jnp.mean(xf * xf, axis=-1, keepdims=True) + EPS)
    return (y * w.astype(F32)).astype(x.dtype)


def head_norm(y, eps):
    mu = jnp.mean(y, axis=-1, keepdims=True)
    yc = y - mu
    return yc * lax.rsqrt(jnp.mean(yc * yc, axis=-1, keepdims=True) + eps)


def causal_dwconv(x, w, b):
    k = w.shape[0]
    y = lax.conv_general_dilated(x, w[:, None, :].astype(x.dtype), window_strides=(1,),
                                 padding=[(k - 1, 0)],
                                 dimension_numbers=('NWC', 'WIO', 'NWC'),
                                 feature_group_count=x.shape[-1])
    return y + b.astype(x.dtype)


def rotary(x, cos, sin):
    x1, x2 = jnp.split(x, 2, axis=-1)
    c, s = cos[:, None, :], sin[:, None, :]
    return jnp.concatenate([x1 * c - x2 * s, x1 * s + x2 * c], axis=-1)


def mamba2_branch(z, xbc, dt, conv_w, conv_b, dt_bias, a_log, d_skip, norm_w):
    b, s, _ = z.shape
    nc, hg = s // CHUNK, SSM_HEADS // SSM_GROUPS
    xbc = jax.nn.silu(causal_dwconv(xbc, conv_w, conv_b)).astype(F32)
    xs, bm, cm = jnp.split(xbc, [BRANCH_WIDTH, BRANCH_WIDTH + SSM_GROUPS * SSM_STATE], axis=-1)
    xs = xs.reshape(b, nc, CHUNK, SSM_GROUPS, hg, SSM_HEAD_DIM)
    bm = bm.reshape(b, nc, CHUNK, SSM_GROUPS, SSM_STATE)
    cm = cm.reshape(b, nc, CHUNK, SSM_GROUPS, SSM_STATE)
    dt = jax.nn.softplus(dt.astype(F32) + dt_bias.astype(F32)).reshape(b, nc, CHUNK, SSM_GROUPS, hg)
    a = -jnp.exp(a_log.astype(F32)).reshape(SSM_GROUPS, hg)
    a_cum = jnp.cumsum(dt * a, axis=2)
    xdt = xs * dt[..., None]
    causal = jnp.tril(jnp.ones((CHUNK, CHUNK), bool))[:, :, None, None]
    seg = a_cum[:, :, :, None] - a_cum[:, :, None, :]
    decay = jnp.exp(jnp.where(causal, seg, -jnp.inf))
    cb = jnp.einsum('bclgn,bcsgn->bclsg', cm, bm)
    y_diag = jnp.einsum('bclsgh,bcsghp->bclghp', cb[..., None] * decay, xdt)
    decay_to_end = jnp.exp(a_cum[:, :, -1:] - a_cum)
    states = jnp.einsum('bclgn,bclgh,bclghp->bcghpn', bm, decay_to_end, xdt)
    chunk_decay = jnp.exp(a_cum[:, :, -1])

    def step(h, inp):
        st, dec = inp
        return h * dec[..., None, None] + st, h

    h0 = jnp.zeros((b, SSM_GROUPS, hg, SSM_HEAD_DIM, SSM_STATE), F32)
    _, prev = lax.scan(step, h0, (jnp.moveaxis(states, 1, 0), jnp.moveaxis(chunk_decay, 1, 0)))
    prev = jnp.moveaxis(prev, 0, 1)
    y_off = jnp.einsum('bclgn,bcghpn,bclgh->bclghp', cm, prev, jnp.exp(a_cum))
    y = y_diag + y_off + xs * d_skip.astype(F32).reshape(SSM_GROUPS, hg, 1)
    y = y.reshape(b, s, BRANCH_WIDTH) * jax.nn.silu(z.astype(F32))
    return rms_norm(y, norm_w).astype(z.dtype)


def rglru_branch(gate_in, x_in, conv_w, conv_b, w_a, b_a, w_i, b_i, lam):
    b, s, _ = x_in.shape
    xc = causal_dwconv(x_in, conv_w, conv_b).astype(F32)
    xb = xc.reshape(b, s, LRU_BLOCKS, LRU_BLOCK)
    r = jax.nn.sigmoid(jnp.einsum('bsgi,gij->bsgj', xb, w_a.astype(F32)).reshape(b, s, BRANCH_WIDTH) + b_a.astype(F32))
    i = jax.nn.sigmoid(jnp.einsum('bsgi,gij->bsgj', xb, w_i.astype(F32)).reshape(b, s, BRANCH_WIDTH) + b_i.astype(F32))
    log_a = -LRU_C * r * jax.nn.softplus(-lam.astype(F32))
    u = jnp.sqrt(-jnp.expm1(2.0 * log_a)) * (i * xc)

    def combine(p, q):
        a1, u1 = p
        a2, u2 = q
        return a1 * a2, a2 * u1 + u2

    _, h = lax.associative_scan(combine, (jnp.exp(log_a), u), axis=1)
    return (h * jax.nn.gelu(gate_in.astype(F32), approximate=True)).astype(x_in.dtype)


def retention_branch(q, k, v, g, norm_w, cos, sin):
    b, s, _ = q.shape
    nc = s // CHUNK
    out_dtype = g.dtype
    qh = rotary(q.astype(F32).reshape(b, s, RET_HEADS, RET_QK_DIM), cos, sin)
    kh = rotary(k.astype(F32).reshape(b, s, RET_HEADS, RET_QK_DIM), cos, sin) * (RET_QK_DIM ** -0.5)
    vh = v.astype(F32).reshape(b, s, RET_HEADS, RET_V_DIM)
    qh, kh, vh = (t.reshape(b, nc, CHUNK, RET_HEADS, t.shape[-1]) for t in (qh, kh, vh))
    log_gamma = jnp.log1p(-jnp.exp2(-5.0 - jnp.arange(RET_HEADS, dtype=F32)))
    pos = jnp.arange(CHUNK, dtype=F32)
    inner = jnp.exp(log_gamma[:, None, None] * jnp.abs(pos[:, None] - pos[None, :]))
    scores = jnp.einsum('bclhd,bcshd->bchls', qh, kh) * inner
    y_inner = jnp.einsum('bchls,bcshe->bclhe', scores, vh)
    k_to_end = jnp.exp(log_gamma[:, None] * (CHUNK - 1.0 - pos))
    kv = jnp.einsum('bcshd,bcshe,hs->bchde', kh, vh, k_to_end)
    chunk_decay = jnp.exp(log_gamma * CHUNK)

    def step(r_state, kv_c):
        return r_state * chunk_decay[:, None, None] + kv_c, r_state

    r0 = jnp.zeros((b, RET_HEADS, RET_QK_DIM, RET_V_DIM), F32)
    _, prev = lax.scan(step, r0, jnp.moveaxis(kv, 1, 0))
    prev = jnp.moveaxis(prev, 0, 1)
    q_from_start = jnp.exp(log_gamma[:, None] * (pos + 1.0))
    y_cross = jnp.einsum('bclhd,bchde,hl->bclhe', qh, prev, q_from_start)
    y = head_norm((y_inner + y_cross).reshape(b, s, RET_HEADS, RET_V_DIM), EPS)
    y = y.reshape(b, s, BRANCH_WIDTH) * norm_w.astype(F32)
    return (jax.nn.silu(g.astype(F32)) * y).astype(out_dtype)


def rwkv7_branch(rw, mu, w0, w2, a0, a2, g2, k_k, k_a, r_k, ln_w, ln_b):
    b, s, _ = rw.shape
    out_dtype = rw.dtype
    rw = rw.astype(F32)
    prev = jnp.pad(rw, ((0, 0), (1, 0), (0, 0)))[:, :-1]
    rw = rw + (prev - rw) * mu.astype(F32)
    wd = BRANCH_WIDTH
    r, k, v, wl, al, gl = jnp.split(
        rw, [wd, 2 * wd, 3 * wd, 3 * wd + RWKV_W_LORA, 3 * wd + RWKV_W_LORA + RWKV_A_LORA], axis=-1)
    w = -jax.nn.softplus(-(w0.astype(F32) + jnp.tanh(wl) @ w2.astype(F32))) - 0.5
    decay = jnp.exp(-jnp.exp(w))
    a = jax.nn.sigmoid(a0.astype(F32) + al @ a2.astype(F32))
    gate = jax.nn.sigmoid(gl) @ g2.astype(F32)
    heads = lambda t: t.reshape(b, s, RWKV_HEADS, RWKV_HEAD)
    kk = heads(k * k_k.astype(F32))
    kk = kk / jnp.maximum(jnp.sqrt(jnp.sum(kk * kk, axis=-1, keepdims=True)), 1e-12)
    k = k * (1.0 + (a - 1.0) * k_a.astype(F32))
    r_h, k_h, v_h, w_h, a_h = heads(r), heads(k), heads(v), heads(decay), heads(a)

    def step(state, inp):
        r_t, w_t, k_t, v_t, kk_t, b_t = inp
        sk = jnp.einsum('bhvk,bhk->bhv', state, kk_t)
        state = (state * w_t[:, :, None, :] - sk[..., None] * b_t[:, :, None, :]
                 + v_t[..., None] * k_t[:, :, None, :])
        return state, jnp.einsum('bhvk,bhk->bhv', state, r_t)

    s0 = jnp.zeros((b, RWKV_HEADS, RWKV_HEAD, RWKV_HEAD), F32)
    seq_first = lambda t: jnp.moveaxis(t, 1, 0)
    _, o = lax.scan(step, s0, (seq_first(r_h), seq_first(w_h), seq_first(k_h), seq_first(v_h),
                               seq_first(kk), seq_first(kk * a_h)))
    o = jnp.moveaxis(o, 0, 1)
    o = head_norm(o, RWKV_LN_EPS).reshape(b, s, wd) * ln_w.astype(F32) + ln_b.astype(F32)
    bonus = jnp.sum(r_h * k_h * r_k.astype(F32).reshape(RWKV_HEADS, RWKV_HEAD), axis=-1, keepdims=True) * v_h
    o = o + bonus.reshape(b, s, wd)
    return (o * gate).astype(out_dtype)


def hybrid_mixer(h, w_in, ssm_p, lru_p, ret_p, rwkv_p, w_branch, w_out, cos, sin):
    b, s, _ = h.shape
    proj = h @ w_in
    (m_z, m_xbc, m_dt, l_gate, l_x, r_q, r_k, r_v, r_g, rw, gate_logits) = jnp.split(
        proj, np.cumsum(IN_SPLITS)[:-1].tolist(), axis=-1)
    y_ssd = mamba2_branch(m_z, m_xbc, m_dt, *ssm_p)
    y_lru = rglru_branch(l_gate, l_x, *lru_p)
    y_ret = retention_branch(r_q, r_k, r_v, r_g, *ret_p, cos, sin)
    y_rwkv = rwkv7_branch(rw, *rwkv_p)
    ys = jnp.stack([y_ssd, y_lru, y_ret, y_rwkv], axis=2)
    branch = jnp.einsum('bsmw,mwd->bsmd', ys, w_branch)
    gates = jax.nn.sigmoid(gate_logits.astype(F32)).reshape(b, s, N_BRANCH, D_MODEL)
    merged = jnp.sum(gates * branch.astype(F32), axis=2).astype(h.dtype)
    return merged @ w_out


def conv_ffn(h, w_up, conv_w, conv_b, w_down):
    u = causal_dwconv(h @ w_up, conv_w, conv_b)
    val, gt = jnp.split(u, 2, axis=-1)
    return (jax.nn.silu(gt) * val) @ w_down


def setup_inputs(seed: int = 0) -> dict:
    key = jax.random.key(seed)
    ks = iter(jax.random.split(key, 64))
    nrm = lambda shape, scale: jax.random.normal(next(ks), shape, F32) * scale
    unif = lambda shape, lo, hi: jax.random.uniform(next(ks), shape, F32, lo, hi)
    L, W, D = DEPTH, BRANCH_WIDTH, D_MODEL
    gain = lambda shape: 1.0 + nrm(shape, 0.02)
    x = nrm((BATCH, SEQ, D), 1.0)
    norm_mix = gain((L, D))
    w_in = nrm((L, D, N_IN), D ** -0.5)
    ssm_conv_w = nrm((L, SSM_CONV, SSM_XBC), SSM_CONV ** -0.5)
    ssm_conv_b = nrm((L, SSM_XBC), 0.02)
    dt0 = jnp.exp(unif((L, SSM_HEADS), math.log(1e-3), math.log(1e-1)))
    ssm_dt_bias = dt0 + jnp.log(-jnp.expm1(-dt0))
    ssm_a_log = jnp.log(unif((L, SSM_HEADS), 1.0, 16.0))
    ssm_d = 1.0 + nrm((L, SSM_HEADS), 0.1)
    ssm_norm = gain((L, W))
    lru_conv_w = nrm((L, LRU_CONV, W), LRU_CONV ** -0.5)
    lru_conv_b = nrm((L, W), 0.02)
    lru_w_a = nrm((L, LRU_BLOCKS, LRU_BLOCK, LRU_BLOCK), LRU_BLOCK ** -0.5)
    lru_b_a = nrm((L, W), 0.02)
    lru_w_i = nrm((L, LRU_BLOCKS, LRU_BLOCK, LRU_BLOCK), LRU_BLOCK ** -0.5)
    lru_b_i = nrm((L, W), 0.02)
    p = unif((L, W), 0.9, 0.999) ** (1.0 / LRU_C)
    lru_lam = jnp.log(p) - jnp.log1p(-p)
    ret_norm = gain((L, W))
    rwkv_mu = unif((L, RWKV_IN), 0.0, 1.0)
    ratio = jnp.arange(W, dtype=F32) / (W - 1)
    rwkv_w0 = (-7.0 + 5.0 * ratio ** 0.85 + 0.5)[None, :] + nrm((L, W), 0.1)
    rwkv_w2 = nrm((L, RWKV_W_LORA, W), 0.1 * RWKV_W_LORA ** -0.5)
    rwkv_a0 = nrm((L, W), 0.1)
    rwkv_a2 = nrm((L, RWKV_A_LORA, W), 0.5 * RWKV_A_LORA ** -0.5)
    rwkv_g2 = nrm((L, RWKV_G_LORA, W), RWKV_G_LORA ** -0.5)
    rwkv_k_k = 0.85 + nrm((L, W), 0.02)
    rwkv_k_a = 1.0 + nrm((L, W), 0.02)
    rwkv_r_k = nrm((L, W), 0.1)
    rwkv_ln_w = gain((L, W))
    rwkv_ln_b = nrm((L, W), 0.02)
    w_branch = nrm((L, N_BRANCH, W, D), W ** -0.5)
    w_out = nrm((L, D, D), D ** -0.5)
    norm_ffn = gain((L, D))
    ffn_up = nrm((L, D, 2 * D_FF), D ** -0.5)
    ffn_conv_w = nrm((L, FFN_CONV, 2 * D_FF), FFN_CONV ** -0.5)
    ffn_conv_b = nrm((L, 2 * D_FF), 0.02)
    ffn_down = nrm((L, D_FF, D), D_FF ** -0.5)
    norm_final = gain((D,))
    return {"x": x, "norm_mix": norm_mix, "w_in": w_in,
            "ssm_conv_w": ssm_conv_w, "ssm_conv_b": ssm_conv_b, "ssm_dt_bias": ssm_dt_bias,
            "ssm_a_log": ssm_a_log, "ssm_d": ssm_d, "ssm_norm": ssm_norm,
            "lru_conv_w": lru_conv_w, "lru_conv_b": lru_conv_b, "lru_w_a": lru_w_a,
            "lru_b_a": lru_b_a, "lru_w_i": lru_w_i, "lru_b_i": lru_b_i, "lru_lam": lru_lam,
            "ret_norm": ret_norm,
            "rwkv_mu": rwkv_mu, "rwkv_w0": rwkv_w0, "rwkv_w2": rwkv_w2, "rwkv_a0": rwkv_a0,
            "rwkv_a2": rwkv_a2, "rwkv_g2": rwkv_g2, "rwkv_k_k": rwkv_k_k, "rwkv_k_a": rwkv_k_a,
            "rwkv_r_k": rwkv_r_k, "rwkv_ln_w": rwkv_ln_w, "rwkv_ln_b": rwkv_ln_b,
            "w_branch": w_branch, "w_out": w_out, "norm_ffn": norm_ffn,
            "ffn_up": ffn_up, "ffn_conv_w": ffn_conv_w, "ffn_conv_b": ffn_conv_b,
            "ffn_down": ffn_down, "norm_final": norm_final}


def reference(x, norm_mix, w_in, ssm_conv_w, ssm_conv_b, ssm_dt_bias, ssm_a_log, ssm_d, ssm_norm,
              lru_conv_w, lru_conv_b, lru_w_a, lru_b_a, lru_w_i, lru_b_i, lru_lam, ret_norm,
              rwkv_mu, rwkv_w0, rwkv_w2, rwkv_a0, rwkv_a2, rwkv_g2, rwkv_k_k, rwkv_k_a, rwkv_r_k,
              rwkv_ln_w, rwkv_ln_b, w_branch, w_out, norm_ffn, ffn_up, ffn_conv_w, ffn_conv_b,
              ffn_down, norm_final):
    s = x.shape[1]
    pos = jnp.arange(s, dtype=F32)
    inv_freq = ROPE_BASE ** (-jnp.arange(0, RET_QK_DIM, 2, dtype=F32) / RET_QK_DIM)
    ang = pos[:, None] * inv_freq[None, :]
    cos, sin = jnp.cos(ang), jnp.sin(ang)
    for l in range(DEPTH):
        h = rms_norm(x, norm_mix[l])
        ssm_p = (ssm_conv_w[l], ssm_conv_b[l], ssm_dt_bias[l], ssm_a_log[l], ssm_d[l], ssm_norm[l])
        lru_p = (lru_conv_w[l], lru_conv_b[l], lru_w_a[l], lru_b_a[l], lru_w_i[l], lru_b_i[l], lru_lam[l])
        ret_p = (ret_norm[l],)
        rwkv_p = (rwkv_mu[l], rwkv_w0[l], rwkv_w2[l], rwkv_a0[l], rwkv_a2[l], rwkv_g2[l],
                  rwkv_k_k[l], rwkv_k_a[l], rwkv_r_k[l], rwkv_ln_w[l], rwkv_ln_b[l])
        x = x + hybrid_mixer(h, w_in[l], ssm_p, lru_p, ret_p, rwkv_p, w_branch[l], w_out[l], cos, sin)
        h = rms_norm(x, norm_ffn[l])
        x = x + conv_ffn(h, ffn_up[l], ffn_conv_w[l], ffn_conv_b[l], ffn_down[l])
    return rms_norm(x, norm_final)
```

```python
import functools

import jax
import jax.numpy as jnp
from jax import lax
from jax.experimental import pallas as pl
from jax.experimental.pallas import tpu as pltpu

F32 = jnp.float32
BF16 = jnp.bfloat16

D_MODEL = 1024
WIDTH = 1024
DEPTH = 2
CHUNK = 64
EPS = 1e-6

SSM_HEADS = 16
SSM_HEAD_DIM = 64
SSM_GROUPS = 4
SSM_STATE = 128
SSM_CONV = 4
SSM_XBC = WIDTH + 2 * SSM_GROUPS * SSM_STATE

LRU_BLOCKS = 16
LRU_BLOCK = 64
LRU_CONV = 4
LRU_C = 8.0

RET_HEADS = 8
RET_QK_DIM = 64
RET_V_DIM = 128
RET_QK = RET_HEADS * RET_QK_DIM
ROPE_BASE = 10000.0

RWKV_HEAD = 64
RWKV_HEADS = 16
RWKV_W_LORA = 64
RWKV_A_LORA = 64
RWKV_G_LORA = 128
RWKV_LORA = RWKV_W_LORA + RWKV_A_LORA + RWKV_G_LORA
RWKV_LN_EPS = 64e-5

D_FF = 2816
FFN_CONV = 3
N_BRANCH = 4

LANE = 128
SUBLANE = 8
VMEM_LIMIT = 56 * 1024 * 1024

_SRC = {}
_o = 0
for _n, _w in (("z", WIDTH), ("xbc", SSM_XBC), ("dt", SSM_HEADS), ("lgate", WIDTH), ("lx", WIDTH),
               ("rq", RET_QK), ("rk", RET_QK), ("rv", WIDTH), ("rg", WIDTH),
               ("wr", WIDTH), ("wk", WIDTH), ("wv", WIDTH), ("lora", RWKV_LORA),
               ("gates", N_BRANCH * D_MODEL)):
    _SRC[_n] = (_o, _w)
    _o += _w
N_IN = _o
_DST_ORDER = ("gates", "xbc", "z", "lgate", "lx", "rv", "rg", "rq", "rk", "wr", "wk", "wv", "lora", "dt")
_DST = {}
_o = 0
for _n in _DST_ORDER:
    _w = _SRC[_n][1]
    _bw = max(_w, LANE)
    assert _o % _bw == 0, (_n, _o, _bw)
    _DST[_n] = (_o, _bw)
    _o += _bw
PROJ_TN = 512
N_PROJ = -(-_o // PROJ_TN) * PROJ_TN


def _col_block(name):
    off, bw = _DST[name]
    return off // bw


def _softplus(x):
    return jnp.maximum(x, 0.0) + jnp.log1p(jnp.exp(-jnp.abs(x)))


def _silu(x):
    return x * jax.nn.sigmoid(x)


def _bdot(a, b):
    return jnp.dot(a.astype(BF16), b.astype(BF16), preferred_element_type=F32)


def _bdot_nt(a, b):
    return lax.dot_general(a.astype(BF16), b.astype(BF16), (((1,), (1,)), ((), ())),
                           preferred_element_type=F32)


def _bdot_tn(a, b):
    return lax.dot_general(a.astype(BF16), b.astype(BF16), (((0,), (0,)), ((), ())),
                           preferred_element_type=F32)


def _cumsum_rows(x):
    n = x.shape[0]
    row = lax.broadcasted_iota(jnp.int32, (n, n), 0)
    col = lax.broadcasted_iota(jnp.int32, (n, n), 1)
    tri = (col <= row).astype(F32)
    return jnp.dot(tri, x, precision=lax.Precision.HIGHEST, preferred_element_type=F32)


def _compiler_params(sem):
    return pltpu.CompilerParams(dimension_semantics=sem, vmem_limit_bytes=VMEM_LIMIT)


def _norm_matmul_kernel(x_ref, g_ref, w_ref, o_ref, hn_ref):
    @pl.when(pl.program_id(1) == 0)
    def _():
        x = x_ref[...]
        ms = jnp.mean(x * x, axis=-1, keepdims=True)
        hn_ref[...] = (x * lax.rsqrt(ms + EPS) * g_ref[...]).astype(BF16)

    o_ref[...] = jnp.dot(hn_ref[...], w_ref[...], preferred_element_type=F32)


def _norm_matmul(x2, gain, w, *, tm=1024, tn=PROJ_TN):
    t, d = x2.shape
    n = w.shape[1]
    return pl.pallas_call(
        _norm_matmul_kernel,
        grid=(t // tm, n // tn),
        in_specs=[pl.BlockSpec((tm, d), lambda i, j: (i, 0)),
                  pl.BlockSpec((1, d), lambda i, j: (0, 0)),
                  pl.BlockSpec((d, tn), lambda i, j: (0, j))],
        out_specs=pl.BlockSpec((tm, tn), lambda i, j: (i, j)),
        out_shape=jax.ShapeDtypeStruct((t, n), F32),
        scratch_shapes=[pltpu.VMEM((tm, d), BF16)],
        compiler_params=_compiler_params(("parallel", "arbitrary")),
        name="norm_in_proj",
    )(x2, gain.reshape(1, d), w)


def _ssd_kernel(z_ref, xbc_ref, dt_ref, cw_ref, cb_ref, dtb_ref, alog_ref, dsk_ref, nw_ref, o_ref,
                cbuf, act, state, ybuf, *, blk):
    @pl.when(pl.program_id(1) == 0)
    def _():
        cbuf[0:SUBLANE, :] = jnp.zeros((SUBLANE, SSM_XBC), F32)
        state[...] = jnp.zeros_like(state)

    cbuf[SUBLANE:SUBLANE + blk, :] = xbc_ref[...]
    conv = cb_ref[...]
    for k in range(SSM_CONV):
        s = SSM_CONV - 1 - k
        conv = conv + cw_ref[k:k + 1, :] * cbuf[SUBLANE - s:SUBLANE - s + blk, :]
    act[...] = _silu(conv)
    cbuf[0:SUBLANE, :] = cbuf[blk:blk + SUBLANE, :]

    a_neg = -jnp.exp(alog_ref[...])
    row = lax.broadcasted_iota(jnp.int32, (CHUNK, CHUNK), 0)
    col = lax.broadcasted_iota(jnp.int32, (CHUNK, CHUNK), 1)
    causal = col <= row
    hg = SSM_HEADS // SSM_GROUPS

    def chunk(c, carry):
        r0 = pl.multiple_of(c * CHUNK, CHUNK)
        dt = _softplus(dt_ref[pl.ds(r0, CHUNK), :] + dtb_ref[...])
        a_cum = _cumsum_rows(dt * a_neg)
        a_cum_t = a_cum.T
        a_last = a_cum[CHUNK - 1:CHUNK, :]
        to_end = jnp.exp(a_last - a_cum)
        from_start = jnp.exp(a_cum)
        chunk_decay = jnp.exp(a_last)
        for g in range(SSM_GROUPS):
            bm = act[pl.ds(r0, CHUNK), WIDTH + g * SSM_STATE:WIDTH + (g + 1) * SSM_STATE]
            cm = act[pl.ds(r0, CHUNK), WIDTH + (SSM_GROUPS + g) * SSM_STATE:
                     WIDTH + (SSM_GROUPS + g + 1) * SSM_STATE]
            cb = _bdot_nt(cm, bm)
            bm_t = bm.T
            for i in range(hg):
                h = g * hg + i
                xs = act[pl.ds(r0, CHUNK), h * SSM_HEAD_DIM:(h + 1) * SSM_HEAD_DIM]
                xdt = xs * dt[:, h:h + 1]
                seg = a_cum[:, h:h + 1] - a_cum_t[h:h + 1, :]
                decay = jnp.exp(jnp.where(causal, seg, -jnp.inf))
                y = _bdot(cb * decay, xdt)
                prev = state[h]
                y = y + _bdot(cm, prev) * from_start[:, h:h + 1]
                y = y + xs * dsk_ref[:, h * SSM_HEAD_DIM:(h + 1) * SSM_HEAD_DIM]
                new = _bdot(bm_t, xdt * to_end[:, h:h + 1])
                state[h] = prev * chunk_decay[:, h:h + 1] + new
                ybuf[pl.ds(r0, CHUNK), h * SSM_HEAD_DIM:(h + 1) * SSM_HEAD_DIM] = y
        return carry

    lax.fori_loop(0, blk // CHUNK, chunk, 0)

    y = ybuf[...] * _silu(z_ref[...])
    ms = jnp.mean(y * y, axis=-1, keepdims=True)
    o_ref[...] = y * lax.rsqrt(ms + EPS) * nw_ref[...]


def _ssd_branch(proj, b, s, conv_w, conv_b, dt_bias, a_log, d_skip, norm_w, *, blk=256):
    nblk = s // blk
    pad = lambda v: jnp.pad(v.reshape(1, -1), ((0, 0), (0, LANE - v.shape[0])))
    dsk = jnp.repeat(d_skip, SSM_HEAD_DIM).reshape(1, WIDTH)
    row = lambda i, j: i * nblk + j
    const = lambda i, j: (0, 0)
    return pl.pallas_call(
        functools.partial(_ssd_kernel, blk=blk),
        grid=(b, nblk),
        in_specs=[pl.BlockSpec((blk, WIDTH), lambda i, j: (row(i, j), _col_block("z"))),
                  pl.BlockSpec((blk, SSM_XBC), lambda i, j: (row(i, j), _col_block("xbc"))),
                  pl.BlockSpec((blk, LANE), lambda i, j: (row(i, j), _col_block("dt"))),
                  pl.BlockSpec((SSM_CONV, SSM_XBC), const),
                  pl.BlockSpec((1, SSM_XBC), const),
                  pl.BlockSpec((1, LANE), const),
                  pl.BlockSpec((1, LANE), const),
                  pl.BlockSpec((1, WIDTH), const),
                  pl.BlockSpec((1, WIDTH), const)],
        out_specs=pl.BlockSpec((blk, WIDTH), lambda i, j: (row(i, j), 0)),
        out_shape=jax.ShapeDtypeStruct((b * s, WIDTH), F32),
        scratch_shapes=[pltpu.VMEM((blk + 2 * SUBLANE, SSM_XBC), F32),
                        pltpu.VMEM((blk, SSM_XBC), F32),
                        pltpu.VMEM((SSM_HEADS, SSM_STATE, SSM_HEAD_DIM), F32),
                        pltpu.VMEM((blk, WIDTH), F32)],
        compiler_params=_compiler_params(("parallel", "arbitrary")),
        name="ssd_branch",
    )(proj, proj, proj, conv_w, conv_b.reshape(1, -1), pad(dt_bias), pad(a_log), dsk,
      norm_w.reshape(1, -1))


def _lru_kernel(gate_ref, x_ref, cw_ref, cb_ref, wa_ref, ba_ref, wi_ref, bi_ref, lam_ref, o_ref,
                cbuf, a_s, u_s, carry_s, *, blk):
    @pl.when(pl.program_id(1) == 0)
    def _():
        cbuf[0:SUBLANE, :] = jnp.zeros((SUBLANE, WIDTH), F32)
        carry_s[...] = jnp.zeros_like(carry_s)

    cbuf[SUBLANE:SUBLANE + blk, :] = x_ref[...]
    xc = cb_ref[...]
    for k in range(LRU_CONV):
        s = LRU_CONV - 1 - k
        xc = xc + cw_ref[k:k + 1, :] * cbuf[SUBLANE - s:SUBLANE - s + blk, :]
    cbuf[0:SUBLANE, :] = cbuf[blk:blk + SUBLANE, :]

    nsl = wa_ref.shape[0]
    wsl = WIDTH // nsl
    xcb = xc.astype(BF16)
    ra = jnp.concatenate([jnp.dot(xcb[:, q * wsl:(q + 1) * wsl], wa_ref[q], preferred_element_type=F32)
                          for q in range(nsl)], axis=1)
    ri = jnp.concatenate([jnp.dot(xcb[:, q * wsl:(q + 1) * wsl], wi_ref[q], preferred_element_type=F32)
                          for q in range(nsl)], axis=1)
    r = jax.nn.sigmoid(ra + ba_ref[...])
    i = jax.nn.sigmoid(ri + bi_ref[...])
    log_a = -LRU_C * r * _softplus(-lam_ref[...])
    a = jnp.exp(log_a)
    a_s[...] = a
    u_s[...] = jnp.sqrt(-jnp.tanh(log_a) * (a * a + 1.0)) * (i * xc)

    rows = lax.broadcasted_iota(jnp.int32, (SUBLANE, WIDTH), 0)

    def group(g, carry):
        r0 = pl.multiple_of(g * SUBLANE, SUBLANE)
        a = a_s[pl.ds(r0, SUBLANE), :]
        u = u_s[pl.ds(r0, SUBLANE), :]
        for k in (1, 2, 4):
            keep = rows >= k
            a_prev = jnp.where(keep, pltpu.roll(a, k, 0), 1.0)
            u_prev = jnp.where(keep, pltpu.roll(u, k, 0), 0.0)
            u = a * u_prev + u
            a = a * a_prev
        h = a * carry + u
        u_s[pl.ds(r0, SUBLANE), :] = h
        return jnp.broadcast_to(h[SUBLANE - 1:SUBLANE, :], (SUBLANE, WIDTH))

    carry_s[...] = lax.fori_loop(0, blk // SUBLANE, group, carry_s[...])
    o_ref[...] = u_s[...] * jax.nn.gelu(gate_ref[...], approximate=True)


def _block_diag_slabs(w, per_slab):
    g, n, _ = w.shape
    w = w.reshape(g // per_slab, per_slab, n, n)
    eye = jnp.eye(per_slab, dtype=w.dtype)
    out = jnp.einsum("spij,pq->spiqj", w, eye)
    return out.reshape(g // per_slab, per_slab * n, per_slab * n)


def _lru_branch(proj, b, s, conv_w, conv_b, w_a, b_a, w_i, b_i, lam, *, blk=256):
    nblk = s // blk
    per_slab = 4
    wa = _block_diag_slabs(w_a, per_slab).astype(BF16)
    wi = _block_diag_slabs(w_i, per_slab).astype(BF16)
    row = lambda i, j: i * nblk + j
    const = lambda i, j: (0, 0)
    const3 = lambda i, j: (0, 0, 0)
    vec = lambda v: v.reshape(1, WIDTH)
    return pl.pallas_call(
        functools.partial(_lru_kernel, blk=blk),
        grid=(b, nblk),
        in_specs=[pl.BlockSpec((blk, WIDTH), lambda i, j: (row(i, j), _col_block("lgate"))),
                  pl.BlockSpec((blk, WIDTH), lambda i, j: (row(i, j), _col_block("lx"))),
                  pl.BlockSpec((LRU_CONV, WIDTH), const),
                  pl.BlockSpec((1, WIDTH), const),
                  pl.BlockSpec(wa.shape, const3),
                  pl.BlockSpec((1, WIDTH), const),
                  pl.BlockSpec(wi.shape, const3),
                  pl.BlockSpec((1, WIDTH), const),
                  pl.BlockSpec((1, WIDTH), const)],
        out_specs=pl.BlockSpec((blk, WIDTH), lambda i, j: (row(i, j), 0)),
        out_shape=jax.ShapeDtypeStruct((b * s, WIDTH), F32),
        scratch_shapes=[pltpu.VMEM((blk + 2 * SUBLANE, WIDTH), F32),
                        pltpu.VMEM((blk, WIDTH), F32),
                        pltpu.VMEM((blk, WIDTH), F32),
                        pltpu.VMEM((SUBLANE, WIDTH), F32)],
        compiler_params=_compiler_params(("parallel", "arbitrary")),
        name="rglru_branch",
    )(proj, proj, conv_w, vec(conv_b), wa, vec(b_a), wi, vec(b_i), vec(lam))


def _ret_kernel(q_ref, k_ref, v_ref, g_ref, cos_ref, sin_ref, inner_ref, kte_ref, qfs_ref, cdec_ref,
                nw_ref, o_ref, state, ybuf, *, blk):
    @pl.when(pl.program_id(1) == 0)
    def _():
        state[...] = jnp.zeros_like(state)

    lane = lax.broadcasted_iota(jnp.int32, (CHUNK, RET_QK), 1)
    first_half = (lane % RET_QK_DIM) < (RET_QK_DIM // 2)
    half = RET_QK_DIM // 2

    def rotary(x, cos, sin):
        swapped = jnp.where(first_half, pltpu.roll(x, RET_QK - half, 1), pltpu.roll(x, half, 1))
        return x * cos + swapped * sin

    def chunk(c, carry):
        r0 = pl.multiple_of(c * CHUNK, CHUNK)
        cos = cos_ref[pl.ds(r0, CHUNK), :]
        sin = sin_ref[pl.ds(r0, CHUNK), :]
        q = rotary(q_ref[pl.ds(r0, CHUNK), :], cos, sin)
        k = rotary(k_ref[pl.ds(r0, CHUNK), :], cos, sin) * (RET_QK_DIM ** -0.5)
        k_end = k * kte_ref[...]
        q_start = q * qfs_ref[...]
        for h in range(RET_HEADS):
            qs = slice(h * RET_QK_DIM, (h + 1) * RET_QK_DIM)
            vs = slice(h * RET_V_DIM, (h + 1) * RET_V_DIM)
            v = v_ref[pl.ds(r0, CHUNK), vs]
            scores = _bdot_nt(q[:, qs], k[:, qs]) * inner_ref[h]
            y = _bdot(scores, v)
            prev = state[h]
            y = y + _bdot(q_start[:, qs], prev)
            state[h] = prev * cdec_ref[:, vs] + _bdot_tn(k_end[:, qs], v)
            mu = jnp.mean(y, axis=-1, keepdims=True)
            yc = y - mu
            var = jnp.mean(yc * yc, axis=-1, keepdims=True)
            ybuf[pl.ds(r0, CHUNK), vs] = yc * lax.rsqrt(var + EPS)
        return carry

    lax.fori_loop(0, blk // CHUNK, chunk, 0)
    o_ref[...] = _silu(g_ref[...]) * (ybuf[...] * nw_ref[...])


def _ret_tables(s):
    pos_s = jnp.arange(s, dtype=F32)
    inv_freq = ROPE_BASE ** (-jnp.arange(0, RET_QK_DIM, 2, dtype=F32) / RET_QK_DIM)
    ang = pos_s[:, None] * inv_freq[None, :]
    cos, sin = jnp.cos(ang), jnp.sin(ang)
    cos_f = jnp.tile(jnp.concatenate([cos, cos], axis=1), (1, RET_HEADS))
    sin_f = jnp.tile(jnp.concatenate([-sin, sin], axis=1), (1, RET_HEADS))
    log_gamma = jnp.log1p(-jnp.exp2(-5.0 - jnp.arange(RET_HEADS, dtype=F32)))
    pos = jnp.arange(CHUNK, dtype=F32)
    inner = jnp.exp(log_gamma[:, None, None] * jnp.abs(pos[:, None] - pos[None, :]))
    k_to_end = jnp.exp(log_gamma[:, None] * (CHUNK - 1.0 - pos))
    q_from_start = jnp.exp(log_gamma[:, None] * (pos + 1.0))
    kte = jnp.repeat(k_to_end.T, RET_QK_DIM, axis=1)
    qfs = jnp.repeat(q_from_start.T, RET_QK_DIM, axis=1)
    cdec = jnp.repeat(jnp.exp(log_gamma * CHUNK), RET_V_DIM).reshape(1, WIDTH)
    return cos_f, sin_f, inner, kte, qfs, cdec


def _ret_branch(proj, b, s, norm_w, tables, *, blk=256):
    nblk = s // blk
    cos_f, sin_f, inner, kte, qfs, cdec = tables
    row = lambda i, j: i * nblk + j
    const = lambda i, j: (0, 0)
    return pl.pallas_call(
        functools.partial(_ret_kernel, blk=blk),
        grid=(b, nblk),
        in_specs=[pl.BlockSpec((blk, RET_QK), lambda i, j: (row(i, j), _col_block("rq"))),
                  pl.BlockSpec((blk, RET_QK), lambda i, j: (row(i, j), _col_block("rk"))),
                  pl.BlockSpec((blk, WIDTH), lambda i, j: (row(i, j), _col_block("rv"))),
                  pl.BlockSpec((blk, WIDTH), lambda i, j: (row(i, j), _col_block("rg"))),
                  pl.BlockSpec((blk, RET_QK), lambda i, j: (j, 0)),
                  pl.BlockSpec((blk, RET_QK), lambda i, j: (j, 0)),
                  pl.BlockSpec((RET_HEADS, CHUNK, CHUNK), lambda i, j: (0, 0, 0)),
                  pl.BlockSpec((CHUNK, RET_QK), const),
                  pl.BlockSpec((CHUNK, RET_QK), const),
                  pl.BlockSpec((1, WIDTH), const),
                  pl.BlockSpec((1, WIDTH), const)],
        out_specs=pl.BlockSpec((blk, WIDTH), lambda i, j: (row(i, j), 0)),
        out_shape=jax.ShapeDtypeStruct((b * s, WIDTH), F32),
        scratch_shapes=[pltpu.VMEM((RET_HEADS, RET_QK_DIM, RET_V_DIM), F32),
                        pltpu.VMEM((blk, WIDTH), F32)],
        compiler_params=_compiler_params(("parallel", "arbitrary")),
        name="retention_branch",
    )(proj, proj, proj, proj, cos_f, sin_f, inner, kte, qfs, cdec, norm_w.reshape(1, -1))


RWKV_PAIR = LANE // RWKV_HEAD


def _rwkv_kernel(r_ref, k_ref, v_ref, l_ref, mur_ref, muk_ref, muv_ref, mul_ref, w0_ref, w2_ref,
                 a0_ref, a2_ref, g2_ref, kk_ref, ka_ref, rk_ref, lnw_ref, lnb_ref, o_ref,
                 tail, r_s, k_s, v_s, kk_s, b_s, lw_s, o_s, state, *, blk):
    @pl.when(pl.program_id(2) == 0)
    def _():
        tail[...] = jnp.zeros_like(tail)
        state[...] = jnp.zeros_like(state)

    rows = lax.broadcasted_iota(jnp.int32, (blk, 1), 0)

    def shift_mix(x, prev_row, mu):
        prev = jnp.where(rows == 0, prev_row, pltpu.roll(x, 1, 0))
        return x + (prev - x) * mu

    lane = lax.broadcasted_iota(jnp.int32, (LANE, LANE), 1) // RWKV_HEAD
    lrow = lax.broadcasted_iota(jnp.int32, (LANE, LANE), 0) // RWKV_HEAD
    head_ones = (lane == lrow).astype(BF16)

    def head_sum(x):
        hi = x.astype(BF16)
        lo = (x - hi.astype(F32)).astype(BF16)
        return (jnp.dot(hi, head_ones, preferred_element_type=F32)
                + jnp.dot(lo, head_ones, preferred_element_type=F32))

    r_raw, k_raw, v_raw, l_raw = r_ref[...], k_ref[...], v_ref[...], l_ref[...]
    r = shift_mix(r_raw, tail[0:1, 0:LANE], mur_ref[...])
    k = shift_mix(k_raw, tail[1:2, 0:LANE], muk_ref[...])
    v = shift_mix(v_raw, tail[2:3, 0:LANE], muv_ref[...])
    lo_ = shift_mix(l_raw, tail[3:4, :], mul_ref[...])
    tail[0:1, 0:LANE] = r_raw[blk - 1:blk, :]
    tail[1:2, 0:LANE] = k_raw[blk - 1:blk, :]
    tail[2:3, 0:LANE] = v_raw[blk - 1:blk, :]
    tail[3:4, :] = l_raw[blk - 1:blk, :]

    wl = lo_[:, 0:RWKV_W_LORA]
    al = lo_[:, RWKV_W_LORA:RWKV_W_LORA + RWKV_A_LORA]
    gl = lo_[:, RWKV_W_LORA + RWKV_A_LORA:]
    w = -_softplus(-(w0_ref[...] + _bdot(jnp.tanh(wl), w2_ref[...]))) - 0.5
    a = jax.nn.sigmoid(a0_ref[...] + _bdot(al, a2_ref[...]))
    gate = _bdot(jax.nn.sigmoid(gl), g2_ref[...])
    kk = k * kk_ref[...]
    kk = kk / jnp.maximum(jnp.sqrt(head_sum(kk * kk)), 1e-12)
    k = k * (1.0 + (a - 1.0) * ka_ref[...])
    r_s[...] = r
    k_s[...] = k
    v_s[...] = v
    kk_s[...] = kk
    b_s[...] = kk * a
    lw_s[...] = -jnp.exp(w)

    trow = lax.broadcasted_iota(jnp.int32, (CHUNK, LANE), 0)
    tcol = lax.broadcasted_iota(jnp.int32, (CHUNK, LANE), 1) % CHUNK
    strict = tcol < trow
    incl = tcol <= trow

    def chunk(c, carry):
        r0 = pl.multiple_of(c * CHUNK, CHUNK)
        sl = pl.ds(r0, CHUNK)
        lw = lw_s[sl, :]
        cum = _cumsum_rows(lw)
        w_in = jnp.exp(cum)
        w_inv = jnp.exp(-cum)
        cum_last = cum[CHUNK - 1:CHUNK, :]
        rt = r_s[sl, :] * w_in
        kt = k_s[sl, :] * w_inv
        bt = b_s[sl, :] * w_inv
        kap = kk_s[sl, :] * jnp.exp(cum - lw)
        to_end = jnp.exp(cum_last - cum)
        k_end = k_s[sl, :] * to_end
        b_end = b_s[sl, :] * to_end
        w_all = jnp.exp(cum_last)
        vv = v_s[sl, :]
        for p in range(RWKV_PAIR):
            hs = slice(p * RWKV_HEAD, (p + 1) * RWKV_HEAD)
            s0 = state[p]
            vh = vv[:, hs]
            lhs = jnp.concatenate([kap[:, hs], rt[:, hs]], axis=0)
            rhs = jnp.concatenate([bt[:, hs], kt[:, hs]], axis=0)
            pm = _bdot_nt(lhs, rhs)
            top = jnp.where(strict, pm[0:CHUNK, :], 0.0)
            bot = jnp.where(incl, pm[CHUNK:, :], 0.0)
            nmat = -top[:, 0:CHUNK]
            x = _bdot(top[:, CHUNK:], vh) + _bdot_nt(kap[:, hs], s0)
            span = 1
            while True:
                x = x + _bdot(nmat, x)
                span *= 2
                if span >= CHUNK:
                    break
                nmat = _bdot(nmat, nmat)
            u = -x
            uv = jnp.concatenate([u, vh], axis=0)
            o_s[sl, hs] = _bdot(bot, uv) + _bdot_nt(rt[:, hs], s0)
            end = jnp.concatenate([b_end[:, hs], k_end[:, hs]], axis=0)
            state[p] = s0 * w_all[:, hs] + _bdot_tn(uv, end)
        return carry

    lax.fori_loop(0, blk // CHUNK, chunk, 0)

    o = o_s[...]
    mu = head_sum(o) * (1.0 / RWKV_HEAD)
    oc = o - mu
    var = head_sum(oc * oc) * (1.0 / RWKV_HEAD)
    o = oc * lax.rsqrt(var + RWKV_LN_EPS) * lnw_ref[...] + lnb_ref[...]
    o = o + head_sum(r * k * rk_ref[...]) * v
    o_ref[...] = o * gate


def _rwkv_branch(proj, b, s, mu, w0, w2, a0, a2, g2, k_k, k_a, r_k, ln_w, ln_b, *, blk=256):
    nblk = s // blk
    npair = RWKV_HEADS // RWKV_PAIR
    row = lambda i, p, j: i * nblk + j
    vec = lambda v: v.reshape(1, -1)
    hcol = lambda i, p, j: (0, p)
    mu_r, mu_k, mu_v, mu_l = (mu[0:WIDTH], mu[WIDTH:2 * WIDTH], mu[2 * WIDTH:3 * WIDTH],
                              mu[3 * WIDTH:])
    wblk = _col_block("wr")
    lblk = _col_block("lora")
    return pl.pallas_call(
        functools.partial(_rwkv_kernel, blk=blk),
        grid=(b, npair, nblk),
        in_specs=[pl.BlockSpec((blk, LANE), lambda i, p, j: (row(i, p, j), wblk * (WIDTH // LANE) + p)),
                  pl.BlockSpec((blk, LANE), lambda i, p, j: (row(i, p, j), (wblk + 1) * (WIDTH // LANE) + p)),
                  pl.BlockSpec((blk, LANE), lambda i, p, j: (row(i, p, j), (wblk + 2) * (WIDTH // LANE) + p)),
                  pl.BlockSpec((blk, RWKV_LORA), lambda i, p, j: (row(i, p, j), lblk)),
                  pl.BlockSpec((1, LANE), hcol),
                  pl.BlockSpec((1, LANE), hcol),
                  pl.BlockSpec((1, LANE), hcol),
                  pl.BlockSpec((1, RWKV_LORA), lambda i, p, j: (0, 0)),
                  pl.BlockSpec((1, LANE), hcol),
                  pl.BlockSpec((RWKV_W_LORA, LANE), hcol),
                  pl.BlockSpec((1, LANE), hcol),
                  pl.BlockSpec((RWKV_A_LORA, LANE), hcol),
                  pl.BlockSpec((RWKV_G_LORA, LANE), hcol),
                  pl.BlockSpec((1, LANE), hcol),
                  pl.BlockSpec((1, LANE), hcol),
                  pl.BlockSpec((1, LANE), hcol),
                  pl.BlockSpec((1, LANE), hcol),
                  pl.BlockSpec((1, LANE), hcol)],
        out_specs=pl.BlockSpec((blk, LANE), lambda i, p, j: (row(i, p, j), p)),
        out_shape=jax.ShapeDtypeStruct((b * s, WIDTH), F32),
        scratch_shapes=[pltpu.VMEM((SUBLANE, RWKV_LORA), F32)]
        + [pltpu.VMEM((blk, LANE), F32)] * 7
        + [pltpu.VMEM((RWKV_PAIR, RWKV_HEAD, RWKV_HEAD), F32)],
        compiler_params=_compiler_params(("parallel", "parallel", "arbitrary")),
        name="rwkv7_branch",
    )(proj, proj, proj, proj, vec(mu_r), vec(mu_k), vec(mu_v), vec(mu_l), vec(w0), w2.astype(BF16),
      vec(a0), a2.astype(BF16), g2.astype(BF16), vec(k_k), vec(k_a), vec(r_k), vec(ln_w), vec(ln_b))


def _merge_kernel(y0_ref, y1_ref, y2_ref, y3_ref, gl_ref, x_ref, wb_ref, wo_ref, o_ref):
    merged = None
    for m, y_ref in enumerate((y0_ref, y1_ref, y2_ref, y3_ref)):
        br = jnp.dot(y_ref[...].astype(BF16), wb_ref[m], preferred_element_type=F32)
        term = jax.nn.sigmoid(gl_ref[:, m * D_MODEL:(m + 1) * D_MODEL]) * br
        merged = term if merged is None else merged + term
    o_ref[...] = x_ref[...] + jnp.dot(merged.astype(BF16), wo_ref[...], preferred_element_type=F32)


def _merge(ys, proj, x2, w_branch, w_out, *, tm=256):
    t = x2.shape[0]
    tok = lambda i: (i, 0)
    return pl.pallas_call(
        _merge_kernel,
        grid=(t // tm,),
        in_specs=[pl.BlockSpec((tm, WIDTH), tok)] * N_BRANCH
        + [pl.BlockSpec((tm, N_BRANCH * D_MODEL), lambda i: (i, _col_block("gates"))),
           pl.BlockSpec((tm, D_MODEL), tok),
           pl.BlockSpec((N_BRANCH, WIDTH, D_MODEL), lambda i: (0, 0, 0)),
           pl.BlockSpec((D_MODEL, D_MODEL), lambda i: (0, 0))],
        out_specs=pl.BlockSpec((tm, D_MODEL), tok),
        out_shape=jax.ShapeDtypeStruct((t, D_MODEL), F32),
        compiler_params=_compiler_params(("parallel",)),
        name="branch_merge",
    )(*ys, proj, x2, w_branch.astype(BF16), w_out.astype(BF16))


FFN_TC = 256


def _ffn_kernel(x_ref, g_ref, wu_ref, cw_ref, cb_ref, wd_ref, gf_ref, o_ref, cbuf, tail, *, tm, final):
    @pl.when(pl.program_id(1) == 0)
    def _():
        tail[...] = jnp.zeros_like(tail)

    x = x_ref[...]
    ms = jnp.mean(x * x, axis=-1, keepdims=True)
    hn = (x * lax.rsqrt(ms + EPS) * g_ref[...]).astype(BF16)
    acc = x
    for i in range(D_FF // FFN_TC):
        halves = []
        for half in range(2):
            c0 = half * D_FF + i * FFN_TC
            cs = slice(c0, c0 + FFN_TC)
            cbuf[0:SUBLANE, :] = tail[:, cs]
            cbuf[SUBLANE:SUBLANE + tm, :] = jnp.dot(hn, wu_ref[:, cs], preferred_element_type=F32)
            y = cb_ref[:, cs]
            for k in range(FFN_CONV):
                s = FFN_CONV - 1 - k
                y = y + cw_ref[k:k + 1, cs] * cbuf[SUBLANE - s:SUBLANE - s + tm, :]
            tail[:, cs] = cbuf[tm:tm + SUBLANE, :]
            halves.append(y)
        val, gt = halves
        acc = acc + jnp.dot((_silu(gt) * val).astype(BF16), wd_ref[i * FFN_TC:(i + 1) * FFN_TC, :],
                            preferred_element_type=F32)
    if final:
        ms = jnp.mean(acc * acc, axis=-1, keepdims=True)
        acc = acc * lax.rsqrt(ms + EPS) * gf_ref[...]
    o_ref[...] = acc


def _conv_ffn(x2, b, s, gain, w_up, conv_w, conv_b, w_down, gain_final, *, final, tm=256):
    nblk = s // tm
    row = lambda i, j: (i * nblk + j, 0)
    const = lambda i, j: (0, 0)
    return pl.pallas_call(
        functools.partial(_ffn_kernel, tm=tm, final=final),
        grid=(b, nblk),
        in_specs=[pl.BlockSpec((tm, D_MODEL), row),
                  pl.BlockSpec((1, D_MODEL), const),
                  pl.BlockSpec((D_MODEL, 2 * D_FF), const),
                  pl.BlockSpec((FFN_CONV, 2 * D_FF), const),
                  pl.BlockSpec((1, 2 * D_FF), const),
                  pl.BlockSpec((D_FF, D_MODEL), const),
                  pl.BlockSpec((1, D_MODEL), const)],
        out_specs=pl.BlockSpec((tm, D_MODEL), row),
        out_shape=jax.ShapeDtypeStruct((b * s, D_MODEL), F32),
        scratch_shapes=[pltpu.VMEM((tm + 2 * SUBLANE, FFN_TC), F32),
                        pltpu.VMEM((SUBLANE, 2 * D_FF), F32)],
        compiler_params=_compiler_params(("parallel", "arbitrary")),
        name="conv_ffn",
    )(x2, gain.reshape(1, -1), w_up.astype(BF16), conv_w, conv_b.reshape(1, -1),
      w_down.astype(BF16), gain_final.reshape(1, -1))


def _reorder_w_in(w):
    cols = []
    for n in _DST_ORDER:
        off, wd = _SRC[n]
        cols.append(w[:, off:off + wd])
    pad = N_PROJ - _DST["dt"][0] - SSM_HEADS
    cols.append(jnp.zeros((w.shape[0], pad), w.dtype))
    return jnp.concatenate(cols, axis=1).astype(BF16)


def kernel(x, norm_mix, w_in, ssm_conv_w, ssm_conv_b, ssm_dt_bias, ssm_a_log, ssm_d, ssm_norm, lru_conv_w, lru_conv_b, lru_w_a, lru_b_a, lru_w_i, lru_b_i, lru_lam, ret_norm, rwkv_mu, rwkv_w0, rwkv_w2, rwkv_a0, rwkv_a2, rwkv_g2, rwkv_k_k, rwkv_k_a, rwkv_r_k, rwkv_ln_w, rwkv_ln_b, w_branch, w_out, norm_ffn, ffn_up, ffn_conv_w, ffn_conv_b, ffn_down, norm_final):
    b, s, d = x.shape
    x2 = x.reshape(b * s, d)
    tables = _ret_tables(s)
    for l in range(DEPTH):
        proj = _norm_matmul(x2, norm_mix[l], _reorder_w_in(w_in[l]))
        y_ssd = _ssd_branch(proj, b, s, ssm_conv_w[l], ssm_conv_b[l], ssm_dt_bias[l], ssm_a_log[l],
                            ssm_d[l], ssm_norm[l])
        y_lru = _lru_branch(proj, b, s, lru_conv_w[l], lru_conv_b[l], lru_w_a[l], lru_b_a[l],
                            lru_w_i[l], lru_b_i[l], lru_lam[l])
        y_ret = _ret_branch(proj, b, s, ret_norm[l], tables)
        y_rwkv = _rwkv_branch(proj, b, s, rwkv_mu[l], rwkv_w0[l], rwkv_w2[l], rwkv_a0[l], rwkv_a2[l],
                              rwkv_g2[l], rwkv_k_k[l], rwkv_k_a[l], rwkv_r_k[l], rwkv_ln_w[l],
                              rwkv_ln_b[l])
        x2 = _merge((y_ssd, y_lru, y_ret, y_rwkv), proj, x2, w_branch[l], w_out[l])
        x2 = _conv_ffn(x2, b, s, norm_ffn[l], ffn_up[l], ffn_conv_w[l], ffn_conv_b[l], ffn_down[l],
                       norm_final, final=(l == DEPTH - 1))
    return x2.reshape(b, s, d)
```

```python
import functools

import jax
import jax.numpy as jnp
from jax import lax
from jax.experimental import pallas as pl
from jax.experimental.pallas import tpu as pltpu

F32 = jnp.float32
BF16 = jnp.bfloat16

D_MODEL = 1024
WIDTH = 1024
DEPTH = 2
CHUNK = 64
EPS = 1e-6

SSM_HEADS = 16
SSM_HEAD_DIM = 64
SSM_GROUPS = 4
SSM_STATE = 128
SSM_CONV = 4
SSM_XBC = WIDTH + 2 * SSM_GROUPS * SSM_STATE

LRU_BLOCKS = 16
LRU_BLOCK = 64
LRU_CONV = 4
LRU_C = 8.0

RET_HEADS = 8
RET_QK_DIM = 64
RET_V_DIM = 128
RET_QK = RET_HEADS * RET_QK_DIM
ROPE_BASE = 10000.0

RWKV_HEAD = 64
RWKV_HEADS = 16
RWKV_W_LORA = 64
RWKV_A_LORA = 64
RWKV_G_LORA = 128
RWKV_LORA = RWKV_W_LORA + RWKV_A_LORA + RWKV_G_LORA
RWKV_LN_EPS = 64e-5

D_FF = 2816
FFN_CONV = 3
N_BRANCH = 4

LANE = 128
SUBLANE = 8
VMEM_LIMIT = 56 * 1024 * 1024

_SRC = {}
_o = 0
for _n, _w in (("z", WIDTH), ("xbc", SSM_XBC), ("dt", SSM_HEADS), ("lgate", WIDTH), ("lx", WIDTH),
               ("rq", RET_QK), ("rk", RET_QK), ("rv", WIDTH), ("rg", WIDTH),
               ("wr", WIDTH), ("wk", WIDTH), ("wv", WIDTH), ("lora", RWKV_LORA),
               ("gates", N_BRANCH * D_MODEL)):
    _SRC[_n] = (_o, _w)
    _o += _w
N_IN = _o
_DST_ORDER = ("gates", "xbc", "z", "lgate", "lx", "rv", "rg", "rq", "rk", "wr", "wk", "wv", "lora", "dt")
_DST = {}
_o = 0
for _n in _DST_ORDER:
    _w = _SRC[_n][1]
    _bw = max(_w, LANE)
    assert _o % _bw == 0, (_n, _o, _bw)
    _DST[_n] = (_o, _bw)
    _o += _bw
PROJ_TN = 512
N_PROJ = -(-_o // PROJ_TN) * PROJ_TN


def _col_block(name):
    off, bw = _DST[name]
    return off // bw


def _softplus(x):
    return jnp.maximum(x, 0.0) + jnp.log1p(jnp.exp(-jnp.abs(x)))


def _silu(x):
    return x * jax.nn.sigmoid(x)


def _bdot(a, b):
    return jnp.dot(a.astype(BF16), b.astype(BF16), preferred_element_type=F32)


def _bdot_nt(a, b):
    return lax.dot_general(a.astype(BF16), b.astype(BF16), (((1,), (1,)), ((), ())),
                           preferred_element_type=F32)


def _bdot_tn(a, b):
    return lax.dot_general(a.astype(BF16), b.astype(BF16), (((0,), (0,)), ((), ())),
                           preferred_element_type=F32)


def _cumsum_rows(x):
    n = x.shape[0]
    row = lax.broadcasted_iota(jnp.int32, (n, n), 0)
    col = lax.broadcasted_iota(jnp.int32, (n, n), 1)
    tri = (col <= row).astype(F32)
    return jnp.dot(tri, x, precision=lax.Precision.HIGHEST, preferred_element_type=F32)


def _compiler_params(sem):
    return pltpu.CompilerParams(dimension_semantics=sem, vmem_limit_bytes=VMEM_LIMIT)


def _norm_matmul_kernel(x_ref, g_ref, w_ref, o_ref, hn_ref):
    @pl.when(pl.program_id(1) == 0)
    def _():
        x = x_ref[...]
        ms = jnp.mean(x * x, axis=-1, keepdims=True)
        hn_ref[...] = (x * lax.rsqrt(ms + EPS) * g_ref[...]).astype(BF16)

    o_ref[...] = jnp.dot(hn_ref[...], w_ref[...], preferred_element_type=F32)


def _norm_matmul(x2, gain, w, *, tm=1024, tn=PROJ_TN):
    t, d = x2.shape
    n = w.shape[1]
    return pl.pallas_call(
        _norm_matmul_kernel,
        grid=(t // tm, n // tn),
        in_specs=[pl.BlockSpec((tm, d), lambda i, j: (i, 0)),
                  pl.BlockSpec((1, d), lambda i, j: (0, 0)),
                  pl.BlockSpec((d, tn), lambda i, j: (0, j))],
        out_specs=pl.BlockSpec((tm, tn), lambda i, j: (i, j)),
        out_shape=jax.ShapeDtypeStruct((t, n), F32),
        scratch_shapes=[pltpu.VMEM((tm, d), BF16)],
        compiler_params=_compiler_params(("parallel", "arbitrary")),
        name="norm_in_proj",
    )(x2, gain.reshape(1, d), w)


def _ssd_kernel(z_ref, xbc_ref, dt_ref, cw_ref, cb_ref, dtb_ref, alog_ref, dsk_ref, nw_ref, o_ref,
                cbuf, act, state, ybuf, *, blk):
    @pl.when(pl.program_id(1) == 0)
    def _():
        cbuf[0:SUBLANE, :] = jnp.zeros((SUBLANE, SSM_XBC), F32)
        state[...] = jnp.zeros_like(state)

    cbuf[SUBLANE:SUBLANE + blk, :] = xbc_ref[...]
    conv = cb_ref[...]
    for k in range(SSM_CONV):
        s = SSM_CONV - 1 - k
        conv = conv + cw_ref[k:k + 1, :] * cbuf[SUBLANE - s:SUBLANE - s + blk, :]
    act[...] = _silu(conv)
    cbuf[0:SUBLANE, :] = cbuf[blk:blk + SUBLANE, :]

    a_neg = -jnp.exp(alog_ref[...])
    row = lax.broadcasted_iota(jnp.int32, (CHUNK, CHUNK), 0)
    col = lax.broadcasted_iota(jnp.int32, (CHUNK, CHUNK), 1)
    causal = col <= row
    hg = SSM_HEADS // SSM_GROUPS

    def chunk(c, carry):
        r0 = pl.multiple_of(c * CHUNK, CHUNK)
        dt = _softplus(dt_ref[pl.ds(r0, CHUNK), :] + dtb_ref[...])
        a_cum = _cumsum_rows(dt * a_neg)
        a_cum_t = a_cum.T
        a_last = a_cum[CHUNK - 1:CHUNK, :]
        to_end = jnp.exp(a_last - a_cum)
        from_start = jnp.exp(a_cum)
        chunk_decay = jnp.exp(a_last)
        for g in range(SSM_GROUPS):
            bm = act[pl.ds(r0, CHUNK), WIDTH + g * SSM_STATE:WIDTH + (g + 1) * SSM_STATE]
            cm = act[pl.ds(r0, CHUNK), WIDTH + (SSM_GROUPS + g) * SSM_STATE:
                     WIDTH + (SSM_GROUPS + g + 1) * SSM_STATE]
            cb = _bdot_nt(cm, bm)
            bm_t = bm.T
            for i in range(hg):
                h = g * hg + i
                xs = act[pl.ds(r0, CHUNK), h * SSM_HEAD_DIM:(h + 1) * SSM_HEAD_DIM]
                xdt = xs * dt[:, h:h + 1]
                seg = a_cum[:, h:h + 1] - a_cum_t[h:h + 1, :]
                decay = jnp.exp(jnp.where(causal, seg, -jnp.inf))
                y = _bdot(cb * decay, xdt)
                prev = state[h]
                y = y + _bdot(cm, prev) * from_start[:, h:h + 1]
                y = y + xs * dsk_ref[:, h * SSM_HEAD_DIM:(h + 1) * SSM_HEAD_DIM]
                new = _bdot(bm_t, xdt * to_end[:, h:h + 1])
                state[h] = prev * chunk_decay[:, h:h + 1] + new
                ybuf[pl.ds(r0, CHUNK), h * SSM_HEAD_DIM:(h + 1) * SSM_HEAD_DIM] = y
        return carry

    lax.fori_loop(0, blk // CHUNK, chunk, 0)

    y = ybuf[...] * _silu(z_ref[...])
    ms = jnp.mean(y * y, axis=-1, keepdims=True)
    o_ref[...] = y * lax.rsqrt(ms + EPS) * nw_ref[...]


def _ssd_branch(proj, b, s, conv_w, conv_b, dt_bias, a_log, d_skip, norm_w, *, blk=256):
    nblk = s // blk
    pad = lambda v: jnp.pad(v.reshape(1, -1), ((0, 0), (0, LANE - v.shape[0])))
    dsk = jnp.repeat(d_skip, SSM_HEAD_DIM).reshape(1, WIDTH)
    row = lambda i, j: i * nblk + j
    const = lambda i, j: (0, 0)
    return pl.pallas_call(
        functools.partial(_ssd_kernel, blk=blk),
        grid=(b, nblk),
        in_specs=[pl.BlockSpec((blk, WIDTH), lambda i, j: (row(i, j), _col_block("z"))),
                  pl.BlockSpec((blk, SSM_XBC), lambda i, j: (row(i, j), _col_block("xbc"))),
                  pl.BlockSpec((blk, LANE), lambda i, j: (row(i, j), _col_block("dt"))),
                  pl.BlockSpec((SSM_CONV, SSM_XBC), const),
                  pl.BlockSpec((1, SSM_XBC), const),
                  pl.BlockSpec((1, LANE), const),
                  pl.BlockSpec((1, LANE), const),
                  pl.BlockSpec((1, WIDTH), const),
                  pl.BlockSpec((1, WIDTH), const)],
        out_specs=pl.BlockSpec((blk, WIDTH), lambda i, j: (row(i, j), 0)),
        out_shape=jax.ShapeDtypeStruct((b * s, WIDTH), F32),
        scratch_shapes=[pltpu.VMEM((blk + 2 * SUBLANE, SSM_XBC), F32),
                        pltpu.VMEM((blk, SSM_XBC), F32),
                        pltpu.VMEM((SSM_HEADS, SSM_STATE, SSM_HEAD_DIM), F32),
                        pltpu.VMEM((blk, WIDTH), F32)],
        compiler_params=_compiler_params(("parallel", "arbitrary")),
        name="ssd_branch",
    )(proj, proj, proj, conv_w, conv_b.reshape(1, -1), pad(dt_bias), pad(a_log), dsk,
      norm_w.reshape(1, -1))


def _lru_kernel(gate_ref, x_ref, cw_ref, cb_ref, wa_ref, ba_ref, wi_ref, bi_ref, lam_ref, o_ref,
                cbuf, a_s, u_s, carry_s, *, blk):
    @pl.when(pl.program_id(1) == 0)
    def _():
        cbuf[0:SUBLANE, :] = jnp.zeros((SUBLANE, WIDTH), F32)
        carry_s[...] = jnp.zeros_like(carry_s)

    cbuf[SUBLANE:SUBLANE + blk, :] = x_ref[...]
    xc = cb_ref[...]
    for k in range(LRU_CONV):
        s = LRU_CONV - 1 - k
        xc = xc + cw_ref[k:k + 1, :] * cbuf[SUBLANE - s:SUBLANE - s + blk, :]
    cbuf[0:SUBLANE, :] = cbuf[blk:blk + SUBLANE, :]

    nsl = wa_ref.shape[0]
    wsl = WIDTH // nsl
    xcb = xc.astype(BF16)
    ra = jnp.concatenate([jnp.dot(xcb[:, q * wsl:(q + 1) * wsl], wa_ref[q], preferred_element_type=F32)
                          for q in range(nsl)], axis=1)
    ri = jnp.concatenate([jnp.dot(xcb[:, q * wsl:(q + 1) * wsl], wi_ref[q], preferred_element_type=F32)
                          for q in range(nsl)], axis=1)
    r = jax.nn.sigmoid(ra + ba_ref[...])
    i = jax.nn.sigmoid(ri + bi_ref[...])
    log_a = -LRU_C * r * _softplus(-lam_ref[...])
    a = jnp.exp(log_a)
    a_s[...] = a
    u_s[...] = jnp.sqrt(-jnp.tanh(log_a) * (a * a + 1.0)) * (i * xc)

    rows = lax.broadcasted_iota(jnp.int32, (SUBLANE, WIDTH), 0)

    def group(g, carry):
        r0 = pl.multiple_of(g * SUBLANE, SUBLANE)
        a = a_s[pl.ds(r0, SUBLANE), :]
        u = u_s[pl.ds(r0, SUBLANE), :]
        for k in (1, 2, 4):
            keep = rows >= k
            a_prev = jnp.where(keep, pltpu.roll(a, k, 0), 1.0)
            u_prev = jnp.where(keep, pltpu.roll(u, k, 0), 0.0)
            u = a * u_prev + u
            a = a * a_prev
        h = a * carry + u
        u_s[pl.ds(r0, SUBLANE), :] = h
        return jnp.broadcast_to(h[SUBLANE - 1:SUBLANE, :], (SUBLANE, WIDTH))

    carry_s[...] = lax.fori_loop(0, blk // SUBLANE, group, carry_s[...])
    o_ref[...] = u_s[...] * jax.nn.gelu(gate_ref[...], approximate=True)


def _block_diag_slabs(w, per_slab):
    g, n, _ = w.shape
    w = w.reshape(g // per_slab, per_slab, n, n)
    eye = jnp.eye(per_slab, dtype=w.dtype)
    out = jnp.einsum("spij,pq->spiqj", w, eye)
    return out.reshape(g // per_slab, per_slab * n, per_slab * n)


def _lru_branch(proj, b, s, conv_w, conv_b, w_a, b_a, w_i, b_i, lam, *, blk=256):
    nblk = s // blk
    per_slab = 4
    wa = _block_diag_slabs(w_a, per_slab).astype(BF16)
    wi = _block_diag_slabs(w_i, per_slab).astype(BF16)
    row = lambda i, j: i * nblk + j
    const = lambda i, j: (0, 0)
    const3 = lambda i, j: (0, 0, 0)
    vec = lambda v: v.reshape(1, WIDTH)
    return pl.pallas_call(
        functools.partial(_lru_kernel, blk=blk),
        grid=(b, nblk),
        in_specs=[pl.BlockSpec((blk, WIDTH), lambda i, j: (row(i, j), _col_block("lgate"))),
                  pl.BlockSpec((blk, WIDTH), lambda i, j: (row(i, j), _col_block("lx"))),
                  pl.BlockSpec((LRU_CONV, WIDTH), const),
                  pl.BlockSpec((1, WIDTH), const),
                  pl.BlockSpec(wa.shape, const3),
                  pl.BlockSpec((1, WIDTH), const),
                  pl.BlockSpec(wi.shape, const3),
                  pl.BlockSpec((1, WIDTH), const),
                  pl.BlockSpec((1, WIDTH), const)],
        out_specs=pl.BlockSpec((blk, WIDTH), lambda i, j: (row(i, j), 0)),
        out_shape=jax.ShapeDtypeStruct((b * s, WIDTH), F32),
        scratch_shapes=[pltpu.VMEM((blk + 2 * SUBLANE, WIDTH), F32),
                        pltpu.VMEM((blk, WIDTH), F32),
                        pltpu.VMEM((blk, WIDTH), F32),
                        pltpu.VMEM((SUBLANE, WIDTH), F32)],
        compiler_params=_compiler_params(("parallel", "arbitrary")),
        name="rglru_branch",
    )(proj, proj, conv_w, vec(conv_b), wa, vec(b_a), wi, vec(b_i), vec(lam))


def _ret_kernel(q_ref, k_ref, v_ref, g_ref, cos_ref, sin_ref, inner_ref, kte_ref, qfs_ref, cdec_ref,
                nw_ref, o_ref, state, ybuf, *, blk):
    @pl.when(pl.program_id(1) == 0)
    def _():
        state[...] = jnp.zeros_like(state)

    lane = lax.broadcasted_iota(jnp.int32, (CHUNK, RET_QK), 1)
    first_half = (lane % RET_QK_DIM) < (RET_QK_DIM // 2)
    half = RET_QK_DIM // 2

    def rotary(x, cos, sin):
        swapped = jnp.where(first_half, pltpu.roll(x, RET_QK - half, 1), pltpu.roll(x, half, 1))
        return x * cos + swapped * sin

    def chunk(c, carry):
        r0 = pl.multiple_of(c * CHUNK, CHUNK)
        cos = cos_ref[pl.ds(r0, CHUNK), :]
        sin = sin_ref[pl.ds(r0, CHUNK), :]
        q = rotary(q_ref[pl.ds(r0, CHUNK), :], cos, sin)
        k = rotary(k_ref[pl.ds(r0, CHUNK), :], cos, sin) * (RET_QK_DIM ** -0.5)
        k_end = k * kte_ref[...]
        q_start = q * qfs_ref[...]
        for h in range(RET_HEADS):
            qs = slice(h * RET_QK_DIM, (h + 1) * RET_QK_DIM)
            vs = slice(h * RET_V_DIM, (h + 1) * RET_V_DIM)
            v = v_ref[pl.ds(r0, CHUNK), vs]
            scores = _bdot_nt(q[:, qs], k[:, qs]) * inner_ref[h]
            y = _bdot(scores, v)
            prev = state[h]
            y = y + _bdot(q_start[:, qs], prev)
            state[h] = prev * cdec_ref[:, vs] + _bdot_tn(k_end[:, qs], v)
            mu = jnp.mean(y, axis=-1, keepdims=True)
            yc = y - mu
            var = jnp.mean(yc * yc, axis=-1, keepdims=True)
            ybuf[pl.ds(r0, CHUNK), vs] = yc * lax.rsqrt(var + EPS)
        return carry

    lax.fori_loop(0, blk // CHUNK, chunk, 0)
    o_ref[...] = _silu(g_ref[...]) * (ybuf[...] * nw_ref[...])


def _ret_tables(s):
    pos_s = jnp.arange(s, dtype=F32)
    inv_freq = ROPE_BASE ** (-jnp.arange(0, RET_QK_DIM, 2, dtype=F32) / RET_QK_DIM)
    ang = pos_s[:, None] * inv_freq[None, :]
    cos, sin = jnp.cos(ang), jnp.sin(ang)
    cos_f = jnp.tile(jnp.concatenate([cos, cos], axis=1), (1, RET_HEADS))
    sin_f = jnp.tile(jnp.concatenate([-sin, sin], axis=1), (1, RET_HEADS))
    log_gamma = jnp.log1p(-jnp.exp2(-5.0 - jnp.arange(RET_HEADS, dtype=F32)))
    pos = jnp.arange(CHUNK, dtype=F32)
    inner = jnp.exp(log_gamma[:, None, None] * jnp.abs(pos[:, None] - pos[None, :]))
    k_to_end = jnp.exp(log_gamma[:, None] * (CHUNK - 1.0 - pos))
    q_from_start = jnp.exp(log_gamma[:, None] * (pos + 1.0))
    kte = jnp.repeat(k_to_end.T, RET_QK_DIM, axis=1)
    qfs = jnp.repeat(q_from_start.T, RET_QK_DIM, axis=1)
    cdec = jnp.repeat(jnp.exp(log_gamma * CHUNK), RET_V_DIM).reshape(1, WIDTH)
    return cos_f, sin_f, inner, kte, qfs, cdec


def _ret_branch(proj, b, s, norm_w, tables, *, blk=256):
    nblk = s // blk
    cos_f, sin_f, inner, kte, qfs, cdec = tables
    row = lambda i, j: i * nblk + j
    const = lambda i, j: (0, 0)
    return pl.pallas_call(
        functools.partial(_ret_kernel, blk=blk),
        grid=(b, nblk),
        in_specs=[pl.BlockSpec((blk, RET_QK), lambda i, j: (row(i, j), _col_block("rq"))),
                  pl.BlockSpec((blk, RET_QK), lambda i, j: (row(i, j), _col_block("rk"))),
                  pl.BlockSpec((blk, WIDTH), lambda i, j: (row(i, j), _col_block("rv"))),
                  pl.BlockSpec((blk, WIDTH), lambda i, j: (row(i, j), _col_block("rg"))),
                  pl.BlockSpec((blk, RET_QK), lambda i, j: (j, 0)),
                  pl.BlockSpec((blk, RET_QK), lambda i, j: (j, 0)),
                  pl.BlockSpec((RET_HEADS, CHUNK, CHUNK), lambda i, j: (0, 0, 0)),
                  pl.BlockSpec((CHUNK, RET_QK), const),
                  pl.BlockSpec((CHUNK, RET_QK), const),
                  pl.BlockSpec((1, WIDTH), const),
                  pl.BlockSpec((1, WIDTH), const)],
        out_specs=pl.BlockSpec((blk, WIDTH), lambda i, j: (row(i, j), 0)),
        out_shape=jax.ShapeDtypeStruct((b * s, WIDTH), F32),
        scratch_shapes=[pltpu.VMEM((RET_HEADS, RET_QK_DIM, RET_V_DIM), F32),
                        pltpu.VMEM((blk, WIDTH), F32)],
        compiler_params=_compiler_params(("parallel", "arbitrary")),
        name="retention_branch",
    )(proj, proj, proj, proj, cos_f, sin_f, inner, kte, qfs, cdec, norm_w.reshape(1, -1))


def _rwkv_kernel(r_ref, k_ref, v_ref, l_ref, mur_ref, muk_ref, muv_ref, mul_ref, w0_ref, w2_ref,
                 a0_ref, a2_ref, g2_ref, kk_ref, ka_ref, rk_ref, lnw_ref, lnb_ref, o_ref,
                 tail, tail_l, r_s, k_s, v_s, kk_s, b_s, lw_s, o_s, state, *, blk, hw):
    nh = hw // RWKV_HEAD

    @pl.when(pl.program_id(2) == 0)
    def _():
        tail[...] = jnp.zeros_like(tail)
        tail_l[...] = jnp.zeros_like(tail_l)
        state[...] = jnp.zeros_like(state)

    rows = lax.broadcasted_iota(jnp.int32, (blk, 1), 0)

    def shift_mix(x, prev_row, mu):
        prev = jnp.where(rows == 0, prev_row, pltpu.roll(x, 1, 0))
        return x + (prev - x) * mu

    lane = lax.broadcasted_iota(jnp.int32, (LANE, LANE), 1) // RWKV_HEAD
    lrow = lax.broadcasted_iota(jnp.int32, (LANE, LANE), 0) // RWKV_HEAD
    head_ones = (lane == lrow).astype(BF16)

    def head_sum(x):
        hi = x.astype(BF16)
        lo = (x - hi.astype(F32)).astype(BF16)
        return jnp.concatenate(
            [jnp.dot(hi[:, q:q + LANE], head_ones, preferred_element_type=F32)
             + jnp.dot(lo[:, q:q + LANE], head_ones, preferred_element_type=F32)
             for q in range(0, hw, LANE)], axis=1)

    r_raw, k_raw, v_raw, l_raw = r_ref[...], k_ref[...], v_ref[...], l_ref[...]
    r = shift_mix(r_raw, tail[0:1, :], mur_ref[...])
    k = shift_mix(k_raw, tail[1:2, :], muk_ref[...])
    v = shift_mix(v_raw, tail[2:3, :], muv_ref[...])
    lo_ = shift_mix(l_raw, tail_l[0:1, :], mul_ref[...])
    tail[0:1, :] = r_raw[blk - 1:blk, :]
    tail[1:2, :] = k_raw[blk - 1:blk, :]
    tail[2:3, :] = v_raw[blk - 1:blk, :]
    tail_l[0:1, :] = l_raw[blk - 1:blk, :]

    wl = lo_[:, 0:RWKV_W_LORA]
    al = lo_[:, RWKV_W_LORA:RWKV_W_LORA + RWKV_A_LORA]
    gl = lo_[:, RWKV_W_LORA + RWKV_A_LORA:]
    w = -_softplus(-(w0_ref[...] + _bdot(jnp.tanh(wl), w2_ref[...]))) - 0.5
    a = jax.nn.sigmoid(a0_ref[...] + _bdot(al, a2_ref[...]))
    gate = _bdot(jax.nn.sigmoid(gl), g2_ref[...])
    kk = k * kk_ref[...]
    kk = kk / jnp.maximum(jnp.sqrt(head_sum(kk * kk)), 1e-12)
    k = k * (1.0 + (a - 1.0) * ka_ref[...])
    r_s[...] = r
    k_s[...] = k
    v_s[...] = v
    kk_s[...] = kk
    b_s[...] = kk * a
    lw_s[...] = -jnp.exp(w)

    trow = lax.broadcasted_iota(jnp.int32, (CHUNK, 2 * CHUNK), 0)
    tcol = lax.broadcasted_iota(jnp.int32, (CHUNK, 2 * CHUNK), 1) % CHUNK
    strict = tcol < trow
    incl = tcol <= trow
    left = lax.broadcasted_iota(jnp.int32, (CHUNK, LANE), 1) < RWKV_HEAD
    left2 = lax.broadcasted_iota(jnp.int32, (2 * CHUNK, LANE), 1) < RWKV_HEAD

    def chunk(c, carry):
        r0 = pl.multiple_of(c * CHUNK, CHUNK)
        sl = pl.ds(r0, CHUNK)
        lw = lw_s[sl, :]
        cum = _cumsum_rows(lw)
        cum_last = cum[CHUNK - 1:CHUNK, :]
        w_inv = jnp.exp(-cum)
        to_end = jnp.exp(cum_last - cum)
        rt = r_s[sl, :] * jnp.exp(cum)
        kt = k_s[sl, :] * w_inv
        bt = b_s[sl, :] * w_inv
        kap = kk_s[sl, :] * jnp.exp(cum - lw)
        k_end = k_s[sl, :] * to_end
        b_end = b_s[sl, :] * to_end
        w_all = jnp.exp(cum_last)
        vv = v_s[sl, :]
        zeros = jnp.zeros((CHUNK, LANE), F32)
        slabs = [slice(p * LANE, (p + 1) * LANE) for p in range(nh // 2)]
        heads = [(p, e) for p in range(nh // 2) for e in range(2)]
        pm2 = []
        for ps in slabs:
            lhs = jnp.concatenate([kap[:, ps], rt[:, ps]], axis=0)
            rhs = jnp.concatenate([bt[:, ps], kt[:, ps]], axis=0)
            rhs2 = jnp.concatenate([jnp.where(left2, rhs, 0.0), jnp.where(left2, 0.0, rhs)], axis=0)
            pm2.append(_bdot_nt(lhs, rhs2))
        v_swp = [pltpu.roll(vv[:, ps], RWKV_HEAD, 1) for ps in slabs]
        v_r = [v_swp[p] if e == 0 else vv[:, slabs[p]] for p, e in heads]
        s0 = [state[2 * p + e] for p, e in heads]
        s2 = [jnp.concatenate([s, s], axis=0) for s in s0]
        top = [jnp.where(strict, pm2[p][0:CHUNK, e * LANE:(e + 1) * LANE], 0.0) for p, e in heads]
        bot = [jnp.where(incl, pm2[p][CHUNK:, e * LANE:(e + 1) * LANE], 0.0) for p, e in heads]
        x0 = [_bdot(top[i], jnp.concatenate([zeros, v_r[i]], axis=0)) + _bdot_nt(kap[:, slabs[p]], s2[i])
              for i, (p, e) in enumerate(heads)]
        z = [jnp.where(left, -top[i], x0[i]) for i in range(nh)]
        span = 1
        while 2 * span < CHUNK:
            z = [_bdot(zi[:, 0:CHUNK], zi) + jnp.where(left, 0.0, zi) for zi in z]
            span *= 2
        xf = [_bdot(zi[:, 0:CHUNK], zi) + zi for zi in z]
        uv = [jnp.concatenate([-xf[i], v_r[i]], axis=0) for i in range(nh)]
        o_raw = [_bdot(bot[i], uv[i]) + _bdot_nt(rt[:, slabs[p]], s2[i]) for i, (p, e) in enumerate(heads)]
        for p, ps in enumerate(slabs):
            o_s[sl, ps] = jnp.where(left, pltpu.roll(o_raw[2 * p], RWKV_HEAD, 1), o_raw[2 * p + 1])
        for i, (p, e) in enumerate(heads):
            ps = slabs[p]
            ends = jnp.concatenate([b_end[:, ps], k_end[:, ps]], axis=0)
            end_m = jnp.where(left2, ends, 0.0) if e == 0 else jnp.where(left2, 0.0, ends)
            state[i] = s0[i] * w_all[:, ps] + _bdot_tn(uv[i], end_m)[LANE - RWKV_HEAD:, :]
        return carry

    lax.fori_loop(0, blk // CHUNK, chunk, 0)

    o = o_s[...]
    mu = head_sum(o) * (1.0 / RWKV_HEAD)
    oc = o - mu
    var = head_sum(oc * oc) * (1.0 / RWKV_HEAD)
    o = oc * lax.rsqrt(var + RWKV_LN_EPS) * lnw_ref[...] + lnb_ref[...]
    o = o + head_sum(r * k * rk_ref[...]) * v
    o_ref[...] = o * gate


def _rwkv_branch(proj, b, s, mu, w0, w2, a0, a2, g2, k_k, k_a, r_k, ln_w, ln_b, *, blk=256, hw=WIDTH):
    nblk = s // blk
    nslab = WIDTH // hw
    row = lambda i, p, j: i * nblk + j
    vec = lambda v: v.reshape(1, -1)
    hcol = lambda i, p, j: (0, p)
    mu_r, mu_k, mu_v, mu_l = (mu[0:WIDTH], mu[WIDTH:2 * WIDTH], mu[2 * WIDTH:3 * WIDTH],
                              mu[3 * WIDTH:])
    wblk = _col_block("wr")
    lblk = _col_block("lora")
    return pl.pallas_call(
        functools.partial(_rwkv_kernel, blk=blk, hw=hw),
        grid=(b, nslab, nblk),
        in_specs=[pl.BlockSpec((blk, hw), lambda i, p, j: (row(i, p, j), wblk * nslab + p)),
                  pl.BlockSpec((blk, hw), lambda i, p, j: (row(i, p, j), (wblk + 1) * nslab + p)),
                  pl.BlockSpec((blk, hw), lambda i, p, j: (row(i, p, j), (wblk + 2) * nslab + p)),
                  pl.BlockSpec((blk, RWKV_LORA), lambda i, p, j: (row(i, p, j), lblk)),
                  pl.BlockSpec((1, hw), hcol),
                  pl.BlockSpec((1, hw), hcol),
                  pl.BlockSpec((1, hw), hcol),
                  pl.BlockSpec((1, RWKV_LORA), lambda i, p, j: (0, 0)),
                  pl.BlockSpec((1, hw), hcol),
                  pl.BlockSpec((RWKV_W_LORA, hw), hcol),
                  pl.BlockSpec((1, hw), hcol),
                  pl.BlockSpec((RWKV_A_LORA, hw), hcol),
                  pl.BlockSpec((RWKV_G_LORA, hw), hcol),
                  pl.BlockSpec((1, hw), hcol),
                  pl.BlockSpec((1, hw), hcol),
                  pl.BlockSpec((1, hw), hcol),
                  pl.BlockSpec((1, hw), hcol),
                  pl.BlockSpec((1, hw), hcol)],
        out_specs=pl.BlockSpec((blk, hw), lambda i, p, j: (row(i, p, j), p)),
        out_shape=jax.ShapeDtypeStruct((b * s, WIDTH), F32),
        scratch_shapes=[pltpu.VMEM((SUBLANE, hw), F32), pltpu.VMEM((SUBLANE, RWKV_LORA), F32)]
        + [pltpu.VMEM((blk, hw), F32)] * 7
        + [pltpu.VMEM((hw // RWKV_HEAD, RWKV_HEAD, LANE), F32)],
        compiler_params=_compiler_params(("parallel", "parallel", "arbitrary")),
        name="rwkv7_branch",
    )(proj, proj, proj, proj, vec(mu_r), vec(mu_k), vec(mu_v), vec(mu_l), vec(w0), w2.astype(BF16),
      vec(a0), a2.astype(BF16), g2.astype(BF16), vec(k_k), vec(k_a), vec(r_k), vec(ln_w), vec(ln_b))


def _merge_kernel(y0_ref, y1_ref, y2_ref, y3_ref, gl_ref, x_ref, wb_ref, wo_ref, o_ref):
    merged = None
    for m, y_ref in enumerate((y0_ref, y1_ref, y2_ref, y3_ref)):
        br = jnp.dot(y_ref[...].astype(BF16), wb_ref[m], preferred_element_type=F32)
        term = jax.nn.sigmoid(gl_ref[:, m * D_MODEL:(m + 1) * D_MODEL]) * br
        merged = term if merged is None else merged + term
    o_ref[...] = x_ref[...] + jnp.dot(merged.astype(BF16), wo_ref[...], preferred_element_type=F32)


def _merge(ys, proj, x2, w_branch, w_out, *, tm=256):
    t = x2.shape[0]
    tok = lambda i: (i, 0)
    return pl.pallas_call(
        _merge_kernel,
        grid=(t // tm,),
        in_specs=[pl.BlockSpec((tm, WIDTH), tok)] * N_BRANCH
        + [pl.BlockSpec((tm, N_BRANCH * D_MODEL), lambda i: (i, _col_block("gates"))),
           pl.BlockSpec((tm, D_MODEL), tok),
           pl.BlockSpec((N_BRANCH, WIDTH, D_MODEL), lambda i: (0, 0, 0)),
           pl.BlockSpec((D_MODEL, D_MODEL), lambda i: (0, 0))],
        out_specs=pl.BlockSpec((tm, D_MODEL), tok),
        out_shape=jax.ShapeDtypeStruct((t, D_MODEL), F32),
        compiler_params=_compiler_params(("parallel",)),
        name="branch_merge",
    )(*ys, proj, x2, w_branch.astype(BF16), w_out.astype(BF16))


FFN_TC = 256


def _ffn_kernel(x_ref, g_ref, wu_ref, cw_ref, cb_ref, wd_ref, gf_ref, o_ref, cbuf, tail, *, tm, final):
    @pl.when(pl.program_id(1) == 0)
    def _():
        tail[...] = jnp.zeros_like(tail)

    x = x_ref[...]
    ms = jnp.mean(x * x, axis=-1, keepdims=True)
    hn = (x * lax.rsqrt(ms + EPS) * g_ref[...]).astype(BF16)
    acc = x
    for i in range(D_FF // FFN_TC):
        halves = []
        for half in range(2):
            c0 = half * D_FF + i * FFN_TC
            cs = slice(c0, c0 + FFN_TC)
            cbuf[0:SUBLANE, :] = tail[:, cs]
            cbuf[SUBLANE:SUBLANE + tm, :] = jnp.dot(hn, wu_ref[:, cs], preferred_element_type=F32)
            y = cb_ref[:, cs]
            for k in range(FFN_CONV):
                s = FFN_CONV - 1 - k
                y = y + cw_ref[k:k + 1, cs] * cbuf[SUBLANE - s:SUBLANE - s + tm, :]
            tail[:, cs] = cbuf[tm:tm + SUBLANE, :]
            halves.append(y)
        val, gt = halves
        acc = acc + jnp.dot((_silu(gt) * val).astype(BF16), wd_ref[i * FFN_TC:(i + 1) * FFN_TC, :],
                            preferred_element_type=F32)
    if final:
        ms = jnp.mean(acc * acc, axis=-1, keepdims=True)
        acc = acc * lax.rsqrt(ms + EPS) * gf_ref[...]
    o_ref[...] = acc


def _conv_ffn(x2, b, s, gain, w_up, conv_w, conv_b, w_down, gain_final, *, final, tm=256):
    nblk = s // tm
    row = lambda i, j: (i * nblk + j, 0)
    const = lambda i, j: (0, 0)
    return pl.pallas_call(
        functools.partial(_ffn_kernel, tm=tm, final=final),
        grid=(b, nblk),
        in_specs=[pl.BlockSpec((tm, D_MODEL), row),
                  pl.BlockSpec((1, D_MODEL), const),
                  pl.BlockSpec((D_MODEL, 2 * D_FF), const),
                  pl.BlockSpec((FFN_CONV, 2 * D_FF), const),
                  pl.BlockSpec((1, 2 * D_FF), const),
                  pl.BlockSpec((D_FF, D_MODEL), const),
                  pl.BlockSpec((1, D_MODEL), const)],
        out_specs=pl.BlockSpec((tm, D_MODEL), row),
        out_shape=jax.ShapeDtypeStruct((b * s, D_MODEL), F32),
        scratch_shapes=[pltpu.VMEM((tm + 2 * SUBLANE, FFN_TC), F32),
                        pltpu.VMEM((SUBLANE, 2 * D_FF), F32)],
        compiler_params=_compiler_params(("parallel", "arbitrary")),
        name="conv_ffn",
    )(x2, gain.reshape(1, -1), w_up.astype(BF16), conv_w, conv_b.reshape(1, -1),
      w_down.astype(BF16), gain_final.reshape(1, -1))


def _reorder_w_in(w):
    cols = []
    for n in _DST_ORDER:
        off, wd = _SRC[n]
        cols.append(w[:, off:off + wd])
    pad = N_PROJ - _DST["dt"][0] - SSM_HEADS
    cols.append(jnp.zeros((w.shape[0], pad), w.dtype))
    return jnp.concatenate(cols, axis=1).astype(BF16)


def kernel(x, norm_mix, w_in, ssm_conv_w, ssm_conv_b, ssm_dt_bias, ssm_a_log, ssm_d, ssm_norm, lru_conv_w, lru_conv_b, lru_w_a, lru_b_a, lru_w_i, lru_b_i, lru_lam, ret_norm, rwkv_mu, rwkv_w0, rwkv_w2, rwkv_a0, rwkv_a2, rwkv_g2, rwkv_k_k, rwkv_k_a, rwkv_r_k, rwkv_ln_w, rwkv_ln_b, w_branch, w_out, norm_ffn, ffn_up, ffn_conv_w, ffn_conv_b, ffn_down, norm_final):
    b, s, d = x.shape
    x2 = x.reshape(b * s, d)
    tables = _ret_tables(s)
    for l in range(DEPTH):
        proj = _norm_matmul(x2, norm_mix[l], _reorder_w_in(w_in[l]))
        y_ssd = _ssd_branch(proj, b, s, ssm_conv_w[l], ssm_conv_b[l], ssm_dt_bias[l], ssm_a_log[l],
                            ssm_d[l], ssm_norm[l])
        y_lru = _lru_branch(proj, b, s, lru_conv_w[l], lru_conv_b[l], lru_w_a[l], lru_b_a[l],
                            lru_w_i[l], lru_b_i[l], lru_lam[l])
        y_ret = _ret_branch(proj, b, s, ret_norm[l], tables)
        y_rwkv = _rwkv_branch(proj, b, s, rwkv_mu[l], rwkv_w0[l], rwkv_w2[l], rwkv_a0[l], rwkv_a2[l],
                              rwkv_g2[l], rwkv_k_k[l], rwkv_k_a[l], rwkv_r_k[l], rwkv_ln_w[l],
                              rwkv_ln_b[l])
        x2 = _merge((y_ssd, y_lru, y_ret, y_rwkv), proj, x2, w_branch[l], w_out[l])
        x2 = _conv_ffn(x2, b, s, norm_ffn[l], ffn_up[l], ffn_conv_w[l], ffn_conv_b[l], ffn_down[l],
                       norm_final, final=(l == DEPTH - 1))
    return x2.reshape(b, s, d)
```

```python
import functools

import jax
import jax.numpy as jnp
from jax import lax
from jax.experimental import pallas as pl
from jax.experimental.pallas import tpu as pltpu

F32 = jnp.float32
BF16 = jnp.bfloat16

D_MODEL = 1024
WIDTH = 1024
DEPTH = 2
CHUNK = 64
EPS = 1e-6

SSM_HEADS = 16
SSM_HEAD_DIM = 64
SSM_GROUPS = 4
SSM_STATE = 128
SSM_CONV = 4
SSM_XBC = WIDTH + 2 * SSM_GROUPS * SSM_STATE

LRU_BLOCKS = 16
LRU_BLOCK = 64
LRU_CONV = 4
LRU_C = 8.0

RET_HEADS = 8
RET_QK_DIM = 64
RET_V_DIM = 128
RET_QK = RET_HEADS * RET_QK_DIM
ROPE_BASE = 10000.0

RWKV_HEAD = 64
RWKV_HEADS = 16
RWKV_W_LORA = 64
RWKV_A_LORA = 64
RWKV_G_LORA = 128
RWKV_LORA = RWKV_W_LORA + RWKV_A_LORA + RWKV_G_LORA
RWKV_LN_EPS = 64e-5

D_FF = 2816
FFN_CONV = 3
N_BRANCH = 4

LANE = 128
SUBLANE = 8
VMEM_LIMIT = 56 * 1024 * 1024

_SRC = {}
_o = 0
for _n, _w in (("z", WIDTH), ("xbc", SSM_XBC), ("dt", SSM_HEADS), ("lgate", WIDTH), ("lx", WIDTH),
               ("rq", RET_QK), ("rk", RET_QK), ("rv", WIDTH), ("rg", WIDTH),
               ("wr", WIDTH), ("wk", WIDTH), ("wv", WIDTH), ("lora", RWKV_LORA),
               ("gates", N_BRANCH * D_MODEL)):
    _SRC[_n] = (_o, _w)
    _o += _w
N_IN = _o
_DST_ORDER = ("gates", "xbc", "z", "lgate", "lx", "rv", "rg", "rq", "rk", "wr", "wk", "wv", "lora", "dt")
_DST = {}
_o = 0
for _n in _DST_ORDER:
    _w = _SRC[_n][1]
    _bw = max(_w, LANE)
    assert _o % _bw == 0, (_n, _o, _bw)
    _DST[_n] = (_o, _bw)
    _o += _bw
PROJ_TN = 512
N_PROJ = -(-_o // PROJ_TN) * PROJ_TN


def _col_block(name):
    off, bw = _DST[name]
    return off // bw


def _softplus(x):
    return jnp.maximum(x, 0.0) + jnp.log1p(jnp.exp(-jnp.abs(x)))


def _silu(x):
    return x * jax.nn.sigmoid(x)


def _bdot(a, b):
    return jnp.dot(a.astype(BF16), b.astype(BF16), preferred_element_type=F32)


def _bdot_nt(a, b):
    return lax.dot_general(a.astype(BF16), b.astype(BF16), (((1,), (1,)), ((), ())),
                           preferred_element_type=F32)


def _bdot_tn(a, b):
    return lax.dot_general(a.astype(BF16), b.astype(BF16), (((0,), (0,)), ((), ())),
                           preferred_element_type=F32)


def _cumsum_rows(x):
    n = x.shape[0]
    row = lax.broadcasted_iota(jnp.int32, (n, n), 0)
    col = lax.broadcasted_iota(jnp.int32, (n, n), 1)
    tri = (col <= row).astype(F32)
    return jnp.dot(tri, x, precision=lax.Precision.HIGHEST, preferred_element_type=F32)


def _compiler_params(sem):
    return pltpu.CompilerParams(dimension_semantics=sem, vmem_limit_bytes=VMEM_LIMIT)


def _norm_matmul_kernel(x_ref, g_ref, w_ref, o_ref, hn_ref):
    @pl.when(pl.program_id(1) == 0)
    def _():
        x = x_ref[...]
        ms = jnp.mean(x * x, axis=-1, keepdims=True)
        hn_ref[...] = (x * lax.rsqrt(ms + EPS) * g_ref[...]).astype(BF16)

    o_ref[...] = jnp.dot(hn_ref[...], w_ref[...], preferred_element_type=F32).astype(o_ref.dtype)


def _norm_matmul(x2, gain, w, *, tm=1024, tn=PROJ_TN):
    t, d = x2.shape
    n = w.shape[1]
    return pl.pallas_call(
        _norm_matmul_kernel,
        grid=(t // tm, n // tn),
        in_specs=[pl.BlockSpec((tm, d), lambda i, j: (i, 0)),
                  pl.BlockSpec((1, d), lambda i, j: (0, 0)),
                  pl.BlockSpec((d, tn), lambda i, j: (0, j))],
        out_specs=pl.BlockSpec((tm, tn), lambda i, j: (i, j)),
        out_shape=jax.ShapeDtypeStruct((t, n), BF16),
        scratch_shapes=[pltpu.VMEM((tm, d), BF16)],
        compiler_params=_compiler_params(("parallel", "arbitrary")),
        name="norm_in_proj",
    )(x2, gain.reshape(1, d), w)


def _ssd_kernel(z_ref, xbc_ref, dt_ref, cw_ref, cb_ref, dtb_ref, alog_ref, dsk_ref, nw_ref, o_ref,
                cbuf, act, state, ybuf, *, blk):
    @pl.when(pl.program_id(1) == 0)
    def _():
        cbuf[0:SUBLANE, :] = jnp.zeros((SUBLANE, SSM_XBC), F32)
        state[...] = jnp.zeros_like(state)

    cbuf[SUBLANE:SUBLANE + blk, :] = xbc_ref[...].astype(F32)
    conv = cb_ref[...]
    for k in range(SSM_CONV):
        s = SSM_CONV - 1 - k
        conv = conv + cw_ref[k:k + 1, :] * cbuf[SUBLANE - s:SUBLANE - s + blk, :]
    act[...] = _silu(conv)
    cbuf[0:SUBLANE, :] = cbuf[blk:blk + SUBLANE, :]

    a_neg = -jnp.exp(alog_ref[...])
    row = lax.broadcasted_iota(jnp.int32, (CHUNK, LANE), 0)
    col = lax.broadcasted_iota(jnp.int32, (CHUNK, LANE), 1) % CHUNK
    causal = col <= row
    diag = col == row
    left = lax.broadcasted_iota(jnp.int32, (CHUNK, LANE), 1) < SSM_HEAD_DIM
    head_of = lax.broadcasted_iota(jnp.int32, (LANE, WIDTH), 1) // SSM_HEAD_DIM
    spread = (head_of == lax.broadcasted_iota(jnp.int32, (LANE, WIDTH), 0)).astype(BF16)
    npair = SSM_HEADS // 2
    pair_group = (SSM_HEADS // SSM_GROUPS) // 2

    def per_channel(x):
        hi = x.astype(BF16)
        lo = (x - hi.astype(F32)).astype(BF16)
        return (jnp.dot(hi, spread, preferred_element_type=F32)
                + jnp.dot(lo, spread, preferred_element_type=F32))

    def chunk(c, carry):
        r0 = pl.multiple_of(c * CHUNK, CHUNK)
        rows = pl.ds(r0, CHUNK)
        dt = _softplus(dt_ref[rows, :].astype(F32) + dtb_ref[...])
        a_cum = _cumsum_rows(dt * a_neg)
        full = per_channel(jnp.concatenate([dt, a_cum], axis=0))
        dt_c, ac_c = full[0:CHUNK, :], full[CHUNK:, :]
        a_last = ac_c[CHUNK - 1:CHUNK, :]
        xs = act[rows, 0:WIDTH]
        xdt = xs * dt_c
        xdt_end = xdt * jnp.exp(a_last - ac_c)
        from_start = jnp.exp(ac_c)
        chunk_decay = jnp.exp(a_last)
        bm = [act[rows, WIDTH + g * SSM_STATE:WIDTH + (g + 1) * SSM_STATE] for g in range(SSM_GROUPS)]
        cm = [act[rows, WIDTH + (SSM_GROUPS + g) * SSM_STATE:WIDTH + (SSM_GROUPS + g + 1) * SSM_STATE]
              for g in range(SSM_GROUPS)]
        cb2 = [_bdot_nt(cm[g], jnp.concatenate([bm[g], bm[g]], axis=0)) for g in range(SSM_GROUPS)]
        for p in range(npair):
            g = p // pair_group
            ps = slice(p * LANE, (p + 1) * LANE)
            seg_l = ac_c[:, ps]
            seg_s = jnp.sum(jnp.where(diag, seg_l, 0.0), axis=0, keepdims=True)
            m = cb2[g] * jnp.exp(jnp.where(causal, seg_l - seg_s, -jnp.inf))
            xd = xdt[:, ps]
            xd2 = jnp.concatenate([jnp.where(left, xd, 0.0), jnp.where(left, 0.0, xd)],
                                  axis=0)
            prev = state[p]
            y = _bdot(m, xd2) + _bdot(cm[g], prev) * from_start[:, ps] + xs[:, ps] * dsk_ref[:, ps]
            state[p] = prev * chunk_decay[:, ps] + _bdot_tn(bm[g], xdt_end[:, ps])
            ybuf[rows, ps] = y
        return carry

    lax.fori_loop(0, blk // CHUNK, chunk, 0)

    y = ybuf[...] * _silu(z_ref[...].astype(F32))
    ms = jnp.mean(y * y, axis=-1, keepdims=True)
    o_ref[...] = (y * lax.rsqrt(ms + EPS) * nw_ref[...]).astype(o_ref.dtype)


def _ssd_branch(proj, b, s, conv_w, conv_b, dt_bias, a_log, d_skip, norm_w, *, blk=256):
    nblk = s // blk
    pad = lambda v: jnp.pad(v.reshape(1, -1), ((0, 0), (0, LANE - v.shape[0])))
    dsk = jnp.repeat(d_skip, SSM_HEAD_DIM).reshape(1, WIDTH)
    row = lambda i, j: i * nblk + j
    const = lambda i, j: (0, 0)
    return pl.pallas_call(
        functools.partial(_ssd_kernel, blk=blk),
        grid=(b, nblk),
        in_specs=[pl.BlockSpec((blk, WIDTH), lambda i, j: (row(i, j), _col_block("z"))),
                  pl.BlockSpec((blk, SSM_XBC), lambda i, j: (row(i, j), _col_block("xbc"))),
                  pl.BlockSpec((blk, LANE), lambda i, j: (row(i, j), _col_block("dt"))),
                  pl.BlockSpec((SSM_CONV, SSM_XBC), const),
                  pl.BlockSpec((1, SSM_XBC), const),
                  pl.BlockSpec((1, LANE), const),
                  pl.BlockSpec((1, LANE), const),
                  pl.BlockSpec((1, WIDTH), const),
                  pl.BlockSpec((1, WIDTH), const)],
        out_specs=pl.BlockSpec((blk, WIDTH), lambda i, j: (row(i, j), 0)),
        out_shape=jax.ShapeDtypeStruct((b * s, WIDTH), BF16),
        scratch_shapes=[pltpu.VMEM((blk + 2 * SUBLANE, SSM_XBC), F32),
                        pltpu.VMEM((blk, SSM_XBC), F32),
                        pltpu.VMEM((SSM_HEADS // 2, SSM_STATE, LANE), F32),
                        pltpu.VMEM((blk, WIDTH), F32)],
        compiler_params=_compiler_params(("parallel", "arbitrary")),
        name="ssd_branch",
    )(proj, proj, proj, conv_w, conv_b.reshape(1, -1), pad(dt_bias), pad(a_log), dsk,
      norm_w.reshape(1, -1))


def _lru_kernel(gate_ref, x_ref, cw_ref, cb_ref, wa_ref, ba_ref, wi_ref, bi_ref, lam_ref, o_ref,
                cbuf, a_s, u_s, carry_s, *, blk):
    @pl.when(pl.program_id(1) == 0)
    def _():
        cbuf[0:SUBLANE, :] = jnp.zeros((SUBLANE, WIDTH), F32)
        carry_s[...] = jnp.zeros_like(carry_s)

    cbuf[SUBLANE:SUBLANE + blk, :] = x_ref[...].astype(F32)
    xc = cb_ref[...]
    for k in range(LRU_CONV):
        s = LRU_CONV - 1 - k
        xc = xc + cw_ref[k:k + 1, :] * cbuf[SUBLANE - s:SUBLANE - s + blk, :]
    cbuf[0:SUBLANE, :] = cbuf[blk:blk + SUBLANE, :]

    nsl = wa_ref.shape[0]
    wsl = WIDTH // nsl
    xcb = xc.astype(BF16)
    ra = jnp.concatenate([jnp.dot(xcb[:, q * wsl:(q + 1) * wsl], wa_ref[q], preferred_element_type=F32)
                          for q in range(nsl)], axis=1)
    ri = jnp.concatenate([jnp.dot(xcb[:, q * wsl:(q + 1) * wsl], wi_ref[q], preferred_element_type=F32)
                          for q in range(nsl)], axis=1)
    r = jax.nn.sigmoid(ra + ba_ref[...])
    i = jax.nn.sigmoid(ri + bi_ref[...])
    log_a = -LRU_C * r * _softplus(-lam_ref[...])
    a = jnp.exp(log_a)
    a_s[...] = a
    u_s[...] = jnp.sqrt(-jnp.tanh(log_a) * (a * a + 1.0)) * (i * xc)

    rows = lax.broadcasted_iota(jnp.int32, (SUBLANE, WIDTH), 0)

    def group(g, carry):
        r0 = pl.multiple_of(g * SUBLANE, SUBLANE)
        a = a_s[pl.ds(r0, SUBLANE), :]
        u = u_s[pl.ds(r0, SUBLANE), :]
        for k in (1, 2, 4):
            keep = rows >= k
            a_prev = jnp.where(keep, pltpu.roll(a, k, 0), 1.0)
            u_prev = jnp.where(keep, pltpu.roll(u, k, 0), 0.0)
            u = a * u_prev + u
            a = a * a_prev
        h = a * carry + u
        u_s[pl.ds(r0, SUBLANE), :] = h
        return jnp.broadcast_to(h[SUBLANE - 1:SUBLANE, :], (SUBLANE, WIDTH))

    carry_s[...] = lax.fori_loop(0, blk // SUBLANE, group, carry_s[...])
    o_ref[...] = (u_s[...] * jax.nn.gelu(gate_ref[...].astype(F32), approximate=True)).astype(o_ref.dtype)


def _block_diag_slabs(w, per_slab):
    g, n, _ = w.shape
    w = w.reshape(g // per_slab, per_slab, n, n)
    eye = jnp.eye(per_slab, dtype=w.dtype)
    out = jnp.einsum("spij,pq->spiqj", w, eye)
    return out.reshape(g // per_slab, per_slab * n, per_slab * n)


def _lru_branch(proj, b, s, conv_w, conv_b, w_a, b_a, w_i, b_i, lam, *, blk=256):
    nblk = s // blk
    per_slab = 4
    wa = _block_diag_slabs(w_a, per_slab).astype(BF16)
    wi = _block_diag_slabs(w_i, per_slab).astype(BF16)
    row = lambda i, j: i * nblk + j
    const = lambda i, j: (0, 0)
    const3 = lambda i, j: (0, 0, 0)
    vec = lambda v: v.reshape(1, WIDTH)
    return pl.pallas_call(
        functools.partial(_lru_kernel, blk=blk),
        grid=(b, nblk),
        in_specs=[pl.BlockSpec((blk, WIDTH), lambda i, j: (row(i, j), _col_block("lgate"))),
                  pl.BlockSpec((blk, WIDTH), lambda i, j: (row(i, j), _col_block("lx"))),
                  pl.BlockSpec((LRU_CONV, WIDTH), const),
                  pl.BlockSpec((1, WIDTH), const),
                  pl.BlockSpec(wa.shape, const3),
                  pl.BlockSpec((1, WIDTH), const),
                  pl.BlockSpec(wi.shape, const3),
                  pl.BlockSpec((1, WIDTH), const),
                  pl.BlockSpec((1, WIDTH), const)],
        out_specs=pl.BlockSpec((blk, WIDTH), lambda i, j: (row(i, j), 0)),
        out_shape=jax.ShapeDtypeStruct((b * s, WIDTH), BF16),
        scratch_shapes=[pltpu.VMEM((blk + 2 * SUBLANE, WIDTH), F32),
                        pltpu.VMEM((blk, WIDTH), F32),
                        pltpu.VMEM((blk, WIDTH), F32),
                        pltpu.VMEM((SUBLANE, WIDTH), F32)],
        compiler_params=_compiler_params(("parallel", "arbitrary")),
        name="rglru_branch",
    )(proj, proj, conv_w, vec(conv_b), wa, vec(b_a), wi, vec(b_i), vec(lam))


def _ret_kernel(q_ref, k_ref, v_ref, g_ref, cos_ref, sin_ref, inner_ref, kte_ref, qfs_ref, cdec_ref,
                nw_ref, o_ref, state, ybuf, *, blk):
    @pl.when(pl.program_id(1) == 0)
    def _():
        state[...] = jnp.zeros_like(state)

    lane = lax.broadcasted_iota(jnp.int32, (CHUNK, RET_QK), 1)
    first_half = (lane % RET_QK_DIM) < (RET_QK_DIM // 2)
    half = RET_QK_DIM // 2

    def rotary(x, cos, sin):
        swapped = jnp.where(first_half, pltpu.roll(x, RET_QK - half, 1), pltpu.roll(x, half, 1))
        return x * cos + swapped * sin

    left = lax.broadcasted_iota(jnp.int32, (CHUNK, LANE), 1) < RET_QK_DIM
    npair = RET_HEADS // 2

    def halves(x):
        return jnp.concatenate([jnp.where(left, x, 0.0), jnp.where(left, 0.0, x)], axis=0)

    def chunk(c, carry):
        r0 = pl.multiple_of(c * CHUNK, CHUNK)
        rows = pl.ds(r0, CHUNK)
        cos = cos_ref[rows, :]
        sin = sin_ref[rows, :]
        q = rotary(q_ref[rows, :].astype(F32), cos, sin)
        k = rotary(k_ref[rows, :].astype(F32), cos, sin) * (RET_QK_DIM ** -0.5)
        k_end = k * kte_ref[...]
        q_start = q * qfs_ref[...]
        slabs = [slice(p * LANE, (p + 1) * LANE) for p in range(npair)]
        v2 = [v_ref[rows, 2 * p * RET_V_DIM:(2 * p + 2) * RET_V_DIM] for p in range(npair)]
        v2 = [jnp.concatenate([v[:, 0:RET_V_DIM], v[:, RET_V_DIM:]], axis=0) for v in v2]
        prev = [state[p] for p in range(npair)]
        s2 = [_bdot_nt(q[:, ps], halves(k[:, ps])) * inner_ref[p] for p, ps in enumerate(slabs)]
        kv = [_bdot_tn(halves(k_end[:, ps]), v2[p]) for p, ps in enumerate(slabs)]
        for p, ps in enumerate(slabs):
            rhs = jnp.concatenate([v2[p], prev[p]], axis=0)
            for e in range(2):
                keep = left if e == 0 else jnp.logical_not(left)
                lhs = jnp.concatenate([jnp.where(keep, s2[p], 0.0), jnp.where(keep, q_start[:, ps], 0.0)],
                                      axis=1)
                y = _bdot(lhs, rhs)
                mu = jnp.mean(y, axis=-1, keepdims=True)
                yc = y - mu
                var = jnp.mean(yc * yc, axis=-1, keepdims=True)
                h = 2 * p + e
                ybuf[rows, h * RET_V_DIM:(h + 1) * RET_V_DIM] = yc * lax.rsqrt(var + EPS)
            state[p] = prev[p] * cdec_ref[p] + kv[p]
        return carry

    lax.fori_loop(0, blk // CHUNK, chunk, 0)
    o_ref[...] = (_silu(g_ref[...].astype(F32)) * (ybuf[...] * nw_ref[...])).astype(o_ref.dtype)


def _ret_tables(s):
    pos_s = jnp.arange(s, dtype=F32)
    inv_freq = ROPE_BASE ** (-jnp.arange(0, RET_QK_DIM, 2, dtype=F32) / RET_QK_DIM)
    ang = pos_s[:, None] * inv_freq[None, :]
    cos, sin = jnp.cos(ang), jnp.sin(ang)
    cos_f = jnp.tile(jnp.concatenate([cos, cos], axis=1), (1, RET_HEADS))
    sin_f = jnp.tile(jnp.concatenate([-sin, sin], axis=1), (1, RET_HEADS))
    log_gamma = jnp.log1p(-jnp.exp2(-5.0 - jnp.arange(RET_HEADS, dtype=F32)))
    pos = jnp.arange(CHUNK, dtype=F32)
    inner = jnp.exp(log_gamma[:, None, None] * jnp.abs(pos[:, None] - pos[None, :]))
    k_to_end = jnp.exp(log_gamma[:, None] * (CHUNK - 1.0 - pos))
    q_from_start = jnp.exp(log_gamma[:, None] * (pos + 1.0))
    kte = jnp.repeat(k_to_end.T, RET_QK_DIM, axis=1)
    qfs = jnp.repeat(q_from_start.T, RET_QK_DIM, axis=1)
    inner2 = inner.reshape(RET_HEADS // 2, 2, CHUNK, CHUNK).transpose(0, 2, 1, 3).reshape(
        RET_HEADS // 2, CHUNK, 2 * CHUNK)
    cdec = jnp.broadcast_to(jnp.repeat(jnp.exp(log_gamma * CHUNK), RET_QK_DIM).reshape(
        RET_HEADS // 2, 2 * RET_QK_DIM, 1), (RET_HEADS // 2, 2 * RET_QK_DIM, RET_V_DIM))
    return cos_f, sin_f, inner2, kte, qfs, cdec


def _ret_branch(proj, b, s, norm_w, tables, *, blk=256):
    nblk = s // blk
    cos_f, sin_f, inner, kte, qfs, cdec = tables
    row = lambda i, j: i * nblk + j
    const = lambda i, j: (0, 0)
    return pl.pallas_call(
        functools.partial(_ret_kernel, blk=blk),
        grid=(b, nblk),
        in_specs=[pl.BlockSpec((blk, RET_QK), lambda i, j: (row(i, j), _col_block("rq"))),
                  pl.BlockSpec((blk, RET_QK), lambda i, j: (row(i, j), _col_block("rk"))),
                  pl.BlockSpec((blk, WIDTH), lambda i, j: (row(i, j), _col_block("rv"))),
                  pl.BlockSpec((blk, WIDTH), lambda i, j: (row(i, j), _col_block("rg"))),
                  pl.BlockSpec((blk, RET_QK), lambda i, j: (j, 0)),
                  pl.BlockSpec((blk, RET_QK), lambda i, j: (j, 0)),
                  pl.BlockSpec(inner.shape, lambda i, j: (0, 0, 0)),
                  pl.BlockSpec((CHUNK, RET_QK), const),
                  pl.BlockSpec((CHUNK, RET_QK), const),
                  pl.BlockSpec(cdec.shape, lambda i, j: (0, 0, 0)),
                  pl.BlockSpec((1, WIDTH), const)],
        out_specs=pl.BlockSpec((blk, WIDTH), lambda i, j: (row(i, j), 0)),
        out_shape=jax.ShapeDtypeStruct((b * s, WIDTH), BF16),
        scratch_shapes=[pltpu.VMEM((RET_HEADS // 2, 2 * RET_QK_DIM, RET_V_DIM), F32),
                        pltpu.VMEM((blk, WIDTH), F32)],
        compiler_params=_compiler_params(("parallel", "arbitrary")),
        name="retention_branch",
    )(proj, proj, proj, proj, cos_f, sin_f, inner, kte, qfs, cdec, norm_w.reshape(1, -1))


def _rwkv_kernel(r_ref, k_ref, v_ref, l_ref, mur_ref, muk_ref, muv_ref, mul_ref, w0_ref, w2_ref,
                 a0_ref, a2_ref, g2_ref, kk_ref, ka_ref, rk_ref, lnw_ref, lnb_ref, o_ref,
                 tail, tail_l, r_s, k_s, v_s, kk_s, b_s, lw_s, o_s, state, *, blk, hw):
    nh = hw // RWKV_HEAD

    @pl.when(pl.program_id(2) == 0)
    def _():
        tail[...] = jnp.zeros_like(tail)
        tail_l[...] = jnp.zeros_like(tail_l)
        state[...] = jnp.zeros_like(state)

    rows = lax.broadcasted_iota(jnp.int32, (blk, 1), 0)

    def shift_mix(x, prev_row, mu):
        prev = jnp.where(rows == 0, prev_row, pltpu.roll(x, 1, 0))
        return x + (prev - x) * mu

    lane = lax.broadcasted_iota(jnp.int32, (LANE, LANE), 1) // RWKV_HEAD
    lrow = lax.broadcasted_iota(jnp.int32, (LANE, LANE), 0) // RWKV_HEAD
    head_ones = (lane == lrow).astype(BF16)

    def head_sum(x):
        hi = x.astype(BF16)
        lo = (x - hi.astype(F32)).astype(BF16)
        return jnp.concatenate(
            [jnp.dot(hi[:, q:q + LANE], head_ones, preferred_element_type=F32)
             + jnp.dot(lo[:, q:q + LANE], head_ones, preferred_element_type=F32)
             for q in range(0, hw, LANE)], axis=1)

    r_raw, k_raw, v_raw, l_raw = (ref[...].astype(F32) for ref in (r_ref, k_ref, v_ref, l_ref))
    r = shift_mix(r_raw, tail[0:1, :], mur_ref[...])
    k = shift_mix(k_raw, tail[1:2, :], muk_ref[...])
    v = shift_mix(v_raw, tail[2:3, :], muv_ref[...])
    lo_ = shift_mix(l_raw, tail_l[0:1, :], mul_ref[...])
    tail[0:1, :] = r_raw[blk - 1:blk, :]
    tail[1:2, :] = k_raw[blk - 1:blk, :]
    tail[2:3, :] = v_raw[blk - 1:blk, :]
    tail_l[0:1, :] = l_raw[blk - 1:blk, :]

    wl = lo_[:, 0:RWKV_W_LORA]
    al = lo_[:, RWKV_W_LORA:RWKV_W_LORA + RWKV_A_LORA]
    gl = lo_[:, RWKV_W_LORA + RWKV_A_LORA:]
    w = -_softplus(-(w0_ref[...] + _bdot(jnp.tanh(wl), w2_ref[...]))) - 0.5
    a = jax.nn.sigmoid(a0_ref[...] + _bdot(al, a2_ref[...]))
    gate = _bdot(jax.nn.sigmoid(gl), g2_ref[...])
    kk = k * kk_ref[...]
    kk = kk / jnp.maximum(jnp.sqrt(head_sum(kk * kk)), 1e-12)
    k = k * (1.0 + (a - 1.0) * ka_ref[...])
    r_s[...] = r
    k_s[...] = k
    v_s[...] = v
    kk_s[...] = kk
    b_s[...] = kk * a
    lw_s[...] = -jnp.exp(w)

    trow = lax.broadcasted_iota(jnp.int32, (CHUNK, 2 * CHUNK), 0)
    tcol = lax.broadcasted_iota(jnp.int32, (CHUNK, 2 * CHUNK), 1) % CHUNK
    strict = tcol < trow
    incl = tcol <= trow
    left = lax.broadcasted_iota(jnp.int32, (CHUNK, LANE), 1) < RWKV_HEAD
    left2 = lax.broadcasted_iota(jnp.int32, (2 * CHUNK, LANE), 1) < RWKV_HEAD

    def chunk(c, carry):
        r0 = pl.multiple_of(c * CHUNK, CHUNK)
        sl = pl.ds(r0, CHUNK)
        lw = lw_s[sl, :]
        cum = _cumsum_rows(lw)
        cum_last = cum[CHUNK - 1:CHUNK, :]
        w_inv = jnp.exp(-cum)
        to_end = jnp.exp(cum_last - cum)
        rt = r_s[sl, :] * jnp.exp(cum)
        kt = k_s[sl, :] * w_inv
        bt = b_s[sl, :] * w_inv
        kap = kk_s[sl, :] * jnp.exp(cum - lw)
        k_end = k_s[sl, :] * to_end
        b_end = b_s[sl, :] * to_end
        w_all = jnp.exp(cum_last)
        vv = v_s[sl, :]
        zeros = jnp.zeros((CHUNK, LANE), F32)
        slabs = [slice(p * LANE, (p + 1) * LANE) for p in range(nh // 2)]
        heads = [(p, e) for p in range(nh // 2) for e in range(2)]
        pm2 = []
        for ps in slabs:
            lhs = jnp.concatenate([kap[:, ps], rt[:, ps]], axis=0)
            rhs = jnp.concatenate([bt[:, ps], kt[:, ps]], axis=0)
            rhs2 = jnp.concatenate([jnp.where(left2, rhs, 0.0), jnp.where(left2, 0.0, rhs)], axis=0)
            pm2.append(_bdot_nt(lhs, rhs2))
        v_swp = [pltpu.roll(vv[:, ps], RWKV_HEAD, 1) for ps in slabs]
        v_r = [v_swp[p] if e == 0 else vv[:, slabs[p]] for p, e in heads]
        s0 = [state[2 * p + e] for p, e in heads]
        s2 = [jnp.concatenate([s, s], axis=0) for s in s0]
        top = [jnp.where(strict, pm2[p][0:CHUNK, e * LANE:(e + 1) * LANE], 0.0) for p, e in heads]
        bot = [jnp.where(incl, pm2[p][CHUNK:, e * LANE:(e + 1) * LANE], 0.0) for p, e in heads]
        x0 = [_bdot(top[i], jnp.concatenate([zeros, v_r[i]], axis=0)) + _bdot_nt(kap[:, slabs[p]], s2[i])
              for i, (p, e) in enumerate(heads)]
        z = [jnp.where(left, -top[i], x0[i]) for i in range(nh)]
        span = 1
        while 2 * span < CHUNK:
            z = [_bdot(zi[:, 0:CHUNK], zi) + jnp.where(left, 0.0, zi) for zi in z]
            span *= 2
        xf = [_bdot(zi[:, 0:CHUNK], zi) + zi for zi in z]
        uv = [jnp.concatenate([-xf[i], v_r[i]], axis=0) for i in range(nh)]
        o_raw = [_bdot(bot[i], uv[i]) + _bdot_nt(rt[:, slabs[p]], s2[i]) for i, (p, e) in enumerate(heads)]
        for p, ps in enumerate(slabs):
            o_s[sl, ps] = jnp.where(left, pltpu.roll(o_raw[2 * p], RWKV_HEAD, 1), o_raw[2 * p + 1])
        for i, (p, e) in enumerate(heads):
            ps = slabs[p]
            ends = jnp.concatenate([b_end[:, ps], k_end[:, ps]], axis=0)
            end_m = jnp.where(left2, ends, 0.0) if e == 0 else jnp.where(left2, 0.0, ends)
            state[i] = s0[i] * w_all[:, ps] + _bdot_tn(uv[i], end_m)[LANE - RWKV_HEAD:, :]
        return carry

    lax.fori_loop(0, blk // CHUNK, chunk, 0)

    o = o_s[...]
    mu = head_sum(o) * (1.0 / RWKV_HEAD)
    oc = o - mu
    var = head_sum(oc * oc) * (1.0 / RWKV_HEAD)
    o = oc * lax.rsqrt(var + RWKV_LN_EPS) * lnw_ref[...] + lnb_ref[...]
    o = o + head_sum(r * k * rk_ref[...]) * v
    o_ref[...] = (o * gate).astype(o_ref.dtype)


def _rwkv_branch(proj, b, s, mu, w0, w2, a0, a2, g2, k_k, k_a, r_k, ln_w, ln_b, *, blk=256, hw=WIDTH):
    nblk = s // blk
    nslab = WIDTH // hw
    row = lambda i, p, j: i * nblk + j
    vec = lambda v: v.reshape(1, -1)
    hcol = lambda i, p, j: (0, p)
    mu_r, mu_k, mu_v, mu_l = (mu[0:WIDTH], mu[WIDTH:2 * WIDTH], mu[2 * WIDTH:3 * WIDTH],
                              mu[3 * WIDTH:])
    wblk = _col_block("wr")
    lblk = _col_block("lora")
    return pl.pallas_call(
        functools.partial(_rwkv_kernel, blk=blk, hw=hw),
        grid=(b, nslab, nblk),
        in_specs=[pl.BlockSpec((blk, hw), lambda i, p, j: (row(i, p, j), wblk * nslab + p)),
                  pl.BlockSpec((blk, hw), lambda i, p, j: (row(i, p, j), (wblk + 1) * nslab + p)),
                  pl.BlockSpec((blk, hw), lambda i, p, j: (row(i, p, j), (wblk + 2) * nslab + p)),
                  pl.BlockSpec((blk, RWKV_LORA), lambda i, p, j: (row(i, p, j), lblk)),
                  pl.BlockSpec((1, hw), hcol),
                  pl.BlockSpec((1, hw), hcol),
                  pl.BlockSpec((1, hw), hcol),
                  pl.BlockSpec((1, RWKV_LORA), lambda i, p, j: (0, 0)),
                  pl.BlockSpec((1, hw), hcol),
                  pl.BlockSpec((RWKV_W_LORA, hw), hcol),
                  pl.BlockSpec((1, hw), hcol),
                  pl.BlockSpec((RWKV_A_LORA, hw), hcol),
                  pl.BlockSpec((RWKV_G_LORA, hw), hcol),
                  pl.BlockSpec((1, hw), hcol),
                  pl.BlockSpec((1, hw), hcol),
                  pl.BlockSpec((1, hw), hcol),
                  pl.BlockSpec((1, hw), hcol),
                  pl.BlockSpec((1, hw), hcol)],
        out_specs=pl.BlockSpec((blk, hw), lambda i, p, j: (row(i, p, j), p)),
        out_shape=jax.ShapeDtypeStruct((b * s, WIDTH), BF16),
        scratch_shapes=[pltpu.VMEM((SUBLANE, hw), F32), pltpu.VMEM((SUBLANE, RWKV_LORA), F32)]
        + [pltpu.VMEM((blk, hw), F32)] * 7
        + [pltpu.VMEM((hw // RWKV_HEAD, RWKV_HEAD, LANE), F32)],
        compiler_params=_compiler_params(("parallel", "parallel", "arbitrary")),
        name="rwkv7_branch",
    )(proj, proj, proj, proj, vec(mu_r), vec(mu_k), vec(mu_v), vec(mu_l), vec(w0), w2.astype(BF16),
      vec(a0), a2.astype(BF16), g2.astype(BF16), vec(k_k), vec(k_a), vec(r_k), vec(ln_w), vec(ln_b))


def _merge_kernel(y0_ref, y1_ref, y2_ref, y3_ref, gl_ref, x_ref, wb_ref, wo_ref, o_ref):
    merged = None
    for m, y_ref in enumerate((y0_ref, y1_ref, y2_ref, y3_ref)):
        br = jnp.dot(y_ref[...], wb_ref[m], preferred_element_type=F32)
        term = jax.nn.sigmoid(gl_ref[:, m * D_MODEL:(m + 1) * D_MODEL].astype(F32)) * br
        merged = term if merged is None else merged + term
    o_ref[...] = x_ref[...] + jnp.dot(merged.astype(BF16), wo_ref[...], preferred_element_type=F32)


def _merge(ys, proj, x2, w_branch, w_out, *, tm=256):
    t = x2.shape[0]
    tok = lambda i: (i, 0)
    return pl.pallas_call(
        _merge_kernel,
        grid=(t // tm,),
        in_specs=[pl.BlockSpec((tm, WIDTH), tok)] * N_BRANCH
        + [pl.BlockSpec((tm, N_BRANCH * D_MODEL), lambda i: (i, _col_block("gates"))),
           pl.BlockSpec((tm, D_MODEL), tok),
           pl.BlockSpec((N_BRANCH, WIDTH, D_MODEL), lambda i: (0, 0, 0)),
           pl.BlockSpec((D_MODEL, D_MODEL), lambda i: (0, 0))],
        out_specs=pl.BlockSpec((tm, D_MODEL), tok),
        out_shape=jax.ShapeDtypeStruct((t, D_MODEL), F32),
        compiler_params=_compiler_params(("parallel",)),
        name="branch_merge",
    )(*ys, proj, x2, w_branch.astype(BF16), w_out.astype(BF16))


FFN_TC = 256


def _ffn_kernel(x_ref, g_ref, wu_ref, cw_ref, cb_ref, wd_ref, gf_ref, o_ref, cbuf, tail, *, tm, final):
    @pl.when(pl.program_id(1) == 0)
    def _():
        tail[...] = jnp.zeros_like(tail)

    x = x_ref[...]
    ms = jnp.mean(x * x, axis=-1, keepdims=True)
    hn = (x * lax.rsqrt(ms + EPS) * g_ref[...]).astype(BF16)
    acc = x
    for i in range(D_FF // FFN_TC):
        halves = []
        for half in range(2):
            c0 = half * D_FF + i * FFN_TC
            cs = slice(c0, c0 + FFN_TC)
            cbuf[0:SUBLANE, :] = tail[:, cs]
            cbuf[SUBLANE:SUBLANE + tm, :] = jnp.dot(hn, wu_ref[:, cs], preferred_element_type=F32)
            y = cb_ref[:, cs]
            for k in range(FFN_CONV):
                s = FFN_CONV - 1 - k
                y = y + cw_ref[k:k + 1, cs] * cbuf[SUBLANE - s:SUBLANE - s + tm, :]
            tail[:, cs] = cbuf[tm:tm + SUBLANE, :]
            halves.append(y)
        val, gt = halves
        acc = acc + jnp.dot((_silu(gt) * val).astype(BF16), wd_ref[i * FFN_TC:(i + 1) * FFN_TC, :],
                            preferred_element_type=F32)
    if final:
        ms = jnp.mean(acc * acc, axis=-1, keepdims=True)
        acc = acc * lax.rsqrt(ms + EPS) * gf_ref[...]
    o_ref[...] = acc


def _conv_ffn(x2, b, s, gain, w_up, conv_w, conv_b, w_down, gain_final, *, final, tm=256):
    nblk = s // tm
    row = lambda i, j: (i * nblk + j, 0)
    const = lambda i, j: (0, 0)
    return pl.pallas_call(
        functools.partial(_ffn_kernel, tm=tm, final=final),
        grid=(b, nblk),
        in_specs=[pl.BlockSpec((tm, D_MODEL), row),
                  pl.BlockSpec((1, D_MODEL), const),
                  pl.BlockSpec((D_MODEL, 2 * D_FF), const),
                  pl.BlockSpec((FFN_CONV, 2 * D_FF), const),
                  pl.BlockSpec((1, 2 * D_FF), const),
                  pl.BlockSpec((D_FF, D_MODEL), const),
                  pl.BlockSpec((1, D_MODEL), const)],
        out_specs=pl.BlockSpec((tm, D_MODEL), row),
        out_shape=jax.ShapeDtypeStruct((b * s, D_MODEL), F32),
        scratch_shapes=[pltpu.VMEM((tm + 2 * SUBLANE, FFN_TC), F32),
                        pltpu.VMEM((SUBLANE, 2 * D_FF), F32)],
        compiler_params=_compiler_params(("parallel", "arbitrary")),
        name="conv_ffn",
    )(x2, gain.reshape(1, -1), w_up.astype(BF16), conv_w, conv_b.reshape(1, -1),
      w_down.astype(BF16), gain_final.reshape(1, -1))


def _reorder_w_in(w):
    cols = []
    for n in _DST_ORDER:
        off, wd = _SRC[n]
        cols.append(w[:, off:off + wd])
    pad = N_PROJ - _DST["dt"][0] - SSM_HEADS
    cols.append(jnp.zeros((w.shape[0], pad), w.dtype))
    return jnp.concatenate(cols, axis=1).astype(BF16)


def kernel(x, norm_mix, w_in, ssm_conv_w, ssm_conv_b, ssm_dt_bias, ssm_a_log, ssm_d, ssm_norm, lru_conv_w, lru_conv_b, lru_w_a, lru_b_a, lru_w_i, lru_b_i, lru_lam, ret_norm, rwkv_mu, rwkv_w0, rwkv_w2, rwkv_a0, rwkv_a2, rwkv_g2, rwkv_k_k, rwkv_k_a, rwkv_r_k, rwkv_ln_w, rwkv_ln_b, w_branch, w_out, norm_ffn, ffn_up, ffn_conv_w, ffn_conv_b, ffn_down, norm_final):
    b, s, d = x.shape
    x2 = x.reshape(b * s, d)
    tables = _ret_tables(s)
    for l in range(DEPTH):
        proj = _norm_matmul(x2, norm_mix[l], _reorder_w_in(w_in[l]))
        y_ssd = _ssd_branch(proj, b, s, ssm_conv_w[l], ssm_conv_b[l], ssm_dt_bias[l], ssm_a_log[l],
                            ssm_d[l], ssm_norm[l])
        y_lru = _lru_branch(proj, b, s, lru_conv_w[l], lru_conv_b[l], lru_w_a[l], lru_b_a[l],
                            lru_w_i[l], lru_b_i[l], lru_lam[l])
        y_ret = _ret_branch(proj, b, s, ret_norm[l], tables)
        y_rwkv = _rwkv_branch(proj, b, s, rwkv_mu[l], rwkv_w0[l], rwkv_w2[l], rwkv_a0[l], rwkv_a2[l],
                              rwkv_g2[l], rwkv_k_k[l], rwkv_k_a[l], rwkv_r_k[l], rwkv_ln_w[l],
                              rwkv_ln_b[l])
        x2 = _merge((y_ssd, y_lru, y_ret, y_rwkv), proj, x2, w_branch[l], w_out[l])
        x2 = _conv_ffn(x2, b, s, norm_ffn[l], ffn_up[l], ffn_conv_w[l], ffn_conv_b[l], ffn_down[l],
                       norm_final, final=(l == DEPTH - 1))
    return x2.reshape(b, s, d)
```

```python
import functools

import jax
import jax.numpy as jnp
from jax import lax
from jax.experimental import pallas as pl
from jax.experimental.pallas import tpu as pltpu

F32 = jnp.float32
BF16 = jnp.bfloat16

D_MODEL = 1024
WIDTH = 1024
DEPTH = 2
CHUNK = 64
EPS = 1e-6

SSM_HEADS = 16
SSM_HEAD_DIM = 64
SSM_GROUPS = 4
SSM_STATE = 128
SSM_CONV = 4
SSM_XBC = WIDTH + 2 * SSM_GROUPS * SSM_STATE

LRU_BLOCKS = 16
LRU_BLOCK = 64
LRU_CONV = 4
LRU_C = 8.0

RET_HEADS = 8
RET_QK_DIM = 64
RET_V_DIM = 128
RET_QK = RET_HEADS * RET_QK_DIM
ROPE_BASE = 10000.0

RWKV_HEAD = 64
RWKV_HEADS = 16
RWKV_W_LORA = 64
RWKV_A_LORA = 64
RWKV_G_LORA = 128
RWKV_LORA = RWKV_W_LORA + RWKV_A_LORA + RWKV_G_LORA
RWKV_LN_EPS = 64e-5
A_UNROLL = 2

D_FF = 2816
FFN_CONV = 3
N_BRANCH = 4

LANE = 128
SUBLANE = 8
VMEM_LIMIT = 56 * 1024 * 1024

_SRC = {}
_o = 0
for _n, _w in (("z", WIDTH), ("xbc", SSM_XBC), ("dt", SSM_HEADS), ("lgate", WIDTH), ("lx", WIDTH),
               ("rq", RET_QK), ("rk", RET_QK), ("rv", WIDTH), ("rg", WIDTH),
               ("wr", WIDTH), ("wk", WIDTH), ("wv", WIDTH), ("lora", RWKV_LORA),
               ("gates", N_BRANCH * D_MODEL)):
    _SRC[_n] = (_o, _w)
    _o += _w
N_IN = _o
_DST_ORDER = ("gates", "xbc", "z", "lgate", "lx", "rv", "rg", "rq", "rk", "wr", "wk", "wv", "lora", "dt")
_DST = {}
_o = 0
for _n in _DST_ORDER:
    _w = _SRC[_n][1]
    _bw = max(_w, LANE)
    assert _o % _bw == 0, (_n, _o, _bw)
    _DST[_n] = (_o, _bw)
    _o += _bw
PROJ_TN = 1024
N_PROJ = -(-_o // PROJ_TN) * PROJ_TN


def _col_block(name):
    off, bw = _DST[name]
    return off // bw


def _softplus(x):
    return jnp.maximum(x, 0.0) + jnp.log1p(jnp.exp(-jnp.abs(x)))


def _silu(x):
    return x * jax.nn.sigmoid(x)


def _bdot(a, b):
    return jnp.dot(a.astype(BF16), b.astype(BF16), preferred_element_type=F32)


def _bdot_nt(a, b):
    return lax.dot_general(a.astype(BF16), b.astype(BF16), (((1,), (1,)), ((), ())),
                           preferred_element_type=F32)


def _bdot_tn(a, b):
    return lax.dot_general(a.astype(BF16), b.astype(BF16), (((0,), (0,)), ((), ())),
                           preferred_element_type=F32)


def _cumsum_rows(x):
    n = x.shape[0]
    row = lax.broadcasted_iota(jnp.int32, (n, n), 0)
    col = lax.broadcasted_iota(jnp.int32, (n, n), 1)
    tri = (col <= row).astype(BF16)
    hi = x.astype(BF16)
    rest = x - hi.astype(F32)
    mid = rest.astype(BF16)
    lo = (rest - mid.astype(F32)).astype(BF16)
    return (jnp.dot(tri, hi, preferred_element_type=F32) + jnp.dot(tri, mid, preferred_element_type=F32)
            + jnp.dot(tri, lo, preferred_element_type=F32))


def _compiler_params(sem):
    return pltpu.CompilerParams(dimension_semantics=sem, vmem_limit_bytes=VMEM_LIMIT)


def _norm_matmul_kernel(x_ref, g_ref, w_ref, o_ref, hn_ref):
    @pl.when(pl.program_id(1) == 0)
    def _():
        x = x_ref[...]
        ms = jnp.mean(x * x, axis=-1, keepdims=True)
        hn_ref[...] = (x * lax.rsqrt(ms + EPS) * g_ref[...]).astype(BF16)

    o_ref[...] = jnp.dot(hn_ref[...], w_ref[...], preferred_element_type=F32).astype(o_ref.dtype)


def _norm_matmul(x2, gain, w, *, tm=2048, tn=PROJ_TN):
    t, d = x2.shape
    n = w.shape[1]
    return pl.pallas_call(
        _norm_matmul_kernel,
        grid=(t // tm, n // tn),
        in_specs=[pl.BlockSpec((tm, d), lambda i, j: (i, 0)),
                  pl.BlockSpec((1, d), lambda i, j: (0, 0)),
                  pl.BlockSpec((d, tn), lambda i, j: (0, j))],
        out_specs=pl.BlockSpec((tm, tn), lambda i, j: (i, j)),
        out_shape=jax.ShapeDtypeStruct((t, n), BF16),
        scratch_shapes=[pltpu.VMEM((tm, d), BF16)],
        compiler_params=_compiler_params(("parallel", "arbitrary")),
        name="norm_in_proj",
    )(x2, gain.reshape(1, d), w)


def _ssd_kernel(z_ref, xbc_ref, dt_ref, cw_ref, cb_ref, dtb_ref, alog_ref, dsk_ref, nw_ref, o_ref,
                cbuf, act, state, ybuf, *, blk):
    @pl.when(pl.program_id(1) == 0)
    def _():
        cbuf[0:SUBLANE, :] = jnp.zeros((SUBLANE, SSM_XBC), F32)
        state[...] = jnp.zeros_like(state)

    cbuf[SUBLANE:SUBLANE + blk, :] = xbc_ref[...].astype(F32)
    conv = cb_ref[...]
    for k in range(SSM_CONV):
        s = SSM_CONV - 1 - k
        conv = conv + cw_ref[k:k + 1, :] * cbuf[SUBLANE - s:SUBLANE - s + blk, :]
    act[...] = _silu(conv)
    cbuf[0:SUBLANE, :] = cbuf[blk:blk + SUBLANE, :]

    a_neg = -jnp.exp(alog_ref[...])
    row = lax.broadcasted_iota(jnp.int32, (CHUNK, LANE), 0)
    col = lax.broadcasted_iota(jnp.int32, (CHUNK, LANE), 1) % CHUNK
    causal = col <= row
    diag = col == row
    left = lax.broadcasted_iota(jnp.int32, (CHUNK, LANE), 1) < SSM_HEAD_DIM
    head_of = lax.broadcasted_iota(jnp.int32, (LANE, WIDTH), 1) // SSM_HEAD_DIM
    spread = (head_of == lax.broadcasted_iota(jnp.int32, (LANE, WIDTH), 0)).astype(BF16)
    npair = SSM_HEADS // 2
    pair_group = (SSM_HEADS // SSM_GROUPS) // 2

    def per_channel(x):
        hi = x.astype(BF16)
        lo = (x - hi.astype(F32)).astype(BF16)
        return (jnp.dot(hi, spread, preferred_element_type=F32)
                + jnp.dot(lo, spread, preferred_element_type=F32))

    def chunk(c, carry):
        r0 = pl.multiple_of(c * CHUNK, CHUNK)
        rows = pl.ds(r0, CHUNK)
        dt = _softplus(dt_ref[rows, :].astype(F32) + dtb_ref[...])
        a_cum = _cumsum_rows(dt * a_neg)
        full = per_channel(jnp.concatenate([dt, a_cum], axis=0))
        dt_c, ac_c = full[0:CHUNK, :], full[CHUNK:, :]
        a_last = ac_c[CHUNK - 1:CHUNK, :]
        xs = act[rows, 0:WIDTH]
        xdt = xs * dt_c
        xdt_end = xdt * jnp.exp(a_last - ac_c)
        from_start = jnp.exp(ac_c)
        chunk_decay = jnp.exp(a_last)
        bm = [act[rows, WIDTH + g * SSM_STATE:WIDTH + (g + 1) * SSM_STATE] for g in range(SSM_GROUPS)]
        cm = [act[rows, WIDTH + (SSM_GROUPS + g) * SSM_STATE:WIDTH + (SSM_GROUPS + g + 1) * SSM_STATE]
              for g in range(SSM_GROUPS)]
        cb2 = [_bdot_nt(cm[g], jnp.concatenate([bm[g], bm[g]], axis=0)) for g in range(SSM_GROUPS)]
        for p in range(npair):
            g = p // pair_group
            ps = slice(p * LANE, (p + 1) * LANE)
            seg_l = ac_c[:, ps]
            seg_s = jnp.sum(jnp.where(diag, seg_l, 0.0), axis=0, keepdims=True)
            m = cb2[g] * jnp.exp(jnp.where(causal, seg_l - seg_s, -jnp.inf))
            xd = xdt[:, ps]
            xd2 = jnp.concatenate([jnp.where(left, xd, 0.0), jnp.where(left, 0.0, xd)],
                                  axis=0)
            prev = state[p]
            y = _bdot(m, xd2) + _bdot(cm[g], prev) * from_start[:, ps] + xs[:, ps] * dsk_ref[:, ps]
            state[p] = prev * chunk_decay[:, ps] + _bdot_tn(bm[g], xdt_end[:, ps])
            ybuf[rows, ps] = y
        return carry

    lax.fori_loop(0, blk // CHUNK, chunk, 0)

    y = ybuf[...] * _silu(z_ref[...].astype(F32))
    ms = jnp.mean(y * y, axis=-1, keepdims=True)
    o_ref[...] = (y * lax.rsqrt(ms + EPS) * nw_ref[...]).astype(o_ref.dtype)


def _ssd_branch(proj, b, s, conv_w, conv_b, dt_bias, a_log, d_skip, norm_w, *, blk=256):
    nblk = s // blk
    pad = lambda v: jnp.pad(v.reshape(1, -1), ((0, 0), (0, LANE - v.shape[0])))
    dsk = jnp.repeat(d_skip, SSM_HEAD_DIM).reshape(1, WIDTH)
    row = lambda i, j: i * nblk + j
    const = lambda i, j: (0, 0)
    return pl.pallas_call(
        functools.partial(_ssd_kernel, blk=blk),
        grid=(b, nblk),
        in_specs=[pl.BlockSpec((blk, WIDTH), lambda i, j: (row(i, j), _col_block("z"))),
                  pl.BlockSpec((blk, SSM_XBC), lambda i, j: (row(i, j), _col_block("xbc"))),
                  pl.BlockSpec((blk, LANE), lambda i, j: (row(i, j), _col_block("dt"))),
                  pl.BlockSpec((SSM_CONV, SSM_XBC), const),
                  pl.BlockSpec((1, SSM_XBC), const),
                  pl.BlockSpec((1, LANE), const),
                  pl.BlockSpec((1, LANE), const),
                  pl.BlockSpec((1, WIDTH), const),
                  pl.BlockSpec((1, WIDTH), const)],
        out_specs=pl.BlockSpec((blk, WIDTH), lambda i, j: (row(i, j), 0)),
        out_shape=jax.ShapeDtypeStruct((b * s, WIDTH), BF16),
        scratch_shapes=[pltpu.VMEM((blk + 2 * SUBLANE, SSM_XBC), F32),
                        pltpu.VMEM((blk, SSM_XBC), F32),
                        pltpu.VMEM((SSM_HEADS // 2, SSM_STATE, LANE), F32),
                        pltpu.VMEM((blk, WIDTH), F32)],
        compiler_params=_compiler_params(("parallel", "arbitrary")),
        name="ssd_branch",
    )(proj, proj, proj, conv_w, conv_b.reshape(1, -1), pad(dt_bias), pad(a_log), dsk,
      norm_w.reshape(1, -1))


def _lru_kernel(gate_ref, x_ref, cw_ref, cb_ref, wa_ref, ba_ref, wi_ref, bi_ref, lam_ref, o_ref,
                cbuf, a_s, u_s, carry_s, *, blk):
    @pl.when(pl.program_id(1) == 0)
    def _():
        cbuf[0:SUBLANE, :] = jnp.zeros((SUBLANE, WIDTH), F32)
        carry_s[...] = jnp.zeros_like(carry_s)

    cbuf[SUBLANE:SUBLANE + blk, :] = x_ref[...].astype(F32)
    xc = cb_ref[...]
    for k in range(LRU_CONV):
        s = LRU_CONV - 1 - k
        xc = xc + cw_ref[k:k + 1, :] * cbuf[SUBLANE - s:SUBLANE - s + blk, :]
    cbuf[0:SUBLANE, :] = cbuf[blk:blk + SUBLANE, :]

    nsl = wa_ref.shape[0]
    wsl = WIDTH // nsl
    xcb = xc.astype(BF16)
    ra = jnp.concatenate([jnp.dot(xcb[:, q * wsl:(q + 1) * wsl], wa_ref[q], preferred_element_type=F32)
                          for q in range(nsl)], axis=1)
    ri = jnp.concatenate([jnp.dot(xcb[:, q * wsl:(q + 1) * wsl], wi_ref[q], preferred_element_type=F32)
                          for q in range(nsl)], axis=1)
    r = jax.nn.sigmoid(ra + ba_ref[...])
    i = jax.nn.sigmoid(ri + bi_ref[...])
    log_a = -LRU_C * r * _softplus(-lam_ref[...])
    a = jnp.exp(log_a)
    a_s[...] = a
    u_s[...] = jnp.sqrt(-jnp.tanh(log_a) * (a * a + 1.0)) * (i * xc)

    rows = lax.broadcasted_iota(jnp.int32, (SUBLANE, WIDTH), 0)

    def group(g, carry):
        r0 = pl.multiple_of(g * SUBLANE, SUBLANE)
        a = a_s[pl.ds(r0, SUBLANE), :]
        u = u_s[pl.ds(r0, SUBLANE), :]
        for k in (1, 2, 4):
            keep = rows >= k
            a_prev = jnp.where(keep, pltpu.roll(a, k, 0), 1.0)
            u_prev = jnp.where(keep, pltpu.roll(u, k, 0), 0.0)
            u = a * u_prev + u
            a = a * a_prev
        h = a * carry + u
        u_s[pl.ds(r0, SUBLANE), :] = h
        return jnp.broadcast_to(h[SUBLANE - 1:SUBLANE, :], (SUBLANE, WIDTH))

    carry_s[...] = lax.fori_loop(0, blk // SUBLANE, group, carry_s[...])
    o_ref[...] = (u_s[...] * jax.nn.gelu(gate_ref[...].astype(F32), approximate=True)).astype(o_ref.dtype)


def _block_diag_slabs(w, per_slab):
    g, n, _ = w.shape
    w = w.reshape(g // per_slab, per_slab, n, n)
    eye = jnp.eye(per_slab, dtype=w.dtype)
    out = jnp.einsum("spij,pq->spiqj", w, eye)
    return out.reshape(g // per_slab, per_slab * n, per_slab * n)


def _lru_branch(proj, b, s, conv_w, conv_b, w_a, b_a, w_i, b_i, lam, *, blk=256):
    nblk = s // blk
    per_slab = 4
    wa = _block_diag_slabs(w_a, per_slab).astype(BF16)
    wi = _block_diag_slabs(w_i, per_slab).astype(BF16)
    row = lambda i, j: i * nblk + j
    const = lambda i, j: (0, 0)
    const3 = lambda i, j: (0, 0, 0)
    vec = lambda v: v.reshape(1, WIDTH)
    return pl.pallas_call(
        functools.partial(_lru_kernel, blk=blk),
        grid=(b, nblk),
        in_specs=[pl.BlockSpec((blk, WIDTH), lambda i, j: (row(i, j), _col_block("lgate"))),
                  pl.BlockSpec((blk, WIDTH), lambda i, j: (row(i, j), _col_block("lx"))),
                  pl.BlockSpec((LRU_CONV, WIDTH), const),
                  pl.BlockSpec((1, WIDTH), const),
                  pl.BlockSpec(wa.shape, const3),
                  pl.BlockSpec((1, WIDTH), const),
                  pl.BlockSpec(wi.shape, const3),
                  pl.BlockSpec((1, WIDTH), const),
                  pl.BlockSpec((1, WIDTH), const)],
        out_specs=pl.BlockSpec((blk, WIDTH), lambda i, j: (row(i, j), 0)),
        out_shape=jax.ShapeDtypeStruct((b * s, WIDTH), BF16),
        scratch_shapes=[pltpu.VMEM((blk + 2 * SUBLANE, WIDTH), F32),
                        pltpu.VMEM((blk, WIDTH), F32),
                        pltpu.VMEM((blk, WIDTH), F32),
                        pltpu.VMEM((SUBLANE, WIDTH), F32)],
        compiler_params=_compiler_params(("parallel", "arbitrary")),
        name="rglru_branch",
    )(proj, proj, conv_w, vec(conv_b), wa, vec(b_a), wi, vec(b_i), vec(lam))


def _ret_kernel(q_ref, k_ref, v_ref, g_ref, cos_ref, sin_ref, inner_ref, kte_ref, qfs_ref, cdec_ref,
                nw_ref, o_ref, state, ybuf, *, blk):
    @pl.when(pl.program_id(1) == 0)
    def _():
        state[...] = jnp.zeros_like(state)

    lane = lax.broadcasted_iota(jnp.int32, (CHUNK, RET_QK), 1)
    first_half = (lane % RET_QK_DIM) < (RET_QK_DIM // 2)
    half = RET_QK_DIM // 2

    def rotary(x, cos, sin):
        swapped = jnp.where(first_half, pltpu.roll(x, RET_QK - half, 1), pltpu.roll(x, half, 1))
        return x * cos + swapped * sin

    left = lax.broadcasted_iota(jnp.int32, (CHUNK, LANE), 1) < RET_QK_DIM
    npair = RET_HEADS // 2

    def halves(x):
        return jnp.concatenate([jnp.where(left, x, 0.0), jnp.where(left, 0.0, x)], axis=0)

    def chunk(c, carry):
        r0 = pl.multiple_of(c * CHUNK, CHUNK)
        rows = pl.ds(r0, CHUNK)
        cos = cos_ref[rows, :]
        sin = sin_ref[rows, :]
        q = rotary(q_ref[rows, :].astype(F32), cos, sin)
        k = rotary(k_ref[rows, :].astype(F32), cos, sin) * (RET_QK_DIM ** -0.5)
        k_end = k * kte_ref[...]
        q_start = q * qfs_ref[...]
        slabs = [slice(p * LANE, (p + 1) * LANE) for p in range(npair)]
        v2 = [v_ref[rows, 2 * p * RET_V_DIM:(2 * p + 2) * RET_V_DIM] for p in range(npair)]
        v2 = [jnp.concatenate([v[:, 0:RET_V_DIM], v[:, RET_V_DIM:]], axis=0) for v in v2]
        prev = [state[p] for p in range(npair)]
        s2 = [_bdot_nt(q[:, ps], halves(k[:, ps])) * inner_ref[p] for p, ps in enumerate(slabs)]
        kv = [_bdot_tn(halves(k_end[:, ps]), v2[p]) for p, ps in enumerate(slabs)]
        for p, ps in enumerate(slabs):
            rhs = jnp.concatenate([v2[p], prev[p]], axis=0)
            for e in range(2):
                keep = left if e == 0 else jnp.logical_not(left)
                lhs = jnp.concatenate([jnp.where(keep, s2[p], 0.0), jnp.where(keep, q_start[:, ps], 0.0)],
                                      axis=1)
                y = _bdot(lhs, rhs)
                mu = jnp.mean(y, axis=-1, keepdims=True)
                yc = y - mu
                var = jnp.mean(yc * yc, axis=-1, keepdims=True)
                h = 2 * p + e
                ybuf[rows, h * RET_V_DIM:(h + 1) * RET_V_DIM] = yc * lax.rsqrt(var + EPS)
            state[p] = prev[p] * cdec_ref[p] + kv[p]
        return carry

    lax.fori_loop(0, blk // CHUNK, chunk, 0)
    o_ref[...] = (_silu(g_ref[...].astype(F32)) * (ybuf[...] * nw_ref[...])).astype(o_ref.dtype)


def _ret_tables(s):
    pos_s = jnp.arange(s, dtype=F32)
    inv_freq = ROPE_BASE ** (-jnp.arange(0, RET_QK_DIM, 2, dtype=F32) / RET_QK_DIM)
    ang = pos_s[:, None] * inv_freq[None, :]
    cos, sin = jnp.cos(ang), jnp.sin(ang)
    cos_f = jnp.tile(jnp.concatenate([cos, cos], axis=1), (1, RET_HEADS))
    sin_f = jnp.tile(jnp.concatenate([-sin, sin], axis=1), (1, RET_HEADS))
    log_gamma = jnp.log1p(-jnp.exp2(-5.0 - jnp.arange(RET_HEADS, dtype=F32)))
    pos = jnp.arange(CHUNK, dtype=F32)
    inner = jnp.exp(log_gamma[:, None, None] * jnp.abs(pos[:, None] - pos[None, :]))
    k_to_end = jnp.exp(log_gamma[:, None] * (CHUNK - 1.0 - pos))
    q_from_start = jnp.exp(log_gamma[:, None] * (pos + 1.0))
    kte = jnp.repeat(k_to_end.T, RET_QK_DIM, axis=1)
    qfs = jnp.repeat(q_from_start.T, RET_QK_DIM, axis=1)
    inner2 = inner.reshape(RET_HEADS // 2, 2, CHUNK, CHUNK).transpose(0, 2, 1, 3).reshape(
        RET_HEADS // 2, CHUNK, 2 * CHUNK)
    cdec = jnp.broadcast_to(jnp.repeat(jnp.exp(log_gamma * CHUNK), RET_QK_DIM).reshape(
        RET_HEADS // 2, 2 * RET_QK_DIM, 1), (RET_HEADS // 2, 2 * RET_QK_DIM, RET_V_DIM))
    return cos_f, sin_f, inner2, kte, qfs, cdec


def _ret_branch(proj, b, s, norm_w, tables, *, blk=256):
    nblk = s // blk
    cos_f, sin_f, inner, kte, qfs, cdec = tables
    row = lambda i, j: i * nblk + j
    const = lambda i, j: (0, 0)
    return pl.pallas_call(
        functools.partial(_ret_kernel, blk=blk),
        grid=(b, nblk),
        in_specs=[pl.BlockSpec((blk, RET_QK), lambda i, j: (row(i, j), _col_block("rq"))),
                  pl.BlockSpec((blk, RET_QK), lambda i, j: (row(i, j), _col_block("rk"))),
                  pl.BlockSpec((blk, WIDTH), lambda i, j: (row(i, j), _col_block("rv"))),
                  pl.BlockSpec((blk, WIDTH), lambda i, j: (row(i, j), _col_block("rg"))),
                  pl.BlockSpec((blk, RET_QK), lambda i, j: (j, 0)),
                  pl.BlockSpec((blk, RET_QK), lambda i, j: (j, 0)),
                  pl.BlockSpec(inner.shape, lambda i, j: (0, 0, 0)),
                  pl.BlockSpec((CHUNK, RET_QK), const),
                  pl.BlockSpec((CHUNK, RET_QK), const),
                  pl.BlockSpec(cdec.shape, lambda i, j: (0, 0, 0)),
                  pl.BlockSpec((1, WIDTH), const)],
        out_specs=pl.BlockSpec((blk, WIDTH), lambda i, j: (row(i, j), 0)),
        out_shape=jax.ShapeDtypeStruct((b * s, WIDTH), BF16),
        scratch_shapes=[pltpu.VMEM((RET_HEADS // 2, 2 * RET_QK_DIM, RET_V_DIM), F32),
                        pltpu.VMEM((blk, WIDTH), F32)],
        compiler_params=_compiler_params(("parallel", "arbitrary")),
        name="retention_branch",
    )(proj, proj, proj, proj, cos_f, sin_f, inner, kte, qfs, cdec, norm_w.reshape(1, -1))


def _rwkv_kernel(r_ref, k_ref, v_ref, l_ref, mur_ref, muk_ref, muv_ref, mul_ref, w0_ref, w2_ref,
                 a0_ref, a2_ref, g2_ref, kk_ref, ka_ref, rk_ref, lnw_ref, lnb_ref, o_ref,
                 tail, tail_l, r_s, k_s, v_s, kk_s, b_s, lw_s, o_s, ub_s, op_s, wall_s, state, *, blk, hw):
    nh = hw // RWKV_HEAD

    @pl.when(pl.program_id(2) == 0)
    def _():
        tail[...] = jnp.zeros_like(tail)
        tail_l[...] = jnp.zeros_like(tail_l)
        state[...] = jnp.zeros_like(state)

    rows = lax.broadcasted_iota(jnp.int32, (blk, 1), 0)

    def shift_mix(x, prev_row, mu):
        prev = jnp.where(rows == 0, prev_row, pltpu.roll(x, 1, 0))
        return x + (prev - x) * mu

    lane = lax.broadcasted_iota(jnp.int32, (LANE, LANE), 1) // RWKV_HEAD
    lrow = lax.broadcasted_iota(jnp.int32, (LANE, LANE), 0) // RWKV_HEAD
    head_ones = (lane == lrow).astype(BF16)

    def head_sum(x):
        hi = x.astype(BF16)
        lo = (x - hi.astype(F32)).astype(BF16)
        return jnp.concatenate(
            [jnp.dot(hi[:, q:q + LANE], head_ones, preferred_element_type=F32)
             + jnp.dot(lo[:, q:q + LANE], head_ones, preferred_element_type=F32)
             for q in range(0, hw, LANE)], axis=1)

    r_raw, k_raw, v_raw, l_raw = (ref[...].astype(F32) for ref in (r_ref, k_ref, v_ref, l_ref))
    r = shift_mix(r_raw, tail[0:1, :], mur_ref[...])
    k = shift_mix(k_raw, tail[1:2, :], muk_ref[...])
    v = shift_mix(v_raw, tail[2:3, :], muv_ref[...])
    lo_ = shift_mix(l_raw, tail_l[0:1, :], mul_ref[...])
    tail[0:1, :] = r_raw[blk - 1:blk, :]
    tail[1:2, :] = k_raw[blk - 1:blk, :]
    tail[2:3, :] = v_raw[blk - 1:blk, :]
    tail_l[0:1, :] = l_raw[blk - 1:blk, :]

    wl = lo_[:, 0:RWKV_W_LORA]
    al = lo_[:, RWKV_W_LORA:RWKV_W_LORA + RWKV_A_LORA]
    gl = lo_[:, RWKV_W_LORA + RWKV_A_LORA:]
    wz = w0_ref[...] + _bdot(jnp.tanh(wl), w2_ref[...])
    w = jnp.minimum(wz, 0.0) - jnp.log(1.0 + jnp.exp(-jnp.abs(wz))) - 0.5
    a = jax.nn.sigmoid(a0_ref[...] + _bdot(al, a2_ref[...]))
    gate = _bdot(jax.nn.sigmoid(gl), g2_ref[...])
    kk = k * kk_ref[...]
    kk = kk * lax.rsqrt(jnp.maximum(head_sum(kk * kk), 1e-24))
    k = k * (1.0 + (a - 1.0) * ka_ref[...])
    r_s[...] = r
    k_s[...] = k
    v_s[...] = v
    kk_s[...] = kk
    b_s[...] = kk * a
    lw_s[...] = -jnp.exp(w)

    trow = lax.broadcasted_iota(jnp.int32, (CHUNK, 2 * CHUNK), 0)
    tcol = lax.broadcasted_iota(jnp.int32, (CHUNK, 2 * CHUNK), 1) % CHUNK
    strict = tcol < trow
    incl = tcol <= trow
    left = lax.broadcasted_iota(jnp.int32, (CHUNK, LANE), 1) < RWKV_HEAD
    left2 = lax.broadcasted_iota(jnp.int32, (2 * CHUNK, LANE), 1) < RWKV_HEAD

    eye2 = (tcol == trow).astype(F32)
    zeros = jnp.zeros((CHUNK, LANE), F32)
    slabs = [slice(p * LANE, (p + 1) * LANE) for p in range(nh // 2)]
    lcat = lambda a, b: jnp.concatenate([a, b], axis=1)
    rcat = lambda *a: jnp.concatenate(a, axis=0)
    keep_l = lambda a: jnp.where(left, a, 0.0)
    keep_r = lambda a: jnp.where(left, 0.0, a)

    def phase_a(g, carry):
        cs = [g * A_UNROLL + i for i in range(A_UNROLL)]
        sls = [pl.ds(pl.multiple_of(c * CHUNK, CHUNK), CHUNK) for c in cs]
        pre = []
        for sl in sls:
            lw = lw_s[sl, :]
            cum = _cumsum_rows(lw)
            cum_last = cum[CHUNK - 1:CHUNK, :]
            w_inv = jnp.exp(-cum)
            to_end = jnp.exp(cum_last - cum)
            pre.append(dict(rt=r_s[sl, :] * jnp.exp(cum), kt=k_s[sl, :] * w_inv, bt=b_s[sl, :] * w_inv,
                            kap=kk_s[sl, :] * jnp.exp(cum - lw), k_end=k_s[sl, :] * to_end,
                            b_end=b_s[sl, :] * to_end, w_all=jnp.exp(cum_last), vv=v_s[sl, :]))
        items = [(i, ps) for i in range(A_UNROLL) for ps in slabs]
        pm2 = []
        for i, ps in items:
            d = pre[i]
            lhs = rcat(d["kap"][:, ps], d["rt"][:, ps])
            rhs_e = jnp.where(left2, rcat(d["bt"][:, ps], d["kt"][:, ps]), 0.0)
            rhs_o = jnp.where(left2, 0.0, rcat(d["kt"][:, ps], d["bt"][:, ps]))
            pm2.append(_bdot_nt(lhs, rcat(rhs_e, rhs_o)))
        top = [(jnp.where(strict, m[0:CHUNK, 0:LANE], 0.0), jnp.where(strict, m[0:CHUNK, LANE:], 0.0))
               for m in pm2]
        bot = [(jnp.where(incl, m[CHUNK:, 0:LANE], 0.0), jnp.where(incl, m[CHUNK:, LANE:], 0.0))
               for m in pm2]
        v_swp = [pltpu.roll(pre[i]["vv"][:, ps], RWKV_HEAD, 1) for i, ps in items]
        v_oe = [rcat(keep_l(v), keep_r(v)) for v in v_swp]
        av = [_bdot(jnp.where(left, to, te), v_oe[n]) for n, (te, to) in enumerate(top)]
        op = [_bdot(jnp.where(left, bo, be), v_oe[n]) for n, (be, bo) in enumerate(bot)]
        z = [(jnp.where(left, -te, eye2), jnp.where(left, eye2, -to)) for te, to in top]

        def step(ze, zo):
            return _bdot(jnp.where(left, ze, zo), rcat(lcat(ze, zeros), lcat(zeros, zo)))

        span = 1
        while 2 * span < CHUNK:
            res = [step(ze, zo) for ze, zo in z]
            z = [(r[:, 0:LANE] + keep_r(ze), r[:, LANE:] + keep_l(zo)) for r, (ze, zo) in zip(res, z)]
            span *= 2
        res = [step(ze, zo) for ze, zo in z]
        t_oe = [jnp.where(left, r[:, LANE:] + zo, r[:, 0:LANE] + ze) for r, (ze, zo) in zip(res, z)]
        for n, (i, ps) in enumerate(items):
            kap = pre[i]["kap"][:, ps]
            prod = _bdot(t_oe[n], lcat(rcat(keep_l(av[n]), keep_r(av[n])), rcat(keep_r(kap), keep_l(kap))))
            sl = sls[i]
            lw_s[sl, ps] = prod[:, 0:LANE]
            kk_s[sl, ps] = prod[:, LANE:]
            ub_s[sl, ps] = jnp.where(left, bot[n][0], bot[n][1])
            op_s[sl, ps] = op[n]
            v_s[sl, ps] = v_swp[n]
        for i, sl in enumerate(sls):
            r_s[sl, :] = pre[i]["rt"]
            k_s[sl, :] = pre[i]["k_end"]
            b_s[sl, :] = pre[i]["b_end"]
            wall_s[cs[i]] = jnp.broadcast_to(pre[i]["w_all"], (SUBLANE, hw))
        return carry

    lax.fori_loop(0, blk // CHUNK // A_UNROLL, phase_a, 0)

    def phase_b(c, carry):
        sl = pl.ds(pl.multiple_of(c * CHUNK, CHUNK), CHUNK)
        w_all = wall_s[c]
        s0 = [(state[2 * p], state[2 * p + 1]) for p in range(nh // 2)]
        s_oe = [rcat(so, se) for se, so in s0]
        u = [-(lw_s[sl, ps] + _bdot_nt(kk_s[sl, ps], s_oe[p])) for p, ps in enumerate(slabs)]
        op = [op_s[sl, ps] + _bdot_nt(r_s[sl, ps], s_oe[p]) for p, ps in enumerate(slabs)]
        for p, ps in enumerate(slabs):
            o_sw = _bdot(ub_s[sl, ps], rcat(keep_r(u[p]), keep_l(u[p]))) + op[p]
            o_s[sl, ps] = pltpu.roll(o_sw, RWKV_HEAD, 1)
        for p, ps in enumerate(slabs):
            add = _bdot_tn(rcat(u[p], v_s[sl, ps]), rcat(b_s[sl, ps], k_s[sl, ps]))
            state[2 * p] = s0[p][0] * w_all[0:1, ps] + keep_l(add[RWKV_HEAD:, :])
            state[2 * p + 1] = s0[p][1] * w_all[0:1, ps] + keep_r(add[0:RWKV_HEAD, :])
        return carry

    lax.fori_loop(0, blk // CHUNK, phase_b, 0)

    o = o_s[...]
    mu = head_sum(o) * (1.0 / RWKV_HEAD)
    oc = o - mu
    var = head_sum(oc * oc) * (1.0 / RWKV_HEAD)
    o = oc * lax.rsqrt(var + RWKV_LN_EPS) * lnw_ref[...] + lnb_ref[...]
    o = o + head_sum(r * k * rk_ref[...]) * v
    o_ref[...] = (o * gate).astype(o_ref.dtype)


def _rwkv_branch(proj, b, s, mu, w0, w2, a0, a2, g2, k_k, k_a, r_k, ln_w, ln_b, *, blk=256, hw=WIDTH):
    nblk = s // blk
    nslab = WIDTH // hw
    row = lambda i, p, j: i * nblk + j
    vec = lambda v: v.reshape(1, -1)
    hcol = lambda i, p, j: (0, p)
    mu_r, mu_k, mu_v, mu_l = (mu[0:WIDTH], mu[WIDTH:2 * WIDTH], mu[2 * WIDTH:3 * WIDTH],
                              mu[3 * WIDTH:])
    wblk = _col_block("wr")
    lblk = _col_block("lora")
    return pl.pallas_call(
        functools.partial(_rwkv_kernel, blk=blk, hw=hw),
        grid=(b, nslab, nblk),
        in_specs=[pl.BlockSpec((blk, hw), lambda i, p, j: (row(i, p, j), wblk * nslab + p)),
                  pl.BlockSpec((blk, hw), lambda i, p, j: (row(i, p, j), (wblk + 1) * nslab + p)),
                  pl.BlockSpec((blk, hw), lambda i, p, j: (row(i, p, j), (wblk + 2) * nslab + p)),
                  pl.BlockSpec((blk, RWKV_LORA), lambda i, p, j: (row(i, p, j), lblk)),
                  pl.BlockSpec((1, hw), hcol),
                  pl.BlockSpec((1, hw), hcol),
                  pl.BlockSpec((1, hw), hcol),
                  pl.BlockSpec((1, RWKV_LORA), lambda i, p, j: (0, 0)),
                  pl.BlockSpec((1, hw), hcol),
                  pl.BlockSpec((RWKV_W_LORA, hw), hcol),
                  pl.BlockSpec((1, hw), hcol),
                  pl.BlockSpec((RWKV_A_LORA, hw), hcol),
                  pl.BlockSpec((RWKV_G_LORA, hw), hcol),
                  pl.BlockSpec((1, hw), hcol),
                  pl.BlockSpec((1, hw), hcol),
                  pl.BlockSpec((1, hw), hcol),
                  pl.BlockSpec((1, hw), hcol),
                  pl.BlockSpec((1, hw), hcol)],
        out_specs=pl.BlockSpec((blk, hw), lambda i, p, j: (row(i, p, j), p)),
        out_shape=jax.ShapeDtypeStruct((b * s, WIDTH), BF16),
        scratch_shapes=[pltpu.VMEM((SUBLANE, hw), F32), pltpu.VMEM((SUBLANE, RWKV_LORA), F32)]
        + [pltpu.VMEM((blk, hw), F32)] * 9
        + [pltpu.VMEM((blk // CHUNK, SUBLANE, hw), F32)]
        + [pltpu.VMEM((hw // RWKV_HEAD, RWKV_HEAD, LANE), F32)],
        compiler_params=_compiler_params(("parallel", "parallel", "arbitrary")),
        name="rwkv7_branch",
    )(proj, proj, proj, proj, vec(mu_r), vec(mu_k), vec(mu_v), vec(mu_l), vec(w0), w2.astype(BF16),
      vec(a0), a2.astype(BF16), g2.astype(BF16), vec(k_k), vec(k_a), vec(r_k), vec(ln_w), vec(ln_b))


def _merge_kernel(y0_ref, y1_ref, y2_ref, y3_ref, gl_ref, x_ref, wb_ref, wo_ref, o_ref):
    merged = None
    for m, y_ref in enumerate((y0_ref, y1_ref, y2_ref, y3_ref)):
        br = jnp.dot(y_ref[...], wb_ref[m], preferred_element_type=F32)
        term = jax.nn.sigmoid(gl_ref[:, m * D_MODEL:(m + 1) * D_MODEL].astype(F32)) * br
        merged = term if merged is None else merged + term
    o_ref[...] = x_ref[...] + jnp.dot(merged.astype(BF16), wo_ref[...], preferred_element_type=F32)


def _merge(ys, proj, x2, w_branch, w_out, *, tm=256):
    t = x2.shape[0]
    tok = lambda i: (i, 0)
    return pl.pallas_call(
        _merge_kernel,
        grid=(t // tm,),
        in_specs=[pl.BlockSpec((tm, WIDTH), tok)] * N_BRANCH
        + [pl.BlockSpec((tm, N_BRANCH * D_MODEL), lambda i: (i, _col_block("gates"))),
           pl.BlockSpec((tm, D_MODEL), tok),
           pl.BlockSpec((N_BRANCH, WIDTH, D_MODEL), lambda i: (0, 0, 0)),
           pl.BlockSpec((D_MODEL, D_MODEL), lambda i: (0, 0))],
        out_specs=pl.BlockSpec((tm, D_MODEL), tok),
        out_shape=jax.ShapeDtypeStruct((t, D_MODEL), F32),
        compiler_params=_compiler_params(("parallel",)),
        name="branch_merge",
    )(*ys, proj, x2, w_branch.astype(BF16), w_out.astype(BF16))


FFN_TC = 256


def _ffn_kernel(x_ref, g_ref, wu_ref, cw_ref, cb_ref, wd_ref, gf_ref, o_ref, cbuf, tail, *, tm, final):
    @pl.when(pl.program_id(1) == 0)
    def _():
        tail[...] = jnp.zeros_like(tail)

    x = x_ref[...]
    ms = jnp.mean(x * x, axis=-1, keepdims=True)
    hn = (x * lax.rsqrt(ms + EPS) * g_ref[...]).astype(BF16)
    acc = x
    for i in range(D_FF // FFN_TC):
        halves = []
        for half in range(2):
            c0 = half * D_FF + i * FFN_TC
            cs = slice(c0, c0 + FFN_TC)
            cbuf[0:SUBLANE, :] = tail[:, cs]
            cbuf[SUBLANE:SUBLANE + tm, :] = jnp.dot(hn, wu_ref[:, cs], preferred_element_type=F32)
            y = cb_ref[:, cs]
            for k in range(FFN_CONV):
                s = FFN_CONV - 1 - k
                y = y + cw_ref[k:k + 1, cs] * cbuf[SUBLANE - s:SUBLANE - s + tm, :]
            tail[:, cs] = cbuf[tm:tm + SUBLANE, :]
            halves.append(y)
        val, gt = halves
        acc = acc + jnp.dot((_silu(gt) * val).astype(BF16), wd_ref[i * FFN_TC:(i + 1) * FFN_TC, :],
                            preferred_element_type=F32)
    if final:
        ms = jnp.mean(acc * acc, axis=-1, keepdims=True)
        acc = acc * lax.rsqrt(ms + EPS) * gf_ref[...]
    o_ref[...] = acc


def _conv_ffn(x2, b, s, gain, w_up, conv_w, conv_b, w_down, gain_final, *, final, tm=512):
    nblk = s // tm
    row = lambda i, j: (i * nblk + j, 0)
    const = lambda i, j: (0, 0)
    return pl.pallas_call(
        functools.partial(_ffn_kernel, tm=tm, final=final),
        grid=(b, nblk),
        in_specs=[pl.BlockSpec((tm, D_MODEL), row),
                  pl.BlockSpec((1, D_MODEL), const),
                  pl.BlockSpec((D_MODEL, 2 * D_FF), const),
                  pl.BlockSpec((FFN_CONV, 2 * D_FF), const),
                  pl.BlockSpec((1, 2 * D_FF), const),
                  pl.BlockSpec((D_FF, D_MODEL), const),
                  pl.BlockSpec((1, D_MODEL), const)],
        out_specs=pl.BlockSpec((tm, D_MODEL), row),
        out_shape=jax.ShapeDtypeStruct((b * s, D_MODEL), F32),
        scratch_shapes=[pltpu.VMEM((tm + 2 * SUBLANE, FFN_TC), F32),
                        pltpu.VMEM((SUBLANE, 2 * D_FF), F32)],
        compiler_params=_compiler_params(("parallel", "arbitrary")),
        name="conv_ffn",
    )(x2, gain.reshape(1, -1), w_up.astype(BF16), conv_w, conv_b.reshape(1, -1),
      w_down.astype(BF16), gain_final.reshape(1, -1))


def _reorder_w_in(w):
    cols = []
    for n in _DST_ORDER:
        off, wd = _SRC[n]
        cols.append(w[:, off:off + wd].astype(BF16))
    pad = N_PROJ - _DST["dt"][0] - SSM_HEADS
    cols.append(jnp.zeros((w.shape[0], pad), BF16))
    return jnp.concatenate(cols, axis=1)


def kernel(x, norm_mix, w_in, ssm_conv_w, ssm_conv_b, ssm_dt_bias, ssm_a_log, ssm_d, ssm_norm, lru_conv_w, lru_conv_b, lru_w_a, lru_b_a, lru_w_i, lru_b_i, lru_lam, ret_norm, rwkv_mu, rwkv_w0, rwkv_w2, rwkv_a0, rwkv_a2, rwkv_g2, rwkv_k_k, rwkv_k_a, rwkv_r_k, rwkv_ln_w, rwkv_ln_b, w_branch, w_out, norm_ffn, ffn_up, ffn_conv_w, ffn_conv_b, ffn_down, norm_final):
    b, s, d = x.shape
    x2 = x.reshape(b * s, d)
    tables = _ret_tables(s)
    for l in range(DEPTH):
        proj = _norm_matmul(x2, norm_mix[l], _reorder_w_in(w_in[l]))
        y_ssd = _ssd_branch(proj, b, s, ssm_conv_w[l], ssm_conv_b[l], ssm_dt_bias[l], ssm_a_log[l],
                            ssm_d[l], ssm_norm[l])
        y_lru = _lru_branch(proj, b, s, lru_conv_w[l], lru_conv_b[l], lru_w_a[l], lru_b_a[l],
                            lru_w_i[l], lru_b_i[l], lru_lam[l])
        y_ret = _ret_branch(proj, b, s, ret_norm[l], tables)
        y_rwkv = _rwkv_branch(proj, b, s, rwkv_mu[l], rwkv_w0[l], rwkv_w2[l], rwkv_a0[l], rwkv_a2[l],
                              rwkv_g2[l], rwkv_k_k[l], rwkv_k_a[l], rwkv_r_k[l], rwkv_ln_w[l],
                              rwkv_ln_b[l])
        x2 = _merge((y_ssd, y_lru, y_ret, y_rwkv), proj, x2, w_branch[l], w_out[l])
        x2 = _conv_ffn(x2, b, s, norm_ffn[l], ffn_up[l], ffn_conv_w[l], ffn_conv_b[l], ffn_down[l],
                       norm_final, final=(l == DEPTH - 1))
    return x2.reshape(b, s, d)
```

```python
import functools

import jax
import jax.numpy as jnp
from jax import lax
from jax.experimental import pallas as pl
from jax.experimental.pallas import tpu as pltpu

F32 = jnp.float32
BF16 = jnp.bfloat16

D_MODEL = 1024
WIDTH = 1024
DEPTH = 2
CHUNK = 64
EPS = 1e-6

SSM_HEADS = 16
SSM_HEAD_DIM = 64
SSM_GROUPS = 4
SSM_STATE = 128
SSM_CONV = 4
SSM_XBC = WIDTH + 2 * SSM_GROUPS * SSM_STATE

LRU_BLOCKS = 16
LRU_BLOCK = 64
LRU_CONV = 4
LRU_C = 8.0

RET_HEADS = 8
RET_QK_DIM = 64
RET_V_DIM = 128
RET_QK = RET_HEADS * RET_QK_DIM
ROPE_BASE = 10000.0

RWKV_HEAD = 64
RWKV_HEADS = 16
RWKV_W_LORA = 64
RWKV_A_LORA = 64
RWKV_G_LORA = 128
RWKV_LORA = RWKV_W_LORA + RWKV_A_LORA + RWKV_G_LORA
RWKV_LN_EPS = 64e-5
A_UNROLL = 2

D_FF = 2816
FFN_CONV = 3
N_BRANCH = 4

LANE = 128
SUBLANE = 8
VMEM_LIMIT = 56 * 1024 * 1024

_SRC = {}
_o = 0
for _n, _w in (("z", WIDTH), ("xbc", SSM_XBC), ("dt", SSM_HEADS), ("lgate", WIDTH), ("lx", WIDTH),
               ("rq", RET_QK), ("rk", RET_QK), ("rv", WIDTH), ("rg", WIDTH),
               ("wr", WIDTH), ("wk", WIDTH), ("wv", WIDTH), ("lora", RWKV_LORA),
               ("gates", N_BRANCH * D_MODEL)):
    _SRC[_n] = (_o, _w)
    _o += _w
N_IN = _o
_DST_ORDER = ("gates", "xbc", "z", "lgate", "lx", "rv", "rg", "rq", "rk", "wr", "wk", "wv", "lora", "dt")
_DST = {}
_o = 0
for _n in _DST_ORDER:
    _w = _SRC[_n][1]
    _bw = max(_w, LANE)
    assert _o % _bw == 0, (_n, _o, _bw)
    _DST[_n] = (_o, _bw)
    _o += _bw
PROJ_TN = 1024
N_PROJ = -(-_o // PROJ_TN) * PROJ_TN


def _col_block(name):
    off, bw = _DST[name]
    return off // bw


def _softplus(x):
    return jnp.maximum(x, 0.0) + jnp.log1p(jnp.exp(-jnp.abs(x)))


def _silu(x):
    return x * jax.nn.sigmoid(x)


def _bdot(a, b):
    return jnp.dot(a.astype(BF16), b.astype(BF16), preferred_element_type=F32)


def _bdot_nt(a, b):
    return lax.dot_general(a.astype(BF16), b.astype(BF16), (((1,), (1,)), ((), ())),
                           preferred_element_type=F32)


def _bdot_tn(a, b):
    return lax.dot_general(a.astype(BF16), b.astype(BF16), (((0,), (0,)), ((), ())),
                           preferred_element_type=F32)


def _cumsum_rows(x):
    n = x.shape[0]
    row = lax.broadcasted_iota(jnp.int32, (n, n), 0)
    col = lax.broadcasted_iota(jnp.int32, (n, n), 1)
    tri = (col <= row).astype(BF16)
    hi = x.astype(BF16)
    rest = x - hi.astype(F32)
    mid = rest.astype(BF16)
    lo = (rest - mid.astype(F32)).astype(BF16)
    return (jnp.dot(tri, hi, preferred_element_type=F32) + jnp.dot(tri, mid, preferred_element_type=F32)
            + jnp.dot(tri, lo, preferred_element_type=F32))


def _compiler_params(sem):
    return pltpu.CompilerParams(dimension_semantics=sem, vmem_limit_bytes=VMEM_LIMIT)


def _norm_matmul_kernel(x_ref, g_ref, w_ref, o_ref, hn_ref):
    @pl.when(pl.program_id(1) == 0)
    def _():
        x = x_ref[...]
        ms = jnp.mean(x * x, axis=-1, keepdims=True)
        hn_ref[...] = (x * lax.rsqrt(ms + EPS) * g_ref[...]).astype(BF16)

    o_ref[...] = jnp.dot(hn_ref[...], w_ref[...], preferred_element_type=F32).astype(o_ref.dtype)


def _norm_matmul(x2, gain, w, layer, *, tm=2048, tn=PROJ_TN):
    t, d = x2.shape
    n = w.shape[2]
    return pl.pallas_call(
        _norm_matmul_kernel,
        grid=(t // tm, n // tn),
        in_specs=[pl.BlockSpec((tm, d), lambda i, j: (i, 0)),
                  pl.BlockSpec((1, d), lambda i, j: (0, 0)),
                  pl.BlockSpec((None, d, tn), lambda i, j: (layer, 0, j))],
        out_specs=pl.BlockSpec((tm, tn), lambda i, j: (i, j)),
        out_shape=jax.ShapeDtypeStruct((t, n), BF16),
        scratch_shapes=[pltpu.VMEM((tm, d), BF16)],
        compiler_params=_compiler_params(("parallel", "arbitrary")),
        name="norm_in_proj",
    )(x2, gain.reshape(1, d), w)


def _ssd_kernel(z_ref, xbc_ref, dt_ref, cw_ref, cb_ref, dtb_ref, alog_ref, dsk_ref, nw_ref, o_ref,
                cbuf, act, state, ybuf, *, blk):
    @pl.when(pl.program_id(1) == 0)
    def _():
        cbuf[0:SUBLANE, :] = jnp.zeros((SUBLANE, SSM_XBC), F32)
        state[...] = jnp.zeros_like(state)

    cbuf[SUBLANE:SUBLANE + blk, :] = xbc_ref[...].astype(F32)
    conv = cb_ref[...]
    for k in range(SSM_CONV):
        s = SSM_CONV - 1 - k
        conv = conv + cw_ref[k:k + 1, :] * cbuf[SUBLANE - s:SUBLANE - s + blk, :]
    act[...] = _silu(conv)
    cbuf[0:SUBLANE, :] = cbuf[blk:blk + SUBLANE, :]

    a_neg = -jnp.exp(alog_ref[...])
    row = lax.broadcasted_iota(jnp.int32, (CHUNK, LANE), 0)
    col = lax.broadcasted_iota(jnp.int32, (CHUNK, LANE), 1) % CHUNK
    causal = col <= row
    diag = col == row
    left = lax.broadcasted_iota(jnp.int32, (CHUNK, LANE), 1) < SSM_HEAD_DIM
    head_of = lax.broadcasted_iota(jnp.int32, (LANE, WIDTH), 1) // SSM_HEAD_DIM
    spread = (head_of == lax.broadcasted_iota(jnp.int32, (LANE, WIDTH), 0)).astype(BF16)
    npair = SSM_HEADS // 2
    pair_group = (SSM_HEADS // SSM_GROUPS) // 2

    def per_channel(x):
        hi = x.astype(BF16)
        lo = (x - hi.astype(F32)).astype(BF16)
        return (jnp.dot(hi, spread, preferred_element_type=F32)
                + jnp.dot(lo, spread, preferred_element_type=F32))

    def chunk(c, carry):
        r0 = pl.multiple_of(c * CHUNK, CHUNK)
        rows = pl.ds(r0, CHUNK)
        dt = _softplus(dt_ref[rows, :].astype(F32) + dtb_ref[...])
        a_cum = _cumsum_rows(dt * a_neg)
        full = per_channel(jnp.concatenate([dt, a_cum], axis=0))
        dt_c, ac_c = full[0:CHUNK, :], full[CHUNK:, :]
        a_last = ac_c[CHUNK - 1:CHUNK, :]
        xs = act[rows, 0:WIDTH]
        xdt = xs * dt_c
        xdt_end = xdt * jnp.exp(a_last - ac_c)
        from_start = jnp.exp(ac_c)
        chunk_decay = jnp.exp(a_last)
        bm = [act[rows, WIDTH + g * SSM_STATE:WIDTH + (g + 1) * SSM_STATE] for g in range(SSM_GROUPS)]
        cm = [act[rows, WIDTH + (SSM_GROUPS + g) * SSM_STATE:WIDTH + (SSM_GROUPS + g + 1) * SSM_STATE]
              for g in range(SSM_GROUPS)]
        cb2 = [_bdot_nt(cm[g], jnp.concatenate([bm[g], bm[g]], axis=0)) for g in range(SSM_GROUPS)]
        for p in range(npair):
            g = p // pair_group
            ps = slice(p * LANE, (p + 1) * LANE)
            seg_l = ac_c[:, ps]
            seg_s = jnp.sum(jnp.where(diag, seg_l, 0.0), axis=0, keepdims=True)
            m = cb2[g] * jnp.exp(jnp.where(causal, seg_l - seg_s, -jnp.inf))
            xd = xdt[:, ps]
            xd2 = jnp.concatenate([jnp.where(left, xd, 0.0), jnp.where(left, 0.0, xd)],
                                  axis=0)
            prev = state[p]
            y = _bdot(m, xd2) + _bdot(cm[g], prev) * from_start[:, ps] + xs[:, ps] * dsk_ref[:, ps]
            state[p] = prev * chunk_decay[:, ps] + _bdot_tn(bm[g], xdt_end[:, ps])
            ybuf[rows, ps] = y
        return carry

    lax.fori_loop(0, blk // CHUNK, chunk, 0)

    y = ybuf[...] * _silu(z_ref[...].astype(F32))
    ms = jnp.mean(y * y, axis=-1, keepdims=True)
    o_ref[...] = (y * lax.rsqrt(ms + EPS) * nw_ref[...]).astype(o_ref.dtype)


def _ssd_branch(proj, b, s, conv_w, conv_b, dt_bias, a_log, d_skip, norm_w, *, blk=256):
    nblk = s // blk
    pad = lambda v: jnp.pad(v.reshape(1, -1), ((0, 0), (0, LANE - v.shape[0])))
    dsk = jnp.repeat(d_skip, SSM_HEAD_DIM).reshape(1, WIDTH)
    row = lambda i, j: i * nblk + j
    const = lambda i, j: (0, 0)
    return pl.pallas_call(
        functools.partial(_ssd_kernel, blk=blk),
        grid=(b, nblk),
        in_specs=[pl.BlockSpec((blk, WIDTH), lambda i, j: (row(i, j), _col_block("z"))),
                  pl.BlockSpec((blk, SSM_XBC), lambda i, j: (row(i, j), _col_block("xbc"))),
                  pl.BlockSpec((blk, LANE), lambda i, j: (row(i, j), _col_block("dt"))),
                  pl.BlockSpec((SSM_CONV, SSM_XBC), const),
                  pl.BlockSpec((1, SSM_XBC), const),
                  pl.BlockSpec((1, LANE), const),
                  pl.BlockSpec((1, LANE), const),
                  pl.BlockSpec((1, WIDTH), const),
                  pl.BlockSpec((1, WIDTH), const)],
        out_specs=pl.BlockSpec((blk, WIDTH), lambda i, j: (row(i, j), 0)),
        out_shape=jax.ShapeDtypeStruct((b * s, WIDTH), BF16),
        scratch_shapes=[pltpu.VMEM((blk + 2 * SUBLANE, SSM_XBC), F32),
                        pltpu.VMEM((blk, SSM_XBC), F32),
                        pltpu.VMEM((SSM_HEADS // 2, SSM_STATE, LANE), F32),
                        pltpu.VMEM((blk, WIDTH), F32)],
        compiler_params=_compiler_params(("parallel", "arbitrary")),
        name="ssd_branch",
    )(proj, proj, proj, conv_w, conv_b.reshape(1, -1), pad(dt_bias), pad(a_log), dsk,
      norm_w.reshape(1, -1))


def _lru_kernel(gate_ref, x_ref, cw_ref, cb_ref, wa_ref, ba_ref, wi_ref, bi_ref, lam_ref, o_ref,
                cbuf, a_s, u_s, carry_s, *, blk):
    @pl.when(pl.program_id(1) == 0)
    def _():
        cbuf[0:SUBLANE, :] = jnp.zeros((SUBLANE, WIDTH), F32)
        carry_s[...] = jnp.zeros_like(carry_s)

    cbuf[SUBLANE:SUBLANE + blk, :] = x_ref[...].astype(F32)
    xc = cb_ref[...]
    for k in range(LRU_CONV):
        s = LRU_CONV - 1 - k
        xc = xc + cw_ref[k:k + 1, :] * cbuf[SUBLANE - s:SUBLANE - s + blk, :]
    cbuf[0:SUBLANE, :] = cbuf[blk:blk + SUBLANE, :]

    nsl = wa_ref.shape[0]
    wsl = WIDTH // nsl
    xcb = xc.astype(BF16)
    ra = jnp.concatenate([jnp.dot(xcb[:, q * wsl:(q + 1) * wsl], wa_ref[q], preferred_element_type=F32)
                          for q in range(nsl)], axis=1)
    ri = jnp.concatenate([jnp.dot(xcb[:, q * wsl:(q + 1) * wsl], wi_ref[q], preferred_element_type=F32)
                          for q in range(nsl)], axis=1)
    r = jax.nn.sigmoid(ra + ba_ref[...])
    i = jax.nn.sigmoid(ri + bi_ref[...])
    log_a = -LRU_C * r * _softplus(-lam_ref[...])
    a = jnp.exp(log_a)
    a_s[...] = a
    u_s[...] = jnp.sqrt(-jnp.tanh(log_a) * (a * a + 1.0)) * (i * xc)

    rows = lax.broadcasted_iota(jnp.int32, (SUBLANE, WIDTH), 0)

    def group(g, carry):
        r0 = pl.multiple_of(g * SUBLANE, SUBLANE)
        a = a_s[pl.ds(r0, SUBLANE), :]
        u = u_s[pl.ds(r0, SUBLANE), :]
        for k in (1, 2, 4):
            keep = rows >= k
            a_prev = jnp.where(keep, pltpu.roll(a, k, 0), 1.0)
            u_prev = jnp.where(keep, pltpu.roll(u, k, 0), 0.0)
            u = a * u_prev + u
            a = a * a_prev
        h = a * carry + u
        u_s[pl.ds(r0, SUBLANE), :] = h
        return jnp.broadcast_to(h[SUBLANE - 1:SUBLANE, :], (SUBLANE, WIDTH))

    carry_s[...] = lax.fori_loop(0, blk // SUBLANE, group, carry_s[...])
    o_ref[...] = (u_s[...] * jax.nn.gelu(gate_ref[...].astype(F32), approximate=True)).astype(o_ref.dtype)


def _block_diag_slabs(w, per_slab):
    g, n, _ = w.shape
    w = w.reshape(g // per_slab, per_slab, n, n)
    eye = jnp.eye(per_slab, dtype=w.dtype)
    out = jnp.einsum("spij,pq->spiqj", w, eye)
    return out.reshape(g // per_slab, per_slab * n, per_slab * n)


def _lru_branch(proj, b, s, conv_w, conv_b, w_a, b_a, w_i, b_i, lam, *, blk=256):
    nblk = s // blk
    per_slab = 4
    wa = _block_diag_slabs(w_a, per_slab).astype(BF16)
    wi = _block_diag_slabs(w_i, per_slab).astype(BF16)
    row = lambda i, j: i * nblk + j
    const = lambda i, j: (0, 0)
    const3 = lambda i, j: (0, 0, 0)
    vec = lambda v: v.reshape(1, WIDTH)
    return pl.pallas_call(
        functools.partial(_lru_kernel, blk=blk),
        grid=(b, nblk),
        in_specs=[pl.BlockSpec((blk, WIDTH), lambda i, j: (row(i, j), _col_block("lgate"))),
                  pl.BlockSpec((blk, WIDTH), lambda i, j: (row(i, j), _col_block("lx"))),
                  pl.BlockSpec((LRU_CONV, WIDTH), const),
                  pl.BlockSpec((1, WIDTH), const),
                  pl.BlockSpec(wa.shape, const3),
                  pl.BlockSpec((1, WIDTH), const),
                  pl.BlockSpec(wi.shape, const3),
                  pl.BlockSpec((1, WIDTH), const),
                  pl.BlockSpec((1, WIDTH), const)],
        out_specs=pl.BlockSpec((blk, WIDTH), lambda i, j: (row(i, j), 0)),
        out_shape=jax.ShapeDtypeStruct((b * s, WIDTH), BF16),
        scratch_shapes=[pltpu.VMEM((blk + 2 * SUBLANE, WIDTH), F32),
                        pltpu.VMEM((blk, WIDTH), F32),
                        pltpu.VMEM((blk, WIDTH), F32),
                        pltpu.VMEM((SUBLANE, WIDTH), F32)],
        compiler_params=_compiler_params(("parallel", "arbitrary")),
        name="rglru_branch",
    )(proj, proj, conv_w, vec(conv_b), wa, vec(b_a), wi, vec(b_i), vec(lam))


def _ret_kernel(q_ref, k_ref, v_ref, g_ref, cos_ref, sin_ref, inner_ref, kte_ref, qfs_ref, cdec_ref,
                nw_ref, o_ref, state, ybuf, *, blk):
    @pl.when(pl.program_id(1) == 0)
    def _():
        state[...] = jnp.zeros_like(state)

    lane = lax.broadcasted_iota(jnp.int32, (CHUNK, RET_QK), 1)
    first_half = (lane % RET_QK_DIM) < (RET_QK_DIM // 2)
    half = RET_QK_DIM // 2

    def rotary(x, cos, sin):
        swapped = jnp.where(first_half, pltpu.roll(x, RET_QK - half, 1), pltpu.roll(x, half, 1))
        return x * cos + swapped * sin

    left = lax.broadcasted_iota(jnp.int32, (CHUNK, LANE), 1) < RET_QK_DIM
    npair = RET_HEADS // 2

    def halves(x):
        return jnp.concatenate([jnp.where(left, x, 0.0), jnp.where(left, 0.0, x)], axis=0)

    def chunk(c, carry):
        r0 = pl.multiple_of(c * CHUNK, CHUNK)
        rows = pl.ds(r0, CHUNK)
        cos = cos_ref[rows, :]
        sin = sin_ref[rows, :]
        q = rotary(q_ref[rows, :].astype(F32), cos, sin)
        k = rotary(k_ref[rows, :].astype(F32), cos, sin) * (RET_QK_DIM ** -0.5)
        k_end = k * kte_ref[...]
        q_start = q * qfs_ref[...]
        slabs = [slice(p * LANE, (p + 1) * LANE) for p in range(npair)]
        v2 = [v_ref[rows, 2 * p * RET_V_DIM:(2 * p + 2) * RET_V_DIM] for p in range(npair)]
        v2 = [jnp.concatenate([v[:, 0:RET_V_DIM], v[:, RET_V_DIM:]], axis=0) for v in v2]
        prev = [state[p] for p in range(npair)]
        s2 = [_bdot_nt(q[:, ps], halves(k[:, ps])) * inner_ref[p] for p, ps in enumerate(slabs)]
        kv = [_bdot_tn(halves(k_end[:, ps]), v2[p]) for p, ps in enumerate(slabs)]
        for p, ps in enumerate(slabs):
            rhs = jnp.concatenate([v2[p], prev[p]], axis=0)
            for e in range(2):
                keep = left if e == 0 else jnp.logical_not(left)
                lhs = jnp.concatenate([jnp.where(keep, s2[p], 0.0), jnp.where(keep, q_start[:, ps], 0.0)],
                                      axis=1)
                y = _bdot(lhs, rhs)
                mu = jnp.mean(y, axis=-1, keepdims=True)
                yc = y - mu
                var = jnp.mean(yc * yc, axis=-1, keepdims=True)
                h = 2 * p + e
                ybuf[rows, h * RET_V_DIM:(h + 1) * RET_V_DIM] = yc * lax.rsqrt(var + EPS)
            state[p] = prev[p] * cdec_ref[p] + kv[p]
        return carry

    lax.fori_loop(0, blk // CHUNK, chunk, 0)
    o_ref[...] = (_silu(g_ref[...].astype(F32)) * (ybuf[...] * nw_ref[...])).astype(o_ref.dtype)


def _ret_tables(s):
    pos_s = jnp.arange(s, dtype=F32)
    inv_freq = ROPE_BASE ** (-jnp.arange(0, RET_QK_DIM, 2, dtype=F32) / RET_QK_DIM)
    ang = pos_s[:, None] * inv_freq[None, :]
    cos, sin = lax.optimization_barrier((jnp.cos(ang), jnp.sin(ang)))
    cos_f = jnp.tile(jnp.concatenate([cos, cos], axis=1), (1, RET_HEADS))
    sin_f = jnp.tile(jnp.concatenate([-sin, sin], axis=1), (1, RET_HEADS))
    log_gamma = jnp.log1p(-jnp.exp2(-5.0 - jnp.arange(RET_HEADS, dtype=F32)))
    pos = jnp.arange(CHUNK, dtype=F32)
    inner = jnp.exp(log_gamma[:, None, None] * jnp.abs(pos[:, None] - pos[None, :]))
    k_to_end = jnp.exp(log_gamma[:, None] * (CHUNK - 1.0 - pos))
    q_from_start = jnp.exp(log_gamma[:, None] * (pos + 1.0))
    kte = jnp.repeat(k_to_end.T, RET_QK_DIM, axis=1)
    qfs = jnp.repeat(q_from_start.T, RET_QK_DIM, axis=1)
    inner2 = inner.reshape(RET_HEADS // 2, 2, CHUNK, CHUNK).transpose(0, 2, 1, 3).reshape(
        RET_HEADS // 2, CHUNK, 2 * CHUNK)
    cdec = jnp.broadcast_to(jnp.repeat(jnp.exp(log_gamma * CHUNK), RET_QK_DIM).reshape(
        RET_HEADS // 2, 2 * RET_QK_DIM, 1), (RET_HEADS // 2, 2 * RET_QK_DIM, RET_V_DIM))
    return cos_f, sin_f, inner2, kte, qfs, cdec


def _ret_branch(proj, b, s, norm_w, tables, *, blk=256):
    nblk = s // blk
    cos_f, sin_f, inner, kte, qfs, cdec = tables
    row = lambda i, j: i * nblk + j
    const = lambda i, j: (0, 0)
    return pl.pallas_call(
        functools.partial(_ret_kernel, blk=blk),
        grid=(b, nblk),
        in_specs=[pl.BlockSpec((blk, RET_QK), lambda i, j: (row(i, j), _col_block("rq"))),
                  pl.BlockSpec((blk, RET_QK), lambda i, j: (row(i, j), _col_block("rk"))),
                  pl.BlockSpec((blk, WIDTH), lambda i, j: (row(i, j), _col_block("rv"))),
                  pl.BlockSpec((blk, WIDTH), lambda i, j: (row(i, j), _col_block("rg"))),
                  pl.BlockSpec((blk, RET_QK), lambda i, j: (j, 0)),
                  pl.BlockSpec((blk, RET_QK), lambda i, j: (j, 0)),
                  pl.BlockSpec(inner.shape, lambda i, j: (0, 0, 0)),
                  pl.BlockSpec((CHUNK, RET_QK), const),
                  pl.BlockSpec((CHUNK, RET_QK), const),
                  pl.BlockSpec(cdec.shape, lambda i, j: (0, 0, 0)),
                  pl.BlockSpec((1, WIDTH), const)],
        out_specs=pl.BlockSpec((blk, WIDTH), lambda i, j: (row(i, j), 0)),
        out_shape=jax.ShapeDtypeStruct((b * s, WIDTH), BF16),
        scratch_shapes=[pltpu.VMEM((RET_HEADS // 2, 2 * RET_QK_DIM, RET_V_DIM), F32),
                        pltpu.VMEM((blk, WIDTH), F32)],
        compiler_params=_compiler_params(("parallel", "arbitrary")),
        name="retention_branch",
    )(proj, proj, proj, proj, cos_f, sin_f, inner, kte, qfs, cdec, norm_w.reshape(1, -1))


def _rwkv_kernel(r_ref, k_ref, v_ref, l_ref, mur_ref, muk_ref, muv_ref, mul_ref, w0_ref, w2_ref,
                 a0_ref, a2_ref, g2_ref, kk_ref, ka_ref, rk_ref, lnw_ref, lnb_ref, o_ref,
                 tail, tail_l, r_s, k_s, v_s, kk_s, b_s, lw_s, o_s, ub_s, op_s, wall_s, state, *, blk, hw):
    nh = hw // RWKV_HEAD

    @pl.when(pl.program_id(2) == 0)
    def _():
        tail[...] = jnp.zeros_like(tail)
        tail_l[...] = jnp.zeros_like(tail_l)
        state[...] = jnp.zeros_like(state)

    rows = lax.broadcasted_iota(jnp.int32, (blk, 1), 0)

    def shift_mix(x, prev_row, mu):
        prev = jnp.where(rows == 0, prev_row, pltpu.roll(x, 1, 0))
        return x + (prev - x) * mu

    lane = lax.broadcasted_iota(jnp.int32, (LANE, LANE), 1) // RWKV_HEAD
    lrow = lax.broadcasted_iota(jnp.int32, (LANE, LANE), 0) // RWKV_HEAD
    head_ones = (lane == lrow).astype(BF16)

    def head_sum(x):
        hi = x.astype(BF16)
        lo = (x - hi.astype(F32)).astype(BF16)
        return jnp.concatenate(
            [jnp.dot(hi[:, q:q + LANE], head_ones, preferred_element_type=F32)
             + jnp.dot(lo[:, q:q + LANE], head_ones, preferred_element_type=F32)
             for q in range(0, hw, LANE)], axis=1)

    r_raw, k_raw, v_raw, l_raw = (ref[...].astype(F32) for ref in (r_ref, k_ref, v_ref, l_ref))
    r = shift_mix(r_raw, tail[0:1, :], mur_ref[...])
    k = shift_mix(k_raw, tail[1:2, :], muk_ref[...])
    v = shift_mix(v_raw, tail[2:3, :], muv_ref[...])
    lo_ = shift_mix(l_raw, tail_l[0:1, :], mul_ref[...])
    tail[0:1, :] = r_raw[blk - 1:blk, :]
    tail[1:2, :] = k_raw[blk - 1:blk, :]
    tail[2:3, :] = v_raw[blk - 1:blk, :]
    tail_l[0:1, :] = l_raw[blk - 1:blk, :]

    wl = lo_[:, 0:RWKV_W_LORA]
    al = lo_[:, RWKV_W_LORA:RWKV_W_LORA + RWKV_A_LORA]
    gl = lo_[:, RWKV_W_LORA + RWKV_A_LORA:]
    wz = w0_ref[...] + _bdot(jnp.tanh(wl), w2_ref[...])
    w = jnp.minimum(wz, 0.0) - jnp.log(1.0 + jnp.exp(-jnp.abs(wz))) - 0.5
    a = jax.nn.sigmoid(a0_ref[...] + _bdot(al, a2_ref[...]))
    gate = _bdot(jax.nn.sigmoid(gl), g2_ref[...])
    kk = k * kk_ref[...]
    kk = kk * lax.rsqrt(jnp.maximum(head_sum(kk * kk), 1e-24))
    k = k * (1.0 + (a - 1.0) * ka_ref[...])
    r_s[...] = r
    k_s[...] = k
    v_s[...] = v
    kk_s[...] = kk
    b_s[...] = kk * a
    lw_s[...] = -jnp.exp(w)

    trow = lax.broadcasted_iota(jnp.int32, (CHUNK, 2 * CHUNK), 0)
    tcol = lax.broadcasted_iota(jnp.int32, (CHUNK, 2 * CHUNK), 1) % CHUNK
    strict = tcol < trow
    incl = tcol <= trow
    left = lax.broadcasted_iota(jnp.int32, (CHUNK, LANE), 1) < RWKV_HEAD
    left2 = lax.broadcasted_iota(jnp.int32, (2 * CHUNK, LANE), 1) < RWKV_HEAD

    eye2 = (tcol == trow).astype(F32)
    zeros = jnp.zeros((CHUNK, LANE), F32)
    slabs = [slice(p * LANE, (p + 1) * LANE) for p in range(nh // 2)]
    lcat = lambda a, b: jnp.concatenate([a, b], axis=1)
    rcat = lambda *a: jnp.concatenate(a, axis=0)
    keep_l = lambda a: jnp.where(left, a, 0.0)
    keep_r = lambda a: jnp.where(left, 0.0, a)

    def phase_a(g, carry):
        cs = [g * A_UNROLL + i for i in range(A_UNROLL)]
        sls = [pl.ds(pl.multiple_of(c * CHUNK, CHUNK), CHUNK) for c in cs]
        pre = []
        for sl in sls:
            lw = lw_s[sl, :]
            cum = _cumsum_rows(lw)
            cum_last = cum[CHUNK - 1:CHUNK, :]
            w_inv = jnp.exp(-cum)
            to_end = jnp.exp(cum_last - cum)
            pre.append(dict(rt=r_s[sl, :] * jnp.exp(cum), kt=k_s[sl, :] * w_inv, bt=b_s[sl, :] * w_inv,
                            kap=kk_s[sl, :] * jnp.exp(cum - lw), k_end=k_s[sl, :] * to_end,
                            b_end=b_s[sl, :] * to_end, w_all=jnp.exp(cum_last), vv=v_s[sl, :]))
        items = [(i, ps) for i in range(A_UNROLL) for ps in slabs]
        pm2 = []
        for i, ps in items:
            d = pre[i]
            lhs = rcat(d["kap"][:, ps], d["rt"][:, ps])
            rhs_e = jnp.where(left2, rcat(d["bt"][:, ps], d["kt"][:, ps]), 0.0)
            rhs_o = jnp.where(left2, 0.0, rcat(d["kt"][:, ps], d["bt"][:, ps]))
            pm2.append(_bdot_nt(lhs, rcat(rhs_e, rhs_o)))
        top = [(jnp.where(strict, m[0:CHUNK, 0:LANE], 0.0), jnp.where(strict, m[0:CHUNK, LANE:], 0.0))
               for m in pm2]
        bot = [(jnp.where(incl, m[CHUNK:, 0:LANE], 0.0), jnp.where(incl, m[CHUNK:, LANE:], 0.0))
               for m in pm2]
        v_swp = [pltpu.roll(pre[i]["vv"][:, ps], RWKV_HEAD, 1) for i, ps in items]
        v_oe = [rcat(keep_l(v), keep_r(v)) for v in v_swp]
        av = [_bdot(jnp.where(left, to, te), v_oe[n]) for n, (te, to) in enumerate(top)]
        op = [_bdot(jnp.where(left, bo, be), v_oe[n]) for n, (be, bo) in enumerate(bot)]
        z = [(jnp.where(left, -te, eye2), jnp.where(left, eye2, -to)) for te, to in top]

        def step(ze, zo):
            return _bdot(jnp.where(left, ze, zo), rcat(lcat(ze, zeros), lcat(zeros, zo)))

        span = 1
        while 2 * span < CHUNK:
            res = [step(ze, zo) for ze, zo in z]
            z = [(r[:, 0:LANE] + keep_r(ze), r[:, LANE:] + keep_l(zo)) for r, (ze, zo) in zip(res, z)]
            span *= 2
        res = [step(ze, zo) for ze, zo in z]
        t_oe = [jnp.where(left, r[:, LANE:] + zo, r[:, 0:LANE] + ze) for r, (ze, zo) in zip(res, z)]
        for n, (i, ps) in enumerate(items):
            kap = pre[i]["kap"][:, ps]
            prod = _bdot(t_oe[n], lcat(rcat(keep_l(av[n]), keep_r(av[n])), rcat(keep_r(kap), keep_l(kap))))
            sl = sls[i]
            lw_s[sl, ps] = prod[:, 0:LANE]
            kk_s[sl, ps] = prod[:, LANE:]
            ub_s[sl, ps] = jnp.where(left, bot[n][0], bot[n][1])
            op_s[sl, ps] = op[n]
            v_s[sl, ps] = v_swp[n]
        for i, sl in enumerate(sls):
            r_s[sl, :] = pre[i]["rt"]
            k_s[sl, :] = pre[i]["k_end"]
            b_s[sl, :] = pre[i]["b_end"]
            wall_s[cs[i]] = jnp.broadcast_to(pre[i]["w_all"], (SUBLANE, hw))
        return carry

    lax.fori_loop(0, blk // CHUNK // A_UNROLL, phase_a, 0)

    def phase_b(c, carry):
        sl = pl.ds(pl.multiple_of(c * CHUNK, CHUNK), CHUNK)
        w_all = wall_s[c]
        s0 = [(state[2 * p], state[2 * p + 1]) for p in range(nh // 2)]
        s_oe = [rcat(so, se) for se, so in s0]
        u = [-(lw_s[sl, ps] + _bdot_nt(kk_s[sl, ps], s_oe[p])) for p, ps in enumerate(slabs)]
        op = [op_s[sl, ps] + _bdot_nt(r_s[sl, ps], s_oe[p]) for p, ps in enumerate(slabs)]
        for p, ps in enumerate(slabs):
            o_sw = _bdot(ub_s[sl, ps], rcat(keep_r(u[p]), keep_l(u[p]))) + op[p]
            o_s[sl, ps] = pltpu.roll(o_sw, RWKV_HEAD, 1)
        for p, ps in enumerate(slabs):
            add = _bdot_tn(rcat(u[p], v_s[sl, ps]), rcat(b_s[sl, ps], k_s[sl, ps]))
            state[2 * p] = s0[p][0] * w_all[0:1, ps] + keep_l(add[RWKV_HEAD:, :])
            state[2 * p + 1] = s0[p][1] * w_all[0:1, ps] + keep_r(add[0:RWKV_HEAD, :])
        return carry

    lax.fori_loop(0, blk // CHUNK, phase_b, 0)

    o = o_s[...]
    mu = head_sum(o) * (1.0 / RWKV_HEAD)
    oc = o - mu
    var = head_sum(oc * oc) * (1.0 / RWKV_HEAD)
    o = oc * lax.rsqrt(var + RWKV_LN_EPS) * lnw_ref[...] + lnb_ref[...]
    o = o + head_sum(r * k * rk_ref[...]) * v
    o_ref[...] = (o * gate).astype(o_ref.dtype)


def _rwkv_branch(proj, b, s, mu, w0, w2, a0, a2, g2, k_k, k_a, r_k, ln_w, ln_b, *, blk=256, hw=WIDTH):
    nblk = s // blk
    nslab = WIDTH // hw
    row = lambda i, p, j: i * nblk + j
    vec = lambda v: v.reshape(1, -1)
    hcol = lambda i, p, j: (0, p)
    mu_r, mu_k, mu_v, mu_l = (mu[0:WIDTH], mu[WIDTH:2 * WIDTH], mu[2 * WIDTH:3 * WIDTH],
                              mu[3 * WIDTH:])
    wblk = _col_block("wr")
    lblk = _col_block("lora")
    return pl.pallas_call(
        functools.partial(_rwkv_kernel, blk=blk, hw=hw),
        grid=(b, nslab, nblk),
        in_specs=[pl.BlockSpec((blk, hw), lambda i, p, j: (row(i, p, j), wblk * nslab + p)),
                  pl.BlockSpec((blk, hw), lambda i, p, j: (row(i, p, j), (wblk + 1) * nslab + p)),
                  pl.BlockSpec((blk, hw), lambda i, p, j: (row(i, p, j), (wblk + 2) * nslab + p)),
                  pl.BlockSpec((blk, RWKV_LORA), lambda i, p, j: (row(i, p, j), lblk)),
                  pl.BlockSpec((1, hw), hcol),
                  pl.BlockSpec((1, hw), hcol),
                  pl.BlockSpec((1, hw), hcol),
                  pl.BlockSpec((1, RWKV_LORA), lambda i, p, j: (0, 0)),
                  pl.BlockSpec((1, hw), hcol),
                  pl.BlockSpec((RWKV_W_LORA, hw), hcol),
                  pl.BlockSpec((1, hw), hcol),
                  pl.BlockSpec((RWKV_A_LORA, hw), hcol),
                  pl.BlockSpec((RWKV_G_LORA, hw), hcol),
                  pl.BlockSpec((1, hw), hcol),
                  pl.BlockSpec((1, hw), hcol),
                  pl.BlockSpec((1, hw), hcol),
                  pl.BlockSpec((1, hw), hcol),
                  pl.BlockSpec((1, hw), hcol)],
        out_specs=pl.BlockSpec((blk, hw), lambda i, p, j: (row(i, p, j), p)),
        out_shape=jax.ShapeDtypeStruct((b * s, WIDTH), BF16),
        scratch_shapes=[pltpu.VMEM((SUBLANE, hw), F32), pltpu.VMEM((SUBLANE, RWKV_LORA), F32)]
        + [pltpu.VMEM((blk, hw), F32)] * 9
        + [pltpu.VMEM((blk // CHUNK, SUBLANE, hw), F32)]
        + [pltpu.VMEM((hw // RWKV_HEAD, RWKV_HEAD, LANE), F32)],
        compiler_params=_compiler_params(("parallel", "parallel", "arbitrary")),
        name="rwkv7_branch",
    )(proj, proj, proj, proj, vec(mu_r), vec(mu_k), vec(mu_v), vec(mu_l), vec(w0), w2.astype(BF16),
      vec(a0), a2.astype(BF16), g2.astype(BF16), vec(k_k), vec(k_a), vec(r_k), vec(ln_w), vec(ln_b))


def _merge_kernel(y0_ref, y1_ref, y2_ref, y3_ref, gl_ref, x_ref, wb_ref, wo_ref, o_ref):
    merged = None
    for m, y_ref in enumerate((y0_ref, y1_ref, y2_ref, y3_ref)):
        br = jnp.dot(y_ref[...], wb_ref[m], preferred_element_type=F32)
        term = jax.nn.sigmoid(gl_ref[:, m * D_MODEL:(m + 1) * D_MODEL].astype(F32)) * br
        merged = term if merged is None else merged + term
    o_ref[...] = x_ref[...] + jnp.dot(merged.astype(BF16), wo_ref[...], preferred_element_type=F32)


def _merge(ys, proj, x2, w_branch, w_out, layer, *, tm=256):
    t = x2.shape[0]
    tok = lambda i: (i, 0)
    return pl.pallas_call(
        _merge_kernel,
        grid=(t // tm,),
        in_specs=[pl.BlockSpec((tm, WIDTH), tok)] * N_BRANCH
        + [pl.BlockSpec((tm, N_BRANCH * D_MODEL), lambda i: (i, _col_block("gates"))),
           pl.BlockSpec((tm, D_MODEL), tok),
           pl.BlockSpec((None, N_BRANCH, WIDTH, D_MODEL), lambda i: (layer, 0, 0, 0)),
           pl.BlockSpec((None, D_MODEL, D_MODEL), lambda i: (layer, 0, 0))],
        out_specs=pl.BlockSpec((tm, D_MODEL), tok),
        out_shape=jax.ShapeDtypeStruct((t, D_MODEL), F32),
        compiler_params=_compiler_params(("parallel",)),
        name="branch_merge",
    )(*ys, proj, x2, w_branch, w_out)


FFN_TC = 256
FFN_AHEAD = 3


def _ffn_kernel(x_ref, g_ref, wu_ref, cw_ref, cb_ref, wd_ref, gf_ref, o_ref, cbuf, tail, *, tm, final):
    @pl.when(pl.program_id(1) == 0)
    def _():
        tail[...] = jnp.zeros_like(tail)

    x = x_ref[...]
    ms = jnp.mean(x * x, axis=-1, keepdims=True)
    hn = (x * lax.rsqrt(ms + EPS) * g_ref[...]).astype(BF16)
    nsteps = D_FF // FFN_TC

    def up_conv(i):
        halves = []
        for half in range(2):
            c0 = half * D_FF + i * FFN_TC
            cs = slice(c0, c0 + FFN_TC)
            buf = cbuf.at[2 * (i % (FFN_AHEAD + 1)) + half]
            buf[0:SUBLANE, :] = tail[:, cs]
            buf[SUBLANE:SUBLANE + tm, :] = jnp.dot(hn, wu_ref[:, cs], preferred_element_type=F32)
            y = cb_ref[:, cs]
            for k in range(FFN_CONV):
                s = FFN_CONV - 1 - k
                y = y + cw_ref[k:k + 1, cs] * buf[SUBLANE - s:SUBLANE - s + tm, :]
            tail[:, cs] = buf[tm:tm + SUBLANE, :]
            halves.append(y)
        return halves

    acc = x
    ups = [up_conv(i) for i in range(FFN_AHEAD)]
    for i in range(nsteps):
        if i + FFN_AHEAD < nsteps:
            ups.append(up_conv(i + FFN_AHEAD))
        val, gt = ups[i]
        acc = acc + jnp.dot((_silu(gt) * val).astype(BF16), wd_ref[i * FFN_TC:(i + 1) * FFN_TC, :],
                            preferred_element_type=F32)
    if final:
        ms = jnp.mean(acc * acc, axis=-1, keepdims=True)
        acc = acc * lax.rsqrt(ms + EPS) * gf_ref[...]
    o_ref[...] = acc


def _conv_ffn(x2, b, s, gain, w_up, conv_w, conv_b, w_down, gain_final, layer, *, final, tm=512):
    nblk = s // tm
    row = lambda i, j: (i * nblk + j, 0)
    const = lambda i, j: (0, 0)
    return pl.pallas_call(
        functools.partial(_ffn_kernel, tm=tm, final=final),
        grid=(b, nblk),
        in_specs=[pl.BlockSpec((tm, D_MODEL), row),
                  pl.BlockSpec((1, D_MODEL), const),
                  pl.BlockSpec((None, D_MODEL, 2 * D_FF), lambda i, j: (layer, 0, 0)),
                  pl.BlockSpec((FFN_CONV, 2 * D_FF), const),
                  pl.BlockSpec((1, 2 * D_FF), const),
                  pl.BlockSpec((None, D_FF, D_MODEL), lambda i, j: (layer, 0, 0)),
                  pl.BlockSpec((1, D_MODEL), const)],
        out_specs=pl.BlockSpec((tm, D_MODEL), row),
        out_shape=jax.ShapeDtypeStruct((b * s, D_MODEL), F32),
        scratch_shapes=[pltpu.VMEM((2 * (FFN_AHEAD + 1), tm + 2 * SUBLANE, FFN_TC), F32),
                        pltpu.VMEM((SUBLANE, 2 * D_FF), F32)],
        compiler_params=_compiler_params(("parallel", "arbitrary")),
        name="conv_ffn",
    )(x2, gain.reshape(1, -1), w_up, conv_w, conv_b.reshape(1, -1), w_down, gain_final.reshape(1, -1))


def _reorder_w_in(w):
    cols = []
    for n in _DST_ORDER:
        off, wd = _SRC[n]
        cols.append(w[..., off:off + wd].astype(BF16))
    pad = N_PROJ - _DST["dt"][0] - SSM_HEADS
    cols.append(jnp.zeros(w.shape[:-1] + (pad,), BF16))
    return jnp.concatenate(cols, axis=-1)


def kernel(x, norm_mix, w_in, ssm_conv_w, ssm_conv_b, ssm_dt_bias, ssm_a_log, ssm_d, ssm_norm, lru_conv_w, lru_conv_b, lru_w_a, lru_b_a, lru_w_i, lru_b_i, lru_lam, ret_norm, rwkv_mu, rwkv_w0, rwkv_w2, rwkv_a0, rwkv_a2, rwkv_g2, rwkv_k_k, rwkv_k_a, rwkv_r_k, rwkv_ln_w, rwkv_ln_b, w_branch, w_out, norm_ffn, ffn_up, ffn_conv_w, ffn_conv_b, ffn_down, norm_final):
    b, s, d = x.shape
    x2 = x.reshape(b * s, d)
    tables = _ret_tables(s)
    w_in_b = _reorder_w_in(w_in)
    w_branch_b, w_out_b = w_branch.astype(BF16), w_out.astype(BF16)
    ffn_up_b, ffn_down_b = ffn_up.astype(BF16), ffn_down.astype(BF16)
    for l in range(DEPTH):
        proj = _norm_matmul(x2, norm_mix[l], w_in_b, l)
        y_ssd = _ssd_branch(proj, b, s, ssm_conv_w[l], ssm_conv_b[l], ssm_dt_bias[l], ssm_a_log[l],
                            ssm_d[l], ssm_norm[l])
        y_lru = _lru_branch(proj, b, s, lru_conv_w[l], lru_conv_b[l], lru_w_a[l], lru_b_a[l],
                            lru_w_i[l], lru_b_i[l], lru_lam[l])
        y_ret = _ret_branch(proj, b, s, ret_norm[l], tables)
        y_rwkv = _rwkv_branch(proj, b, s, rwkv_mu[l], rwkv_w0[l], rwkv_w2[l], rwkv_a0[l], rwkv_a2[l],
                              rwkv_g2[l], rwkv_k_k[l], rwkv_k_a[l], rwkv_r_k[l], rwkv_ln_w[l],
                              rwkv_ln_b[l])
        x2 = _merge((y_ssd, y_lru, y_ret, y_rwkv), proj, x2, w_branch_b, w_out_b, l)
        x2 = _conv_ffn(x2, b, s, norm_ffn[l], ffn_up_b, ffn_conv_w[l], ffn_conv_b[l], ffn_down_b,
                       norm_final, l, final=(l == DEPTH - 1))
    return x2.reshape(b, s, d)
```

```python
import functools

import jax
import jax.numpy as jnp
from jax import lax
from jax.experimental import pallas as pl
from jax.experimental.pallas import tpu as pltpu

F32 = jnp.float32
BF16 = jnp.bfloat16

D_MODEL = 1024
WIDTH = 1024
DEPTH = 2
CHUNK = 64
EPS = 1e-6

SSM_HEADS = 16
SSM_HEAD_DIM = 64
SSM_GROUPS = 4
SSM_STATE = 128
SSM_CONV = 4
SSM_XBC = WIDTH + 2 * SSM_GROUPS * SSM_STATE

LRU_BLOCKS = 16
LRU_BLOCK = 64
LRU_CONV = 4
LRU_C = 8.0

RET_HEADS = 8
RET_QK_DIM = 64
RET_V_DIM = 128
RET_QK = RET_HEADS * RET_QK_DIM
ROPE_BASE = 10000.0

RWKV_HEAD = 64
RWKV_HEADS = 16
RWKV_W_LORA = 64
RWKV_A_LORA = 64
RWKV_G_LORA = 128
RWKV_LORA = RWKV_W_LORA + RWKV_A_LORA + RWKV_G_LORA
RWKV_LN_EPS = 64e-5
A_UNROLL = 2

D_FF = 2816
FFN_CONV = 3
N_BRANCH = 4

LANE = 128
SUBLANE = 8
VMEM_LIMIT = 56 * 1024 * 1024

_SRC = {}
_o = 0
for _n, _w in (("z", WIDTH), ("xbc", SSM_XBC), ("dt", SSM_HEADS), ("lgate", WIDTH), ("lx", WIDTH),
               ("rq", RET_QK), ("rk", RET_QK), ("rv", WIDTH), ("rg", WIDTH),
               ("wr", WIDTH), ("wk", WIDTH), ("wv", WIDTH), ("lora", RWKV_LORA),
               ("gates", N_BRANCH * D_MODEL)):
    _SRC[_n] = (_o, _w)
    _o += _w
N_IN = _o
_DST_ORDER = ("gates", "xbc", "z", "lgate", "lx", "rv", "rg", "rq", "rk", "wr", "wk", "wv", "lora", "dt")
_DST = {}
_o = 0
for _n in _DST_ORDER:
    _w = _SRC[_n][1]
    _bw = max(_w, LANE)
    assert _o % _bw == 0, (_n, _o, _bw)
    _DST[_n] = (_o, _bw)
    _o += _bw
PROJ_TN = 1024
N_PROJ = -(-_o // PROJ_TN) * PROJ_TN


def _col_block(name):
    off, bw = _DST[name]
    return off // bw


def _softplus(x):
    return jnp.maximum(x, 0.0) + jnp.log1p(jnp.exp(-jnp.abs(x)))


def _silu(x):
    return x * jax.nn.sigmoid(x)


def _bdot(a, b):
    return jnp.dot(a.astype(BF16), b.astype(BF16), preferred_element_type=F32)


def _bdot_nt(a, b):
    return lax.dot_general(a.astype(BF16), b.astype(BF16), (((1,), (1,)), ((), ())),
                           preferred_element_type=F32)


def _bdot_tn(a, b):
    return lax.dot_general(a.astype(BF16), b.astype(BF16), (((0,), (0,)), ((), ())),
                           preferred_element_type=F32)


def _cumsum_rows(x):
    n = x.shape[0]
    row = lax.broadcasted_iota(jnp.int32, (n, n), 0)
    col = lax.broadcasted_iota(jnp.int32, (n, n), 1)
    tri = (col <= row).astype(BF16)
    hi = x.astype(BF16)
    rest = x - hi.astype(F32)
    mid = rest.astype(BF16)
    lo = (rest - mid.astype(F32)).astype(BF16)
    return (jnp.dot(tri, hi, preferred_element_type=F32) + jnp.dot(tri, mid, preferred_element_type=F32)
            + jnp.dot(tri, lo, preferred_element_type=F32))


CONV_HALO = CHUNK


def _shift_matrix(nshift):
    r = lax.broadcasted_iota(jnp.int32, (nshift * CHUNK, CONV_HALO + CHUNK), 0)
    c = lax.broadcasted_iota(jnp.int32, (nshift * CHUNK, CONV_HALO + CHUNK), 1)
    return (c == CONV_HALO + r % CHUNK - (r // CHUNK + 1)).astype(BF16)


def _causal_conv_chunk(xb, c, cw_ref, cb_ref, shifts):
    x2 = xb[c * CHUNK:c * CHUNK + CONV_HALO + CHUNK, :]
    sh = jnp.dot(shifts, x2, preferred_element_type=F32)
    k = cw_ref.shape[0]
    y = cb_ref[...] + cw_ref[k - 1:k, :] * x2[CONV_HALO:, :].astype(F32)
    for s in range(1, k):
        y = y + cw_ref[k - 1 - s:k - s, :] * sh[(s - 1) * CHUNK:s * CHUNK, :]
    return y


def _compiler_params(sem):
    return pltpu.CompilerParams(dimension_semantics=sem, vmem_limit_bytes=VMEM_LIMIT)


def _norm_matmul_kernel(x_ref, g_ref, w_ref, o_ref, hn_ref):
    @pl.when(pl.program_id(1) == 0)
    def _():
        x = x_ref[...]
        ms = jnp.mean(x * x, axis=-1, keepdims=True)
        hn_ref[...] = (x * lax.rsqrt(ms + EPS) * g_ref[...]).astype(BF16)

    o_ref[...] = jnp.dot(hn_ref[...], w_ref[...], preferred_element_type=F32).astype(o_ref.dtype)


def _norm_matmul(x2, gain, w, layer, *, tm=2048, tn=PROJ_TN):
    t, d = x2.shape
    n = w.shape[2]
    return pl.pallas_call(
        _norm_matmul_kernel,
        grid=(t // tm, n // tn),
        in_specs=[pl.BlockSpec((tm, d), lambda i, j: (i, 0)),
                  pl.BlockSpec((1, d), lambda i, j: (0, 0)),
                  pl.BlockSpec((None, d, tn), lambda i, j: (layer, 0, j))],
        out_specs=pl.BlockSpec((tm, tn), lambda i, j: (i, j)),
        out_shape=jax.ShapeDtypeStruct((t, n), BF16),
        scratch_shapes=[pltpu.VMEM((tm, d), BF16)],
        compiler_params=_compiler_params(("parallel", "arbitrary")),
        name="norm_in_proj",
    )(x2, gain.reshape(1, d), w)


def _ssd_kernel(z_ref, xbc_ref, dt_ref, cw_ref, cb_ref, dtb_ref, alog_ref, dsk_ref, nw_ref, o_ref,
                cbuf, state, *, blk):
    @pl.when(pl.program_id(1) == 0)
    def _():
        cbuf[0:CONV_HALO, :] = jnp.zeros((CONV_HALO, SSM_XBC), BF16)
        state[...] = jnp.zeros_like(state)

    cbuf[CONV_HALO:CONV_HALO + blk, :] = xbc_ref[...]
    shifts = _shift_matrix(SSM_CONV - 1)

    a_neg = -jnp.exp(alog_ref[...])
    row = lax.broadcasted_iota(jnp.int32, (CHUNK, LANE), 0)
    col = lax.broadcasted_iota(jnp.int32, (CHUNK, LANE), 1) % CHUNK
    causal = col <= row
    diag = col == row
    left = lax.broadcasted_iota(jnp.int32, (CHUNK, LANE), 1) < SSM_HEAD_DIM
    head_of = lax.broadcasted_iota(jnp.int32, (LANE, WIDTH), 1) // SSM_HEAD_DIM
    spread = (head_of == lax.broadcasted_iota(jnp.int32, (LANE, WIDTH), 0)).astype(BF16)
    npair = SSM_HEADS // 2
    pair_group = (SSM_HEADS // SSM_GROUPS) // 2
    slabs = [slice(p * LANE, (p + 1) * LANE) for p in range(npair)]

    def per_channel(x):
        hi = x.astype(BF16)
        lo = (x - hi.astype(F32)).astype(BF16)
        return (jnp.dot(hi, spread, preferred_element_type=F32)
                + jnp.dot(lo, spread, preferred_element_type=F32))

    def local(c):
        r0 = c * CHUNK
        act = _silu(_causal_conv_chunk(cbuf, c, cw_ref, cb_ref, shifts))
        dt = _softplus(dt_ref[r0:r0 + CHUNK, :].astype(F32) + dtb_ref[...])
        a_cum = _cumsum_rows(dt * a_neg)
        full = per_channel(jnp.concatenate([dt, a_cum], axis=0))
        dt_c, ac_c = full[0:CHUNK, :], full[CHUNK:, :]
        a_last = ac_c[CHUNK - 1:CHUNK, :]
        xs = act[:, 0:WIDTH]
        xdt = xs * dt_c
        xdt_end = xdt * jnp.exp(a_last - ac_c)
        bm = [act[:, WIDTH + g * SSM_STATE:WIDTH + (g + 1) * SSM_STATE] for g in range(SSM_GROUPS)]
        cm = [act[:, WIDTH + (SSM_GROUPS + g) * SSM_STATE:WIDTH + (SSM_GROUPS + g + 1) * SSM_STATE]
              for g in range(SSM_GROUPS)]
        cb2 = [_bdot_nt(cm[g], jnp.concatenate([bm[g], bm[g]], axis=0)) for g in range(SSM_GROUPS)]
        y_diag, s_new = [], []
        for p, ps in enumerate(slabs):
            g = p // pair_group
            seg_l = ac_c[:, ps]
            seg_s = jnp.sum(jnp.where(diag, seg_l, 0.0), axis=0, keepdims=True)
            m = cb2[g] * jnp.exp(jnp.where(causal, seg_l - seg_s, -jnp.inf))
            xd = xdt[:, ps]
            xd2 = jnp.concatenate([jnp.where(left, xd, 0.0), jnp.where(left, 0.0, xd)],
                                  axis=0)
            y_diag.append(_bdot(m, xd2) + xs[:, ps] * dsk_ref[:, ps])
            s_new.append(_bdot_tn(bm[g], xdt_end[:, ps]))
        return dict(y=y_diag, s=s_new, cm=cm, from_start=jnp.exp(ac_c), decay=jnp.exp(a_last))

    loc = [local(c) for c in range(blk // CHUNK)]
    cbuf[0:CONV_HALO, :] = cbuf[blk:blk + CONV_HALO, :]
    for c, d in enumerate(loc):
        ys = []
        for p, ps in enumerate(slabs):
            prev = state[p]
            ys.append(d["y"][p] + _bdot(d["cm"][p // pair_group], prev) * d["from_start"][:, ps])
            state[p] = prev * d["decay"][:, ps] + d["s"][p]
        y = jnp.concatenate(ys, axis=1) * _silu(z_ref[c * CHUNK:(c + 1) * CHUNK, :].astype(F32))
        ms = jnp.mean(y * y, axis=-1, keepdims=True)
        o_ref[c * CHUNK:(c + 1) * CHUNK, :] = (y * lax.rsqrt(ms + EPS) * nw_ref[...]).astype(o_ref.dtype)


def _ssd_branch(proj, b, s, conv_w, conv_b, dt_bias, a_log, d_skip, norm_w, *, blk=256):
    nblk = s // blk
    pad = lambda v: jnp.pad(v.reshape(1, -1), ((0, 0), (0, LANE - v.shape[0])))
    dsk = jnp.repeat(d_skip, SSM_HEAD_DIM).reshape(1, WIDTH)
    row = lambda i, j: i * nblk + j
    const = lambda i, j: (0, 0)
    return pl.pallas_call(
        functools.partial(_ssd_kernel, blk=blk),
        grid=(b, nblk),
        in_specs=[pl.BlockSpec((blk, WIDTH), lambda i, j: (row(i, j), _col_block("z"))),
                  pl.BlockSpec((blk, SSM_XBC), lambda i, j: (row(i, j), _col_block("xbc"))),
                  pl.BlockSpec((blk, LANE), lambda i, j: (row(i, j), _col_block("dt"))),
                  pl.BlockSpec((SSM_CONV, SSM_XBC), const),
                  pl.BlockSpec((1, SSM_XBC), const),
                  pl.BlockSpec((1, LANE), const),
                  pl.BlockSpec((1, LANE), const),
                  pl.BlockSpec((1, WIDTH), const),
                  pl.BlockSpec((1, WIDTH), const)],
        out_specs=pl.BlockSpec((blk, WIDTH), lambda i, j: (row(i, j), 0)),
        out_shape=jax.ShapeDtypeStruct((b * s, WIDTH), BF16),
        scratch_shapes=[pltpu.VMEM((CONV_HALO + blk, SSM_XBC), BF16),
                        pltpu.VMEM((SSM_HEADS // 2, SSM_STATE, LANE), F32)],
        compiler_params=_compiler_params(("parallel", "arbitrary")),
        name="ssd_branch",
    )(proj, proj, proj, conv_w, conv_b.reshape(1, -1), pad(dt_bias), pad(a_log), dsk,
      norm_w.reshape(1, -1))


def _lru_kernel(gate_ref, x_ref, cw_ref, cb_ref, wa_ref, ba_ref, wi_ref, bi_ref, lam_ref, o_ref,
                cbuf, a_s, u_s, carry_s, *, blk):
    @pl.when(pl.program_id(1) == 0)
    def _():
        cbuf[0:CONV_HALO, :] = jnp.zeros((CONV_HALO, WIDTH), BF16)
        carry_s[...] = jnp.zeros_like(carry_s)

    cbuf[CONV_HALO:CONV_HALO + blk, :] = x_ref[...]
    shifts = _shift_matrix(LRU_CONV - 1)
    xc = jnp.concatenate([_causal_conv_chunk(cbuf, c, cw_ref, cb_ref, shifts) for c in range(blk // CHUNK)],
                         axis=0)
    cbuf[0:CONV_HALO, :] = cbuf[blk:blk + CONV_HALO, :]

    nsl = wa_ref.shape[0]
    wsl = WIDTH // nsl
    xcb = xc.astype(BF16)
    ra = jnp.concatenate([jnp.dot(xcb[:, q * wsl:(q + 1) * wsl], wa_ref[q], preferred_element_type=F32)
                          for q in range(nsl)], axis=1)
    ri = jnp.concatenate([jnp.dot(xcb[:, q * wsl:(q + 1) * wsl], wi_ref[q], preferred_element_type=F32)
                          for q in range(nsl)], axis=1)
    r = jax.nn.sigmoid(ra + ba_ref[...])
    i = jax.nn.sigmoid(ri + bi_ref[...])
    log_a = -LRU_C * r * _softplus(-lam_ref[...])
    a = jnp.exp(log_a)
    a_s[...] = a
    u_s[...] = jnp.sqrt(-jnp.tanh(log_a) * (a * a + 1.0)) * (i * xc)

    rows = lax.broadcasted_iota(jnp.int32, (SUBLANE, WIDTH), 0)

    def group(g, carry):
        r0 = pl.multiple_of(g * SUBLANE, SUBLANE)
        a = a_s[pl.ds(r0, SUBLANE), :]
        u = u_s[pl.ds(r0, SUBLANE), :]
        for k in (1, 2, 4):
            keep = rows >= k
            a_prev = jnp.where(keep, pltpu.roll(a, k, 0), 1.0)
            u_prev = jnp.where(keep, pltpu.roll(u, k, 0), 0.0)
            u = a * u_prev + u
            a = a * a_prev
        h = a * carry + u
        u_s[pl.ds(r0, SUBLANE), :] = h
        return jnp.broadcast_to(h[SUBLANE - 1:SUBLANE, :], (SUBLANE, WIDTH))

    carry_s[...] = lax.fori_loop(0, blk // SUBLANE, group, carry_s[...])
    o_ref[...] = (u_s[...] * jax.nn.gelu(gate_ref[...].astype(F32), approximate=True)).astype(o_ref.dtype)


def _block_diag_slabs(w, per_slab):
    g, n, _ = w.shape
    w = w.reshape(g // per_slab, per_slab, n, n)
    eye = jnp.eye(per_slab, dtype=w.dtype)
    out = jnp.einsum("spij,pq->spiqj", w, eye)
    return out.reshape(g // per_slab, per_slab * n, per_slab * n)


def _lru_branch(proj, b, s, conv_w, conv_b, w_a, b_a, w_i, b_i, lam, *, blk=256):
    nblk = s // blk
    per_slab = 4
    wa = _block_diag_slabs(w_a, per_slab).astype(BF16)
    wi = _block_diag_slabs(w_i, per_slab).astype(BF16)
    row = lambda i, j: i * nblk + j
    const = lambda i, j: (0, 0)
    const3 = lambda i, j: (0, 0, 0)
    vec = lambda v: v.reshape(1, WIDTH)
    return pl.pallas_call(
        functools.partial(_lru_kernel, blk=blk),
        grid=(b, nblk),
        in_specs=[pl.BlockSpec((blk, WIDTH), lambda i, j: (row(i, j), _col_block("lgate"))),
                  pl.BlockSpec((blk, WIDTH), lambda i, j: (row(i, j), _col_block("lx"))),
                  pl.BlockSpec((LRU_CONV, WIDTH), const),
                  pl.BlockSpec((1, WIDTH), const),
                  pl.BlockSpec(wa.shape, const3),
                  pl.BlockSpec((1, WIDTH), const),
                  pl.BlockSpec(wi.shape, const3),
                  pl.BlockSpec((1, WIDTH), const),
                  pl.BlockSpec((1, WIDTH), const)],
        out_specs=pl.BlockSpec((blk, WIDTH), lambda i, j: (row(i, j), 0)),
        out_shape=jax.ShapeDtypeStruct((b * s, WIDTH), BF16),
        scratch_shapes=[pltpu.VMEM((CONV_HALO + blk, WIDTH), BF16),
                        pltpu.VMEM((blk, WIDTH), F32),
                        pltpu.VMEM((blk, WIDTH), F32),
                        pltpu.VMEM((SUBLANE, WIDTH), F32)],
        compiler_params=_compiler_params(("parallel", "arbitrary")),
        name="rglru_branch",
    )(proj, proj, conv_w, vec(conv_b), wa, vec(b_a), wi, vec(b_i), vec(lam))


def _ret_kernel(q_ref, k_ref, v_ref, g_ref, cos_ref, sin_ref, inner_ref, kte_ref, qfs_ref, cdec_ref,
                nw_ref, o_ref, state, ybuf, *, blk):
    @pl.when(pl.program_id(1) == 0)
    def _():
        state[...] = jnp.zeros_like(state)

    lane = lax.broadcasted_iota(jnp.int32, (CHUNK, RET_QK), 1)
    first_half = (lane % RET_QK_DIM) < (RET_QK_DIM // 2)
    half = RET_QK_DIM // 2

    def rotary(x, cos, sin):
        swapped = jnp.where(first_half, pltpu.roll(x, RET_QK - half, 1), pltpu.roll(x, half, 1))
        return x * cos + swapped * sin

    left = lax.broadcasted_iota(jnp.int32, (CHUNK, LANE), 1) < RET_QK_DIM
    npair = RET_HEADS // 2

    def halves(x):
        return jnp.concatenate([jnp.where(left, x, 0.0), jnp.where(left, 0.0, x)], axis=0)

    def chunk(c, carry):
        r0 = pl.multiple_of(c * CHUNK, CHUNK)
        rows = pl.ds(r0, CHUNK)
        cos = cos_ref[rows, :]
        sin = sin_ref[rows, :]
        q = rotary(q_ref[rows, :].astype(F32), cos, sin)
        k = rotary(k_ref[rows, :].astype(F32), cos, sin) * (RET_QK_DIM ** -0.5)
        k_end = k * kte_ref[...]
        q_start = q * qfs_ref[...]
        slabs = [slice(p * LANE, (p + 1) * LANE) for p in range(npair)]
        v2 = [v_ref[rows, 2 * p * RET_V_DIM:(2 * p + 2) * RET_V_DIM] for p in range(npair)]
        v2 = [jnp.concatenate([v[:, 0:RET_V_DIM], v[:, RET_V_DIM:]], axis=0) for v in v2]
        prev = [state[p] for p in range(npair)]
        s2 = [_bdot_nt(q[:, ps], halves(k[:, ps])) * inner_ref[p] for p, ps in enumerate(slabs)]
        kv = [_bdot_tn(halves(k_end[:, ps]), v2[p]) for p, ps in enumerate(slabs)]
        for p, ps in enumerate(slabs):
            rhs = jnp.concatenate([v2[p], prev[p]], axis=0)
            for e in range(2):
                keep = left if e == 0 else jnp.logical_not(left)
                lhs = jnp.concatenate([jnp.where(keep, s2[p], 0.0), jnp.where(keep, q_start[:, ps], 0.0)],
                                      axis=1)
                y = _bdot(lhs, rhs)
                mu = jnp.mean(y, axis=-1, keepdims=True)
                yc = y - mu
                var = jnp.mean(yc * yc, axis=-1, keepdims=True)
                h = 2 * p + e
                ybuf[rows, h * RET_V_DIM:(h + 1) * RET_V_DIM] = yc * lax.rsqrt(var + EPS)
            state[p] = prev[p] * cdec_ref[p] + kv[p]
        return carry

    lax.fori_loop(0, blk // CHUNK, chunk, 0)
    o_ref[...] = (_silu(g_ref[...].astype(F32)) * (ybuf[...] * nw_ref[...])).astype(o_ref.dtype)


def _ret_tables(s):
    pos_s = jnp.arange(s, dtype=F32)
    inv_freq = ROPE_BASE ** (-jnp.arange(0, RET_QK_DIM, 2, dtype=F32) / RET_QK_DIM)
    ang = pos_s[:, None] * inv_freq[None, :]
    cos, sin = lax.optimization_barrier((jnp.cos(ang), jnp.sin(ang)))
    cos_f = jnp.tile(jnp.concatenate([cos, cos], axis=1), (1, RET_HEADS))
    sin_f = jnp.tile(jnp.concatenate([-sin, sin], axis=1), (1, RET_HEADS))
    log_gamma = jnp.log1p(-jnp.exp2(-5.0 - jnp.arange(RET_HEADS, dtype=F32)))
    pos = jnp.arange(CHUNK, dtype=F32)
    inner = jnp.exp(log_gamma[:, None, None] * jnp.abs(pos[:, None] - pos[None, :]))
    k_to_end = jnp.exp(log_gamma[:, None] * (CHUNK - 1.0 - pos))
    q_from_start = jnp.exp(log_gamma[:, None] * (pos + 1.0))
    kte = jnp.repeat(k_to_end.T, RET_QK_DIM, axis=1)
    qfs = jnp.repeat(q_from_start.T, RET_QK_DIM, axis=1)
    inner2 = inner.reshape(RET_HEADS // 2, 2, CHUNK, CHUNK).transpose(0, 2, 1, 3).reshape(
        RET_HEADS // 2, CHUNK, 2 * CHUNK)
    cdec = jnp.broadcast_to(jnp.repeat(jnp.exp(log_gamma * CHUNK), RET_QK_DIM).reshape(
        RET_HEADS // 2, 2 * RET_QK_DIM, 1), (RET_HEADS // 2, 2 * RET_QK_DIM, RET_V_DIM))
    return cos_f, sin_f, inner2, kte, qfs, cdec


def _ret_branch(proj, b, s, norm_w, tables, *, blk=256):
    nblk = s // blk
    cos_f, sin_f, inner, kte, qfs, cdec = tables
    row = lambda i, j: i * nblk + j
    const = lambda i, j: (0, 0)
    return pl.pallas_call(
        functools.partial(_ret_kernel, blk=blk),
        grid=(b, nblk),
        in_specs=[pl.BlockSpec((blk, RET_QK), lambda i, j: (row(i, j), _col_block("rq"))),
                  pl.BlockSpec((blk, RET_QK), lambda i, j: (row(i, j), _col_block("rk"))),
                  pl.BlockSpec((blk, WIDTH), lambda i, j: (row(i, j), _col_block("rv"))),
                  pl.BlockSpec((blk, WIDTH), lambda i, j: (row(i, j), _col_block("rg"))),
                  pl.BlockSpec((blk, RET_QK), lambda i, j: (j, 0)),
                  pl.BlockSpec((blk, RET_QK), lambda i, j: (j, 0)),
                  pl.BlockSpec(inner.shape, lambda i, j: (0, 0, 0)),
                  pl.BlockSpec((CHUNK, RET_QK), const),
                  pl.BlockSpec((CHUNK, RET_QK), const),
                  pl.BlockSpec(cdec.shape, lambda i, j: (0, 0, 0)),
                  pl.BlockSpec((1, WIDTH), const)],
        out_specs=pl.BlockSpec((blk, WIDTH), lambda i, j: (row(i, j), 0)),
        out_shape=jax.ShapeDtypeStruct((b * s, WIDTH), BF16),
        scratch_shapes=[pltpu.VMEM((RET_HEADS // 2, 2 * RET_QK_DIM, RET_V_DIM), F32),
                        pltpu.VMEM((blk, WIDTH), F32)],
        compiler_params=_compiler_params(("parallel", "arbitrary")),
        name="retention_branch",
    )(proj, proj, proj, proj, cos_f, sin_f, inner, kte, qfs, cdec, norm_w.reshape(1, -1))


def _rwkv_kernel(r_ref, k_ref, v_ref, l_ref, mur_ref, muk_ref, muv_ref, mul_ref, w0_ref, w2_ref,
                 a0_ref, a2_ref, g2_ref, kk_ref, ka_ref, rk_ref, lnw_ref, lnb_ref, o_ref,
                 tail, tail_l, r_s, k_s, v_s, kk_s, b_s, lw_s, o_s, ub_s, op_s, gate_s, bon_s, wall_s, state,
                 *, blk, hw):
    nh = hw // RWKV_HEAD

    @pl.when(pl.program_id(2) == 0)
    def _():
        tail[...] = jnp.zeros_like(tail)
        tail_l[...] = jnp.zeros_like(tail_l)
        state[...] = jnp.zeros_like(state)

    group = A_UNROLL * CHUNK
    ngroups = blk // group
    rows = lax.broadcasted_iota(jnp.int32, (group, 1), 0)

    def shift_mix(x, prev_row, mu):
        prev = jnp.where(rows == 0, prev_row, pltpu.roll(x, 1, 0))
        return x + (prev - x) * mu

    lane = lax.broadcasted_iota(jnp.int32, (LANE, LANE), 1) // RWKV_HEAD
    lrow = lax.broadcasted_iota(jnp.int32, (LANE, LANE), 0) // RWKV_HEAD
    head_ones = (lane == lrow).astype(BF16)

    def head_sum(x):
        hi = x.astype(BF16)
        lo = (x - hi.astype(F32)).astype(BF16)
        return jnp.concatenate(
            [jnp.dot(hi[:, q:q + LANE], head_ones, preferred_element_type=F32)
             + jnp.dot(lo[:, q:q + LANE], head_ones, preferred_element_type=F32)
             for q in range(0, hw, LANE)], axis=1)

    def prep(gi, last):
        rs = slice(gi * group, (gi + 1) * group)
        r_raw, k_raw, v_raw, l_raw = (ref[rs, :].astype(F32) for ref in (r_ref, k_ref, v_ref, l_ref))
        r = shift_mix(r_raw, last[0], mur_ref[...])
        k = shift_mix(k_raw, last[1], muk_ref[...])
        v = shift_mix(v_raw, last[2], muv_ref[...])
        lo_ = shift_mix(l_raw, last[3], mul_ref[...])
        wl = lo_[:, 0:RWKV_W_LORA]
        al = lo_[:, RWKV_W_LORA:RWKV_W_LORA + RWKV_A_LORA]
        gl = lo_[:, RWKV_W_LORA + RWKV_A_LORA:]
        wz = w0_ref[...] + _bdot(jnp.tanh(wl), w2_ref[...])
        w = jnp.minimum(wz, 0.0) - jnp.log(1.0 + jnp.exp(-jnp.abs(wz))) - 0.5
        a = jax.nn.sigmoid(a0_ref[...] + _bdot(al, a2_ref[...]))
        gate_s[rs, :] = _bdot(jax.nn.sigmoid(gl), g2_ref[...])
        kk = k * kk_ref[...]
        kk = kk * lax.rsqrt(jnp.maximum(head_sum(kk * kk), 1e-24))
        k = k * (1.0 + (a - 1.0) * ka_ref[...])
        bon_s[rs, :] = head_sum(r * k * rk_ref[...]) * v
        r_s[rs, :] = r
        k_s[rs, :] = k
        v_s[rs, :] = v
        kk_s[rs, :] = kk
        b_s[rs, :] = kk * a
        lw_s[rs, :] = -jnp.exp(w)
        return [x[group - 1:group, :] for x in (r_raw, k_raw, v_raw, l_raw)]

    last = [tail[0:1, :], tail[1:2, :], tail[2:3, :], tail_l[0:1, :]]
    for gi in range(ngroups):
        last = prep(gi, last)
    tail[0:1, :], tail[1:2, :], tail[2:3, :], tail_l[0:1, :] = last

    trow = lax.broadcasted_iota(jnp.int32, (CHUNK, 2 * CHUNK), 0)
    tcol = lax.broadcasted_iota(jnp.int32, (CHUNK, 2 * CHUNK), 1) % CHUNK
    strict = tcol < trow
    incl = tcol <= trow
    left = lax.broadcasted_iota(jnp.int32, (CHUNK, LANE), 1) < RWKV_HEAD
    left2 = lax.broadcasted_iota(jnp.int32, (2 * CHUNK, LANE), 1) < RWKV_HEAD

    eye2 = (tcol == trow).astype(F32)
    zeros = jnp.zeros((CHUNK, LANE), F32)
    slabs = [slice(p * LANE, (p + 1) * LANE) for p in range(nh // 2)]
    lcat = lambda a, b: jnp.concatenate([a, b], axis=1)
    rcat = lambda *a: jnp.concatenate(a, axis=0)
    keep_l = lambda a: jnp.where(left, a, 0.0)
    keep_r = lambda a: jnp.where(left, 0.0, a)

    def phase_a(g):
        cs = [g * A_UNROLL + i for i in range(A_UNROLL)]
        sls = [slice(c * CHUNK, (c + 1) * CHUNK) for c in cs]
        pre = []
        for sl in sls:
            lw = lw_s[sl, :]
            cum = _cumsum_rows(lw)
            cum_last = cum[CHUNK - 1:CHUNK, :]
            w_inv = jnp.exp(-cum)
            to_end = jnp.exp(cum_last - cum)
            pre.append(dict(rt=r_s[sl, :] * jnp.exp(cum), kt=k_s[sl, :] * w_inv, bt=b_s[sl, :] * w_inv,
                            kap=kk_s[sl, :] * jnp.exp(cum - lw), k_end=k_s[sl, :] * to_end,
                            b_end=b_s[sl, :] * to_end, w_all=jnp.exp(cum_last), vv=v_s[sl, :]))
        items = [(i, ps) for i in range(A_UNROLL) for ps in slabs]
        pm2 = []
        for i, ps in items:
            d = pre[i]
            lhs = rcat(d["kap"][:, ps], d["rt"][:, ps])
            rhs_e = jnp.where(left2, rcat(d["bt"][:, ps], d["kt"][:, ps]), 0.0)
            rhs_o = jnp.where(left2, 0.0, rcat(d["kt"][:, ps], d["bt"][:, ps]))
            pm2.append(_bdot_nt(lhs, rcat(rhs_e, rhs_o)))
        top = [(jnp.where(strict, m[0:CHUNK, 0:LANE], 0.0), jnp.where(strict, m[0:CHUNK, LANE:], 0.0))
               for m in pm2]
        bot = [(jnp.where(incl, m[CHUNK:, 0:LANE], 0.0), jnp.where(incl, m[CHUNK:, LANE:], 0.0))
               for m in pm2]
        v_swp = [pltpu.roll(pre[i]["vv"][:, ps], RWKV_HEAD, 1) for i, ps in items]
        v_oe = [rcat(keep_l(v), keep_r(v)) for v in v_swp]
        av = [_bdot(jnp.where(left, to, te), v_oe[n]) for n, (te, to) in enumerate(top)]
        op = [_bdot(jnp.where(left, bo, be), v_oe[n]) for n, (be, bo) in enumerate(bot)]
        z = [(jnp.where(left, -te, eye2), jnp.where(left, eye2, -to)) for te, to in top]

        def step(ze, zo):
            return _bdot(jnp.where(left, ze, zo), rcat(lcat(ze, zeros), lcat(zeros, zo)))

        span = 1
        while 2 * span < CHUNK:
            res = [step(ze, zo) for ze, zo in z]
            z = [(r[:, 0:LANE] + keep_r(ze), r[:, LANE:] + keep_l(zo)) for r, (ze, zo) in zip(res, z)]
            span *= 2
        res = [step(ze, zo) for ze, zo in z]
        t_oe = [jnp.where(left, r[:, LANE:] + zo, r[:, 0:LANE] + ze) for r, (ze, zo) in zip(res, z)]
        for n, (i, ps) in enumerate(items):
            kap = pre[i]["kap"][:, ps]
            prod = _bdot(t_oe[n], lcat(rcat(keep_l(av[n]), keep_r(av[n])), rcat(keep_r(kap), keep_l(kap))))
            sl = sls[i]
            lw_s[sl, ps] = prod[:, 0:LANE]
            kk_s[sl, ps] = prod[:, LANE:]
            ub_s[sl, ps] = jnp.where(left, bot[n][0], bot[n][1])
            op_s[sl, ps] = op[n]
            v_s[sl, ps] = v_swp[n]
        for i, sl in enumerate(sls):
            r_s[sl, :] = pre[i]["rt"]
            k_s[sl, :] = pre[i]["k_end"]
            b_s[sl, :] = pre[i]["b_end"]
            wall_s[cs[i]] = jnp.broadcast_to(pre[i]["w_all"], (SUBLANE, hw))

    def phase_b(c):
        sl = slice(c * CHUNK, (c + 1) * CHUNK)
        w_all = wall_s[c]
        s0 = [(state[2 * p], state[2 * p + 1]) for p in range(nh // 2)]
        s_oe = [rcat(so, se) for se, so in s0]
        u = [-(lw_s[sl, ps] + _bdot_nt(kk_s[sl, ps], s_oe[p])) for p, ps in enumerate(slabs)]
        op = [op_s[sl, ps] + _bdot_nt(r_s[sl, ps], s_oe[p]) for p, ps in enumerate(slabs)]
        for p, ps in enumerate(slabs):
            o_sw = _bdot(ub_s[sl, ps], rcat(keep_r(u[p]), keep_l(u[p]))) + op[p]
            o_s[sl, ps] = pltpu.roll(o_sw, RWKV_HEAD, 1)
        for p, ps in enumerate(slabs):
            add = _bdot_tn(rcat(u[p], v_s[sl, ps]), rcat(b_s[sl, ps], k_s[sl, ps]))
            state[2 * p] = s0[p][0] * w_all[0:1, ps] + keep_l(add[RWKV_HEAD:, :])
            state[2 * p + 1] = s0[p][1] * w_all[0:1, ps] + keep_r(add[0:RWKV_HEAD, :])

    def finish(gi):
        rs = slice(gi * group, (gi + 1) * group)
        o = o_s[rs, :]
        mu = head_sum(o) * (1.0 / RWKV_HEAD)
        oc = o - mu
        var = head_sum(oc * oc) * (1.0 / RWKV_HEAD)
        o = oc * lax.rsqrt(var + RWKV_LN_EPS) * lnw_ref[...] + lnb_ref[...]
        o_ref[rs, :] = ((o + bon_s[rs, :]) * gate_s[rs, :]).astype(o_ref.dtype)

    for g in range(ngroups):
        phase_a(g)
    for g in range(ngroups):
        for c in range(g * A_UNROLL, (g + 1) * A_UNROLL):
            phase_b(c)
        finish(g)


def _rwkv_branch(proj, b, s, mu, w0, w2, a0, a2, g2, k_k, k_a, r_k, ln_w, ln_b, *, blk=256, hw=WIDTH):
    nblk = s // blk
    nslab = WIDTH // hw
    row = lambda i, p, j: i * nblk + j
    vec = lambda v: v.reshape(1, -1)
    hcol = lambda i, p, j: (0, p)
    mu_r, mu_k, mu_v, mu_l = (mu[0:WIDTH], mu[WIDTH:2 * WIDTH], mu[2 * WIDTH:3 * WIDTH],
                              mu[3 * WIDTH:])
    wblk = _col_block("wr")
    lblk = _col_block("lora")
    return pl.pallas_call(
        functools.partial(_rwkv_kernel, blk=blk, hw=hw),
        grid=(b, nslab, nblk),
        in_specs=[pl.BlockSpec((blk, hw), lambda i, p, j: (row(i, p, j), wblk * nslab + p)),
                  pl.BlockSpec((blk, hw), lambda i, p, j: (row(i, p, j), (wblk + 1) * nslab + p)),
                  pl.BlockSpec((blk, hw), lambda i, p, j: (row(i, p, j), (wblk + 2) * nslab + p)),
                  pl.BlockSpec((blk, RWKV_LORA), lambda i, p, j: (row(i, p, j), lblk)),
                  pl.BlockSpec((1, hw), hcol),
                  pl.BlockSpec((1, hw), hcol),
                  pl.BlockSpec((1, hw), hcol),
                  pl.BlockSpec((1, RWKV_LORA), lambda i, p, j: (0, 0)),
                  pl.BlockSpec((1, hw), hcol),
                  pl.BlockSpec((RWKV_W_LORA, hw), hcol),
                  pl.BlockSpec((1, hw), hcol),
                  pl.BlockSpec((RWKV_A_LORA, hw), hcol),
                  pl.BlockSpec((RWKV_G_LORA, hw), hcol),
                  pl.BlockSpec((1, hw), hcol),
                  pl.BlockSpec((1, hw), hcol),
                  pl.BlockSpec((1, hw), hcol),
                  pl.BlockSpec((1, hw), hcol),
                  pl.BlockSpec((1, hw), hcol)],
        out_specs=pl.BlockSpec((blk, hw), lambda i, p, j: (row(i, p, j), p)),
        out_shape=jax.ShapeDtypeStruct((b * s, WIDTH), BF16),
        scratch_shapes=[pltpu.VMEM((SUBLANE, hw), F32), pltpu.VMEM((SUBLANE, RWKV_LORA), F32)]
        + [pltpu.VMEM((blk, hw), F32)] * 11
        + [pltpu.VMEM((blk // CHUNK, SUBLANE, hw), F32)]
        + [pltpu.VMEM((hw // RWKV_HEAD, RWKV_HEAD, LANE), F32)],
        compiler_params=_compiler_params(("parallel", "parallel", "arbitrary")),
        name="rwkv7_branch",
    )(proj, proj, proj, proj, vec(mu_r), vec(mu_k), vec(mu_v), vec(mu_l), vec(w0), w2.astype(BF16),
      vec(a0), a2.astype(BF16), g2.astype(BF16), vec(k_k), vec(k_a), vec(r_k), vec(ln_w), vec(ln_b))


def _merge_kernel(y0_ref, y1_ref, y2_ref, y3_ref, gl_ref, x_ref, wb_ref, wo_ref, o_ref):
    merged = None
    for m, y_ref in enumerate((y0_ref, y1_ref, y2_ref, y3_ref)):
        br = jnp.dot(y_ref[...], wb_ref[m], preferred_element_type=F32)
        term = jax.nn.sigmoid(gl_ref[:, m * D_MODEL:(m + 1) * D_MODEL].astype(F32)) * br
        merged = term if merged is None else merged + term
    o_ref[...] = x_ref[...] + jnp.dot(merged.astype(BF16), wo_ref[...], preferred_element_type=F32)


def _merge(ys, proj, x2, w_branch, w_out, layer, *, tm=256):
    t = x2.shape[0]
    tok = lambda i: (i, 0)
    return pl.pallas_call(
        _merge_kernel,
        grid=(t // tm,),
        in_specs=[pl.BlockSpec((tm, WIDTH), tok)] * N_BRANCH
        + [pl.BlockSpec((tm, N_BRANCH * D_MODEL), lambda i: (i, _col_block("gates"))),
           pl.BlockSpec((tm, D_MODEL), tok),
           pl.BlockSpec((None, N_BRANCH, WIDTH, D_MODEL), lambda i: (layer, 0, 0, 0)),
           pl.BlockSpec((None, D_MODEL, D_MODEL), lambda i: (layer, 0, 0))],
        out_specs=pl.BlockSpec((tm, D_MODEL), tok),
        out_shape=jax.ShapeDtypeStruct((t, D_MODEL), F32),
        compiler_params=_compiler_params(("parallel",)),
        name="branch_merge",
    )(*ys, proj, x2, w_branch, w_out)


FFN_TC = 256
FFN_AHEAD = 3


def _ffn_kernel(x_ref, g_ref, wu_ref, cw_ref, cb_ref, wd_ref, gf_ref, o_ref, cbuf, tail, *, tm, final):
    @pl.when(pl.program_id(1) == 0)
    def _():
        tail[...] = jnp.zeros_like(tail)

    x = x_ref[...]
    ms = jnp.mean(x * x, axis=-1, keepdims=True)
    hn = (x * lax.rsqrt(ms + EPS) * g_ref[...]).astype(BF16)
    nsteps = D_FF // FFN_TC

    def up_conv(i):
        halves = []
        for half in range(2):
            c0 = half * D_FF + i * FFN_TC
            cs = slice(c0, c0 + FFN_TC)
            buf = cbuf.at[2 * (i % (FFN_AHEAD + 1)) + half]
            buf[0:SUBLANE, :] = tail[:, cs]
            buf[SUBLANE:SUBLANE + tm, :] = jnp.dot(hn, wu_ref[:, cs], preferred_element_type=F32)
            y = cb_ref[:, cs]
            for k in range(FFN_CONV):
                s = FFN_CONV - 1 - k
                y = y + cw_ref[k:k + 1, cs] * buf[SUBLANE - s:SUBLANE - s + tm, :]
            tail[:, cs] = buf[tm:tm + SUBLANE, :]
            halves.append(y)
        return halves

    acc = x
    ups = [up_conv(i) for i in range(FFN_AHEAD)]
    for i in range(nsteps):
        if i + FFN_AHEAD < nsteps:
            ups.append(up_conv(i + FFN_AHEAD))
        val, gt = ups[i]
        acc = acc + jnp.dot((_silu(gt) * val).astype(BF16), wd_ref[i * FFN_TC:(i + 1) * FFN_TC, :],
                            preferred_element_type=F32)
    if final:
        ms = jnp.mean(acc * acc, axis=-1, keepdims=True)
        acc = acc * lax.rsqrt(ms + EPS) * gf_ref[...]
    o_ref[...] = acc


def _conv_ffn(x2, b, s, gain, w_up, conv_w, conv_b, w_down, gain_final, layer, *, final, tm=512):
    nblk = s // tm
    row = lambda i, j: (i * nblk + j, 0)
    const = lambda i, j: (0, 0)
    return pl.pallas_call(
        functools.partial(_ffn_kernel, tm=tm, final=final),
        grid=(b, nblk),
        in_specs=[pl.BlockSpec((tm, D_MODEL), row),
                  pl.BlockSpec((1, D_MODEL), const),
                  pl.BlockSpec((None, D_MODEL, 2 * D_FF), lambda i, j: (layer, 0, 0)),
                  pl.BlockSpec((FFN_CONV, 2 * D_FF), const),
                  pl.BlockSpec((1, 2 * D_FF), const),
                  pl.BlockSpec((None, D_FF, D_MODEL), lambda i, j: (layer, 0, 0)),
                  pl.BlockSpec((1, D_MODEL), const)],
        out_specs=pl.BlockSpec((tm, D_MODEL), row),
        out_shape=jax.ShapeDtypeStruct((b * s, D_MODEL), F32),
        scratch_shapes=[pltpu.VMEM((2 * (FFN_AHEAD + 1), tm + 2 * SUBLANE, FFN_TC), F32),
                        pltpu.VMEM((SUBLANE, 2 * D_FF), F32)],
        compiler_params=_compiler_params(("parallel", "arbitrary")),
        name="conv_ffn",
    )(x2, gain.reshape(1, -1), w_up, conv_w, conv_b.reshape(1, -1), w_down, gain_final.reshape(1, -1))


def _reorder_w_in(w):
    cols = []
    for n in _DST_ORDER:
        off, wd = _SRC[n]
        cols.append(w[..., off:off + wd].astype(BF16))
    pad = N_PROJ - _DST["dt"][0] - SSM_HEADS
    cols.append(jnp.zeros(w.shape[:-1] + (pad,), BF16))
    return jnp.concatenate(cols, axis=-1)


def kernel(x, norm_mix, w_in, ssm_conv_w, ssm_conv_b, ssm_dt_bias, ssm_a_log, ssm_d, ssm_norm, lru_conv_w, lru_conv_b, lru_w_a, lru_b_a, lru_w_i, lru_b_i, lru_lam, ret_norm, rwkv_mu, rwkv_w0, rwkv_w2, rwkv_a0, rwkv_a2, rwkv_g2, rwkv_k_k, rwkv_k_a, rwkv_r_k, rwkv_ln_w, rwkv_ln_b, w_branch, w_out, norm_ffn, ffn_up, ffn_conv_w, ffn_conv_b, ffn_down, norm_final):
    b, s, d = x.shape
    x2 = x.reshape(b * s, d)
    tables = _ret_tables(s)
    w_branch_b, w_out_b = w_branch.astype(BF16), w_out.astype(BF16)
    ffn_up_b, ffn_down_b = ffn_up.astype(BF16), ffn_down.astype(BF16)
    for l in range(DEPTH):
        proj = _norm_matmul(x2, norm_mix[l], _reorder_w_in(w_in[l])[None], 0)
        y_ssd = _ssd_branch(proj, b, s, ssm_conv_w[l], ssm_conv_b[l], ssm_dt_bias[l], ssm_a_log[l],
                            ssm_d[l], ssm_norm[l])
        y_lru = _lru_branch(proj, b, s, lru_conv_w[l], lru_conv_b[l], lru_w_a[l], lru_b_a[l],
                            lru_w_i[l], lru_b_i[l], lru_lam[l])
        y_ret = _ret_branch(proj, b, s, ret_norm[l], tables)
        y_rwkv = _rwkv_branch(proj, b, s, rwkv_mu[l], rwkv_w0[l], rwkv_w2[l], rwkv_a0[l], rwkv_a2[l],
                              rwkv_g2[l], rwkv_k_k[l], rwkv_k_a[l], rwkv_r_k[l], rwkv_ln_w[l],
                              rwkv_ln_b[l])
        x2 = _merge((y_ssd, y_lru, y_ret, y_rwkv), proj, x2, w_branch_b, w_out_b, l)
        x2 = _conv_ffn(x2, b, s, norm_ffn[l], ffn_up_b, ffn_conv_w[l], ffn_conv_b[l], ffn_down_b,
                       norm_final, l, final=(l == DEPTH - 1))
    return x2.reshape(b, s, d)
```

```python
import functools

import jax
import jax.numpy as jnp
from jax import lax
from jax.experimental import pallas as pl
from jax.experimental.pallas import tpu as pltpu

F32 = jnp.float32
BF16 = jnp.bfloat16

D_MODEL = 1024
WIDTH = 1024
DEPTH = 2
CHUNK = 64
EPS = 1e-6

SSM_HEADS = 16
SSM_HEAD_DIM = 64
SSM_GROUPS = 4
SSM_STATE = 128
SSM_CONV = 4
SSM_XBC = WIDTH + 2 * SSM_GROUPS * SSM_STATE

LRU_BLOCKS = 16
LRU_BLOCK = 64
LRU_CONV = 4
LRU_C = 8.0

RET_HEADS = 8
RET_QK_DIM = 64
RET_V_DIM = 128
RET_QK = RET_HEADS * RET_QK_DIM
ROPE_BASE = 10000.0

RWKV_HEAD = 64
RWKV_HEADS = 16
RWKV_W_LORA = 64
RWKV_A_LORA = 64
RWKV_G_LORA = 128
RWKV_LORA = RWKV_W_LORA + RWKV_A_LORA + RWKV_G_LORA
RWKV_LN_EPS = 64e-5
A_UNROLL = 2

D_FF = 2816
FFN_CONV = 3
N_BRANCH = 4

LANE = 128
SUBLANE = 8
VMEM_LIMIT = 56 * 1024 * 1024

_SRC = {}
_o = 0
for _n, _w in (("z", WIDTH), ("xbc", SSM_XBC), ("dt", SSM_HEADS), ("lgate", WIDTH), ("lx", WIDTH),
               ("rq", RET_QK), ("rk", RET_QK), ("rv", WIDTH), ("rg", WIDTH),
               ("wr", WIDTH), ("wk", WIDTH), ("wv", WIDTH), ("lora", RWKV_LORA),
               ("gates", N_BRANCH * D_MODEL)):
    _SRC[_n] = (_o, _w)
    _o += _w
N_IN = _o
_DST_ORDER = ("gates", "xbc", "z", "lgate", "lx", "rv", "rg", "rq", "rk", "wr", "wk", "wv", "lora", "dt")
_DST = {}
_o = 0
for _n in _DST_ORDER:
    _w = _SRC[_n][1]
    _bw = max(_w, LANE)
    assert _o % _bw == 0, (_n, _o, _bw)
    _DST[_n] = (_o, _bw)
    _o += _bw
PROJ_TN = 1024
N_PROJ = -(-_o // PROJ_TN) * PROJ_TN


def _col_block(name):
    off, bw = _DST[name]
    return off // bw


def _softplus(x):
    return jnp.maximum(x, 0.0) + jnp.log1p(jnp.exp(-jnp.abs(x)))


def _silu(x):
    return x * jax.nn.sigmoid(x)


def _bdot(a, b):
    return jnp.dot(a.astype(BF16), b.astype(BF16), preferred_element_type=F32)


def _bdot_nt(a, b):
    return lax.dot_general(a.astype(BF16), b.astype(BF16), (((1,), (1,)), ((), ())),
                           preferred_element_type=F32)


def _bdot_tn(a, b):
    return lax.dot_general(a.astype(BF16), b.astype(BF16), (((0,), (0,)), ((), ())),
                           preferred_element_type=F32)


def _cumsum_rows(x):
    n = x.shape[0]
    row = lax.broadcasted_iota(jnp.int32, (n, n), 0)
    col = lax.broadcasted_iota(jnp.int32, (n, n), 1)
    tri = (col <= row).astype(BF16)
    hi = x.astype(BF16)
    rest = x - hi.astype(F32)
    mid = rest.astype(BF16)
    lo = (rest - mid.astype(F32)).astype(BF16)
    return (jnp.dot(tri, hi, preferred_element_type=F32) + jnp.dot(tri, mid, preferred_element_type=F32)
            + jnp.dot(tri, lo, preferred_element_type=F32))


CONV_HALO = CHUNK


def _shift_matrix(nshift):
    r = lax.broadcasted_iota(jnp.int32, (nshift * CHUNK, CONV_HALO + CHUNK), 0)
    c = lax.broadcasted_iota(jnp.int32, (nshift * CHUNK, CONV_HALO + CHUNK), 1)
    return (c == CONV_HALO + r % CHUNK - (r // CHUNK + 1)).astype(BF16)


def _causal_conv_chunk(xb, c, cw_ref, cb_ref, shifts):
    x2 = xb[c * CHUNK:c * CHUNK + CONV_HALO + CHUNK, :]
    sh = jnp.dot(shifts, x2, preferred_element_type=F32)
    k = cw_ref.shape[0]
    y = cb_ref[...] + cw_ref[k - 1:k, :] * x2[CONV_HALO:, :].astype(F32)
    for s in range(1, k):
        y = y + cw_ref[k - 1 - s:k - s, :] * sh[(s - 1) * CHUNK:s * CHUNK, :]
    return y


def _compiler_params(sem):
    return pltpu.CompilerParams(dimension_semantics=sem, vmem_limit_bytes=VMEM_LIMIT)


CAST_BLOCK_BYTES = 4 * 1024 * 1024


def _cast_kernel(x_ref, o_ref):
    o_ref[...] = x_ref[...].astype(o_ref.dtype)


def _cast_bf16(x):
    x2 = x.reshape(-1, x.shape[-1])
    r, c = x2.shape
    rows = max(n for n in range(16, r + 1, 16) if r % n == 0 and n * c * 4 <= CAST_BLOCK_BYTES)
    out = pl.pallas_call(
        _cast_kernel,
        grid=(r // rows,),
        in_specs=[pl.BlockSpec((rows, c), lambda i: (i, 0))],
        out_specs=pl.BlockSpec((rows, c), lambda i: (i, 0)),
        out_shape=jax.ShapeDtypeStruct((r, c), BF16),
        compiler_params=_compiler_params(("parallel",)),
        name="cast_bf16",
    )(x2)
    return out.reshape(x.shape)


def _norm_matmul_kernel(x_ref, g_ref, w_ref, o_ref, hn_ref):
    @pl.when(pl.program_id(1) == 0)
    def _():
        x = x_ref[...]
        ms = jnp.mean(x * x, axis=-1, keepdims=True)
        hn_ref[...] = (x * lax.rsqrt(ms + EPS) * g_ref[...]).astype(BF16)

    o_ref[...] = jnp.dot(hn_ref[...], w_ref[...], preferred_element_type=F32).astype(o_ref.dtype)


def _norm_matmul(x2, gain, w, layer, *, tm=2048, tn=PROJ_TN):
    t, d = x2.shape
    n = w.shape[2]
    return pl.pallas_call(
        _norm_matmul_kernel,
        grid=(t // tm, n // tn),
        in_specs=[pl.BlockSpec((tm, d), lambda i, j: (i, 0)),
                  pl.BlockSpec((1, d), lambda i, j: (0, 0)),
                  pl.BlockSpec((None, d, tn), lambda i, j: (layer, 0, j))],
        out_specs=pl.BlockSpec((tm, tn), lambda i, j: (i, j)),
        out_shape=jax.ShapeDtypeStruct((t, n), BF16),
        scratch_shapes=[pltpu.VMEM((tm, d), BF16)],
        compiler_params=_compiler_params(("parallel", "arbitrary")),
        name="norm_in_proj",
    )(x2, gain.reshape(1, d), w)


def _ssd_kernel(z_ref, xbc_ref, dt_ref, cw_ref, cb_ref, dtb_ref, alog_ref, dsk_ref, nw_ref, o_ref,
                cbuf, state, *, blk):
    @pl.when(pl.program_id(1) == 0)
    def _():
        cbuf[0:CONV_HALO, :] = jnp.zeros((CONV_HALO, SSM_XBC), BF16)
        state[...] = jnp.zeros_like(state)

    cbuf[CONV_HALO:CONV_HALO + blk, :] = xbc_ref[...]
    shifts = _shift_matrix(SSM_CONV - 1)

    a_neg = -jnp.exp(alog_ref[...])
    row = lax.broadcasted_iota(jnp.int32, (CHUNK, LANE), 0)
    col = lax.broadcasted_iota(jnp.int32, (CHUNK, LANE), 1) % CHUNK
    causal = col <= row
    diag = col == row
    left = lax.broadcasted_iota(jnp.int32, (CHUNK, LANE), 1) < SSM_HEAD_DIM
    head_of = lax.broadcasted_iota(jnp.int32, (LANE, WIDTH), 1) // SSM_HEAD_DIM
    spread = (head_of == lax.broadcasted_iota(jnp.int32, (LANE, WIDTH), 0)).astype(BF16)
    npair = SSM_HEADS // 2
    pair_group = (SSM_HEADS // SSM_GROUPS) // 2
    slabs = [slice(p * LANE, (p + 1) * LANE) for p in range(npair)]

    def per_channel(x):
        hi = x.astype(BF16)
        lo = (x - hi.astype(F32)).astype(BF16)
        return (jnp.dot(hi, spread, preferred_element_type=F32)
                + jnp.dot(lo, spread, preferred_element_type=F32))

    def local(c):
        r0 = c * CHUNK
        act = _silu(_causal_conv_chunk(cbuf, c, cw_ref, cb_ref, shifts))
        dt = _softplus(dt_ref[r0:r0 + CHUNK, :].astype(F32) + dtb_ref[...])
        a_cum = _cumsum_rows(dt * a_neg)
        full = per_channel(jnp.concatenate([dt, a_cum], axis=0))
        dt_c, ac_c = full[0:CHUNK, :], full[CHUNK:, :]
        a_last = ac_c[CHUNK - 1:CHUNK, :]
        xs = act[:, 0:WIDTH]
        xdt = xs * dt_c
        xdt_end = xdt * jnp.exp(a_last - ac_c)
        bm = [act[:, WIDTH + g * SSM_STATE:WIDTH + (g + 1) * SSM_STATE] for g in range(SSM_GROUPS)]
        cm = [act[:, WIDTH + (SSM_GROUPS + g) * SSM_STATE:WIDTH + (SSM_GROUPS + g + 1) * SSM_STATE]
              for g in range(SSM_GROUPS)]
        cb2 = [_bdot_nt(cm[g], jnp.concatenate([bm[g], bm[g]], axis=0)) for g in range(SSM_GROUPS)]
        y_diag, s_new = [], []
        for p, ps in enumerate(slabs):
            g = p // pair_group
            seg_l = ac_c[:, ps]
            seg_s = jnp.sum(jnp.where(diag, seg_l, 0.0), axis=0, keepdims=True)
            m = cb2[g] * jnp.exp(jnp.where(causal, seg_l - seg_s, -jnp.inf))
            xd = xdt[:, ps]
            xd2 = jnp.concatenate([jnp.where(left, xd, 0.0), jnp.where(left, 0.0, xd)],
                                  axis=0)
            y_diag.append(_bdot(m, xd2) + xs[:, ps] * dsk_ref[:, ps])
            s_new.append(_bdot_tn(bm[g], xdt_end[:, ps]))
        return dict(y=y_diag, s=s_new, cm=cm, from_start=jnp.exp(ac_c), decay=jnp.exp(a_last))

    loc = [local(c) for c in range(blk // CHUNK)]
    cbuf[0:CONV_HALO, :] = cbuf[blk:blk + CONV_HALO, :]
    for c, d in enumerate(loc):
        ys = []
        for p, ps in enumerate(slabs):
            prev = state[p]
            ys.append(d["y"][p] + _bdot(d["cm"][p // pair_group], prev) * d["from_start"][:, ps])
            state[p] = prev * d["decay"][:, ps] + d["s"][p]
        y = jnp.concatenate(ys, axis=1) * _silu(z_ref[c * CHUNK:(c + 1) * CHUNK, :].astype(F32))
        ms = jnp.mean(y * y, axis=-1, keepdims=True)
        o_ref[c * CHUNK:(c + 1) * CHUNK, :] = (y * lax.rsqrt(ms + EPS) * nw_ref[...]).astype(o_ref.dtype)


def _ssd_branch(proj, b, s, conv_w, conv_b, dt_bias, a_log, d_skip, norm_w, *, blk=256):
    nblk = s // blk
    pad = lambda v: jnp.pad(v.reshape(1, -1), ((0, 0), (0, LANE - v.shape[0])))
    dsk = jnp.repeat(d_skip, SSM_HEAD_DIM).reshape(1, WIDTH)
    row = lambda i, j: i * nblk + j
    const = lambda i, j: (0, 0)
    return pl.pallas_call(
        functools.partial(_ssd_kernel, blk=blk),
        grid=(b, nblk),
        in_specs=[pl.BlockSpec((blk, WIDTH), lambda i, j: (row(i, j), _col_block("z"))),
                  pl.BlockSpec((blk, SSM_XBC), lambda i, j: (row(i, j), _col_block("xbc"))),
                  pl.BlockSpec((blk, LANE), lambda i, j: (row(i, j), _col_block("dt"))),
                  pl.BlockSpec((SSM_CONV, SSM_XBC), const),
                  pl.BlockSpec((1, SSM_XBC), const),
                  pl.BlockSpec((1, LANE), const),
                  pl.BlockSpec((1, LANE), const),
                  pl.BlockSpec((1, WIDTH), const),
                  pl.BlockSpec((1, WIDTH), const)],
        out_specs=pl.BlockSpec((blk, WIDTH), lambda i, j: (row(i, j), 0)),
        out_shape=jax.ShapeDtypeStruct((b * s, WIDTH), BF16),
        scratch_shapes=[pltpu.VMEM((CONV_HALO + blk, SSM_XBC), BF16),
                        pltpu.VMEM((SSM_HEADS // 2, SSM_STATE, LANE), F32)],
        compiler_params=_compiler_params(("parallel", "arbitrary")),
        name="ssd_branch",
    )(proj, proj, proj, conv_w, conv_b.reshape(1, -1), pad(dt_bias), pad(a_log), dsk,
      norm_w.reshape(1, -1))


def _lru_kernel(gate_ref, x_ref, cw_ref, cb_ref, wa_ref, ba_ref, wi_ref, bi_ref, lam_ref, o_ref,
                cbuf, a_s, u_s, carry_s, *, blk):
    @pl.when(pl.program_id(1) == 0)
    def _():
        cbuf[0:CONV_HALO, :] = jnp.zeros((CONV_HALO, WIDTH), BF16)
        carry_s[...] = jnp.zeros_like(carry_s)

    cbuf[CONV_HALO:CONV_HALO + blk, :] = x_ref[...]
    shifts = _shift_matrix(LRU_CONV - 1)
    xc = jnp.concatenate([_causal_conv_chunk(cbuf, c, cw_ref, cb_ref, shifts) for c in range(blk // CHUNK)],
                         axis=0)
    cbuf[0:CONV_HALO, :] = cbuf[blk:blk + CONV_HALO, :]

    nsl = wa_ref.shape[0]
    wsl = WIDTH // nsl
    xcb = xc.astype(BF16)
    ra = jnp.concatenate([jnp.dot(xcb[:, q * wsl:(q + 1) * wsl], wa_ref[q], preferred_element_type=F32)
                          for q in range(nsl)], axis=1)
    ri = jnp.concatenate([jnp.dot(xcb[:, q * wsl:(q + 1) * wsl], wi_ref[q], preferred_element_type=F32)
                          for q in range(nsl)], axis=1)
    r = jax.nn.sigmoid(ra + ba_ref[...])
    i = jax.nn.sigmoid(ri + bi_ref[...])
    log_a = -LRU_C * r * _softplus(-lam_ref[...])
    a = jnp.exp(log_a)
    a_s[...] = a
    u_s[...] = jnp.sqrt(-jnp.tanh(log_a) * (a * a + 1.0)) * (i * xc)

    rows = lax.broadcasted_iota(jnp.int32, (SUBLANE, WIDTH), 0)

    def group(g, carry):
        r0 = pl.multiple_of(g * SUBLANE, SUBLANE)
        a = a_s[pl.ds(r0, SUBLANE), :]
        u = u_s[pl.ds(r0, SUBLANE), :]
        for k in (1, 2, 4):
            keep = rows >= k
            a_prev = jnp.where(keep, pltpu.roll(a, k, 0), 1.0)
            u_prev = jnp.where(keep, pltpu.roll(u, k, 0), 0.0)
            u = a * u_prev + u
            a = a * a_prev
        h = a * carry + u
        u_s[pl.ds(r0, SUBLANE), :] = h
        return jnp.broadcast_to(h[SUBLANE - 1:SUBLANE, :], (SUBLANE, WIDTH))

    carry_s[...] = lax.fori_loop(0, blk // SUBLANE, group, carry_s[...])
    o_ref[...] = (u_s[...] * jax.nn.gelu(gate_ref[...].astype(F32), approximate=True)).astype(o_ref.dtype)


def _block_diag_slabs(w, per_slab):
    g, n, _ = w.shape
    w = w.reshape(g // per_slab, per_slab, n, n)
    eye = jnp.eye(per_slab, dtype=w.dtype)
    out = jnp.einsum("spij,pq->spiqj", w, eye)
    return out.reshape(g // per_slab, per_slab * n, per_slab * n)


def _lru_branch(proj, b, s, conv_w, conv_b, w_a, b_a, w_i, b_i, lam, *, blk=256):
    nblk = s // blk
    per_slab = 4
    wa = _block_diag_slabs(w_a, per_slab).astype(BF16)
    wi = _block_diag_slabs(w_i, per_slab).astype(BF16)
    row = lambda i, j: i * nblk + j
    const = lambda i, j: (0, 0)
    const3 = lambda i, j: (0, 0, 0)
    vec = lambda v: v.reshape(1, WIDTH)
    return pl.pallas_call(
        functools.partial(_lru_kernel, blk=blk),
        grid=(b, nblk),
        in_specs=[pl.BlockSpec((blk, WIDTH), lambda i, j: (row(i, j), _col_block("lgate"))),
                  pl.BlockSpec((blk, WIDTH), lambda i, j: (row(i, j), _col_block("lx"))),
                  pl.BlockSpec((LRU_CONV, WIDTH), const),
                  pl.BlockSpec((1, WIDTH), const),
                  pl.BlockSpec(wa.shape, const3),
                  pl.BlockSpec((1, WIDTH), const),
                  pl.BlockSpec(wi.shape, const3),
                  pl.BlockSpec((1, WIDTH), const),
                  pl.BlockSpec((1, WIDTH), const)],
        out_specs=pl.BlockSpec((blk, WIDTH), lambda i, j: (row(i, j), 0)),
        out_shape=jax.ShapeDtypeStruct((b * s, WIDTH), BF16),
        scratch_shapes=[pltpu.VMEM((CONV_HALO + blk, WIDTH), BF16),
                        pltpu.VMEM((blk, WIDTH), F32),
                        pltpu.VMEM((blk, WIDTH), F32),
                        pltpu.VMEM((SUBLANE, WIDTH), F32)],
        compiler_params=_compiler_params(("parallel", "arbitrary")),
        name="rglru_branch",
    )(proj, proj, conv_w, vec(conv_b), wa, vec(b_a), wi, vec(b_i), vec(lam))


def _ret_kernel(q_ref, k_ref, v_ref, g_ref, cos_ref, sin_ref, inner_ref, kte_ref, qfs_ref, cdec_ref,
                nw_ref, o_ref, state, ybuf, *, blk):
    @pl.when(pl.program_id(1) == 0)
    def _():
        state[...] = jnp.zeros_like(state)

    lane = lax.broadcasted_iota(jnp.int32, (CHUNK, RET_QK), 1)
    first_half = (lane % RET_QK_DIM) < (RET_QK_DIM // 2)
    half = RET_QK_DIM // 2

    def rotary(x, cos, sin):
        swapped = jnp.where(first_half, pltpu.roll(x, RET_QK - half, 1), pltpu.roll(x, half, 1))
        return x * cos + swapped * sin

    left = lax.broadcasted_iota(jnp.int32, (CHUNK, LANE), 1) < RET_QK_DIM
    npair = RET_HEADS // 2

    def halves(x):
        return jnp.concatenate([jnp.where(left, x, 0.0), jnp.where(left, 0.0, x)], axis=0)

    def chunk(c, carry):
        rows = slice(c * CHUNK, (c + 1) * CHUNK)
        cos = cos_ref[rows, :]
        sin = sin_ref[rows, :]
        q = rotary(q_ref[rows, :].astype(F32), cos, sin)
        k = rotary(k_ref[rows, :].astype(F32), cos, sin) * (RET_QK_DIM ** -0.5)
        k_end = k * kte_ref[...]
        q_start = q * qfs_ref[...]
        slabs = [slice(p * LANE, (p + 1) * LANE) for p in range(npair)]
        v2 = [v_ref[rows, 2 * p * RET_V_DIM:(2 * p + 2) * RET_V_DIM] for p in range(npair)]
        v2 = [jnp.concatenate([v[:, 0:RET_V_DIM], v[:, RET_V_DIM:]], axis=0) for v in v2]
        prev = [state[p] for p in range(npair)]
        s2 = [_bdot_nt(q[:, ps], halves(k[:, ps])) * inner_ref[p] for p, ps in enumerate(slabs)]
        kv = [_bdot_tn(halves(k_end[:, ps]), v2[p]) for p, ps in enumerate(slabs)]
        for p, ps in enumerate(slabs):
            rhs = jnp.concatenate([v2[p], prev[p]], axis=0)
            for e in range(2):
                keep = left if e == 0 else jnp.logical_not(left)
                lhs = jnp.concatenate([jnp.where(keep, s2[p], 0.0), jnp.where(keep, q_start[:, ps], 0.0)],
                                      axis=1)
                y = _bdot(lhs, rhs)
                mu = jnp.mean(y, axis=-1, keepdims=True)
                yc = y - mu
                var = jnp.mean(yc * yc, axis=-1, keepdims=True)
                h = 2 * p + e
                ybuf[rows, h * RET_V_DIM:(h + 1) * RET_V_DIM] = yc * lax.rsqrt(var + EPS)
            state[p] = prev[p] * cdec_ref[p] + kv[p]
        return carry

    for c in range(blk // CHUNK):
        chunk(c, 0)
        rows = slice(c * CHUNK, (c + 1) * CHUNK)
        o_ref[rows, :] = (_silu(g_ref[rows, :].astype(F32)) * (ybuf[rows, :] * nw_ref[...])).astype(o_ref.dtype)


def _ret_tables(s):
    pos_s = jnp.arange(s, dtype=F32)
    inv_freq = ROPE_BASE ** (-jnp.arange(0, RET_QK_DIM, 2, dtype=F32) / RET_QK_DIM)
    ang = pos_s[:, None] * inv_freq[None, :]
    cos, sin = lax.optimization_barrier((jnp.cos(ang), jnp.sin(ang)))
    cos_f = jnp.tile(jnp.concatenate([cos, cos], axis=1), (1, RET_HEADS))
    sin_f = jnp.tile(jnp.concatenate([-sin, sin], axis=1), (1, RET_HEADS))
    log_gamma = jnp.log1p(-jnp.exp2(-5.0 - jnp.arange(RET_HEADS, dtype=F32)))
    pos = jnp.arange(CHUNK, dtype=F32)
    inner = jnp.exp(log_gamma[:, None, None] * jnp.abs(pos[:, None] - pos[None, :]))
    k_to_end = jnp.exp(log_gamma[:, None] * (CHUNK - 1.0 - pos))
    q_from_start = jnp.exp(log_gamma[:, None] * (pos + 1.0))
    kte = jnp.repeat(k_to_end.T, RET_QK_DIM, axis=1)
    qfs = jnp.repeat(q_from_start.T, RET_QK_DIM, axis=1)
    inner2 = inner.reshape(RET_HEADS // 2, 2, CHUNK, CHUNK).transpose(0, 2, 1, 3).reshape(
        RET_HEADS // 2, CHUNK, 2 * CHUNK)
    cdec = jnp.broadcast_to(jnp.repeat(jnp.exp(log_gamma * CHUNK), RET_QK_DIM).reshape(
        RET_HEADS // 2, 2 * RET_QK_DIM, 1), (RET_HEADS // 2, 2 * RET_QK_DIM, RET_V_DIM))
    return cos_f, sin_f, inner2, kte, qfs, cdec


def _ret_branch(proj, b, s, norm_w, tables, *, blk=256):
    nblk = s // blk
    cos_f, sin_f, inner, kte, qfs, cdec = tables
    row = lambda i, j: i * nblk + j
    const = lambda i, j: (0, 0)
    return pl.pallas_call(
        functools.partial(_ret_kernel, blk=blk),
        grid=(b, nblk),
        in_specs=[pl.BlockSpec((blk, RET_QK), lambda i, j: (row(i, j), _col_block("rq"))),
                  pl.BlockSpec((blk, RET_QK), lambda i, j: (row(i, j), _col_block("rk"))),
                  pl.BlockSpec((blk, WIDTH), lambda i, j: (row(i, j), _col_block("rv"))),
                  pl.BlockSpec((blk, WIDTH), lambda i, j: (row(i, j), _col_block("rg"))),
                  pl.BlockSpec((blk, RET_QK), lambda i, j: (j, 0)),
                  pl.BlockSpec((blk, RET_QK), lambda i, j: (j, 0)),
                  pl.BlockSpec(inner.shape, lambda i, j: (0, 0, 0)),
                  pl.BlockSpec((CHUNK, RET_QK), const),
                  pl.BlockSpec((CHUNK, RET_QK), const),
                  pl.BlockSpec(cdec.shape, lambda i, j: (0, 0, 0)),
                  pl.BlockSpec((1, WIDTH), const)],
        out_specs=pl.BlockSpec((blk, WIDTH), lambda i, j: (row(i, j), 0)),
        out_shape=jax.ShapeDtypeStruct((b * s, WIDTH), BF16),
        scratch_shapes=[pltpu.VMEM((RET_HEADS // 2, 2 * RET_QK_DIM, RET_V_DIM), F32),
                        pltpu.VMEM((blk, WIDTH), F32)],
        compiler_params=_compiler_params(("parallel", "arbitrary")),
        name="retention_branch",
    )(proj, proj, proj, proj, cos_f, sin_f, inner, kte, qfs, cdec, norm_w.reshape(1, -1))


def _rwkv_kernel(r_ref, k_ref, v_ref, l_ref, mur_ref, muk_ref, muv_ref, mul_ref, w0_ref, w2_ref,
                 a0_ref, a2_ref, g2_ref, kk_ref, ka_ref, rk_ref, lnw_ref, lnb_ref, o_ref,
                 tail, tail_l, r_s, k_s, v_s, kk_s, b_s, lw_s, o_s, ub_s, op_s, gate_s, bon_s, wall_s, state,
                 *, blk, hw):
    nh = hw // RWKV_HEAD

    @pl.when(pl.program_id(2) == 0)
    def _():
        tail[...] = jnp.zeros_like(tail)
        tail_l[...] = jnp.zeros_like(tail_l)
        state[...] = jnp.zeros_like(state)

    group = A_UNROLL * CHUNK
    ngroups = blk // group
    rows = lax.broadcasted_iota(jnp.int32, (group, 1), 0)

    def shift_mix(x, prev_row, mu):
        prev = jnp.where(rows == 0, prev_row, pltpu.roll(x, 1, 0))
        return x + (prev - x) * mu

    lane = lax.broadcasted_iota(jnp.int32, (LANE, LANE), 1) // RWKV_HEAD
    lrow = lax.broadcasted_iota(jnp.int32, (LANE, LANE), 0) // RWKV_HEAD
    head_ones = (lane == lrow).astype(BF16)

    def head_sum(x):
        hi = x.astype(BF16)
        lo = (x - hi.astype(F32)).astype(BF16)
        return jnp.concatenate(
            [jnp.dot(hi[:, q:q + LANE], head_ones, preferred_element_type=F32)
             + jnp.dot(lo[:, q:q + LANE], head_ones, preferred_element_type=F32)
             for q in range(0, hw, LANE)], axis=1)

    def prep(gi, last):
        rs = slice(gi * group, (gi + 1) * group)
        r_raw, k_raw, v_raw, l_raw = (ref[rs, :].astype(F32) for ref in (r_ref, k_ref, v_ref, l_ref))
        r = shift_mix(r_raw, last[0], mur_ref[...])
        k = shift_mix(k_raw, last[1], muk_ref[...])
        v = shift_mix(v_raw, last[2], muv_ref[...])
        lo_ = shift_mix(l_raw, last[3], mul_ref[...])
        wl = lo_[:, 0:RWKV_W_LORA]
        al = lo_[:, RWKV_W_LORA:RWKV_W_LORA + RWKV_A_LORA]
        gl = lo_[:, RWKV_W_LORA + RWKV_A_LORA:]
        wz = w0_ref[...] + _bdot(jnp.tanh(wl), w2_ref[...])
        w = jnp.minimum(wz, 0.0) - jnp.log(1.0 + jnp.exp(-jnp.abs(wz))) - 0.5
        a = jax.nn.sigmoid(a0_ref[...] + _bdot(al, a2_ref[...]))
        gate_s[rs, :] = _bdot(jax.nn.sigmoid(gl), g2_ref[...])
        kk = k * kk_ref[...]
        kk = kk * lax.rsqrt(jnp.maximum(head_sum(kk * kk), 1e-24))
        k = k * (1.0 + (a - 1.0) * ka_ref[...])
        bon_s[rs, :] = head_sum(r * k * rk_ref[...]) * v
        r_s[rs, :] = r
        k_s[rs, :] = k
        v_s[rs, :] = v
        kk_s[rs, :] = kk
        b_s[rs, :] = kk * a
        lw_s[rs, :] = -jnp.exp(w)
        return [x[group - 1:group, :] for x in (r_raw, k_raw, v_raw, l_raw)]

    last = [tail[0:1, :], tail[1:2, :], tail[2:3, :], tail_l[0:1, :]]
    for gi in range(ngroups):
        last = prep(gi, last)
    tail[0:1, :], tail[1:2, :], tail[2:3, :], tail_l[0:1, :] = last

    trow = lax.broadcasted_iota(jnp.int32, (CHUNK, 2 * CHUNK), 0)
    tcol = lax.broadcasted_iota(jnp.int32, (CHUNK, 2 * CHUNK), 1) % CHUNK
    strict = tcol < trow
    incl = tcol <= trow
    left = lax.broadcasted_iota(jnp.int32, (CHUNK, LANE), 1) < RWKV_HEAD
    left2 = lax.broadcasted_iota(jnp.int32, (2 * CHUNK, LANE), 1) < RWKV_HEAD

    eye2 = (tcol == trow).astype(F32)
    zeros = jnp.zeros((CHUNK, LANE), F32)
    slabs = [slice(p * LANE, (p + 1) * LANE) for p in range(nh // 2)]
    lcat = lambda a, b: jnp.concatenate([a, b], axis=1)
    rcat = lambda *a: jnp.concatenate(a, axis=0)
    keep_l = lambda a: jnp.where(left, a, 0.0)
    keep_r = lambda a: jnp.where(left, 0.0, a)

    def phase_a(g):
        cs = [g * A_UNROLL + i for i in range(A_UNROLL)]
        sls = [slice(c * CHUNK, (c + 1) * CHUNK) for c in cs]
        pre = []
        for sl in sls:
            lw = lw_s[sl, :]
            cum = _cumsum_rows(lw)
            cum_last = cum[CHUNK - 1:CHUNK, :]
            w_inv = jnp.exp(-cum)
            to_end = jnp.exp(cum_last - cum)
            pre.append(dict(rt=r_s[sl, :] * jnp.exp(cum), kt=k_s[sl, :] * w_inv, bt=b_s[sl, :] * w_inv,
                            kap=kk_s[sl, :] * jnp.exp(cum - lw), k_end=k_s[sl, :] * to_end,
                            b_end=b_s[sl, :] * to_end, w_all=jnp.exp(cum_last), vv=v_s[sl, :]))
        items = [(i, ps) for i in range(A_UNROLL) for ps in slabs]
        pm2 = []
        for i, ps in items:
            d = pre[i]
            lhs = rcat(d["kap"][:, ps], d["rt"][:, ps])
            rhs_e = jnp.where(left2, rcat(d["bt"][:, ps], d["kt"][:, ps]), 0.0)
            rhs_o = jnp.where(left2, 0.0, rcat(d["kt"][:, ps], d["bt"][:, ps]))
            pm2.append(_bdot_nt(lhs, rcat(rhs_e, rhs_o)))
        top = [(jnp.where(strict, m[0:CHUNK, 0:LANE], 0.0), jnp.where(strict, m[0:CHUNK, LANE:], 0.0))
               for m in pm2]
        bot = [(jnp.where(incl, m[CHUNK:, 0:LANE], 0.0), jnp.where(incl, m[CHUNK:, LANE:], 0.0))
               for m in pm2]
        v_swp = [pltpu.roll(pre[i]["vv"][:, ps], RWKV_HEAD, 1) for i, ps in items]
        v_oe = [rcat(keep_l(v), keep_r(v)) for v in v_swp]
        av = [_bdot(jnp.where(left, to, te), v_oe[n]) for n, (te, to) in enumerate(top)]
        op = [_bdot(jnp.where(left, bo, be), v_oe[n]) for n, (be, bo) in enumerate(bot)]
        z = [(jnp.where(left, -te, eye2), jnp.where(left, eye2, -to)) for te, to in top]

        def step(ze, zo):
            return _bdot(jnp.where(left, ze, zo), rcat(lcat(ze, zeros), lcat(zeros, zo)))

        span = 1
        while 2 * span < CHUNK:
            res = [step(ze, zo) for ze, zo in z]
            z = [(r[:, 0:LANE] + keep_r(ze), r[:, LANE:] + keep_l(zo)) for r, (ze, zo) in zip(res, z)]
            span *= 2
        res = [step(ze, zo) for ze, zo in z]
        t_oe = [jnp.where(left, r[:, LANE:] + zo, r[:, 0:LANE] + ze) for r, (ze, zo) in zip(res, z)]
        for n, (i, ps) in enumerate(items):
            kap = pre[i]["kap"][:, ps]
            prod = _bdot(t_oe[n], lcat(rcat(keep_l(av[n]), keep_r(av[n])), rcat(keep_r(kap), keep_l(kap))))
            sl = sls[i]
            lw_s[sl, ps] = prod[:, 0:LANE]
            kk_s[sl, ps] = prod[:, LANE:]
            ub_s[sl, ps] = jnp.where(left, bot[n][0], bot[n][1])
            op_s[sl, ps] = op[n]
            v_s[sl, ps] = v_swp[n]
        for i, sl in enumerate(sls):
            r_s[sl, :] = pre[i]["rt"]
            k_s[sl, :] = pre[i]["k_end"]
            b_s[sl, :] = pre[i]["b_end"]
            wall_s[cs[i]] = jnp.broadcast_to(pre[i]["w_all"], (SUBLANE, hw))

    def phase_b(c):
        sl = slice(c * CHUNK, (c + 1) * CHUNK)
        w_all = wall_s[c]
        s0 = [(state[2 * p], state[2 * p + 1]) for p in range(nh // 2)]
        s_oe = [rcat(so, se) for se, so in s0]
        u = [-(lw_s[sl, ps] + _bdot_nt(kk_s[sl, ps], s_oe[p])) for p, ps in enumerate(slabs)]
        op = [op_s[sl, ps] + _bdot_nt(r_s[sl, ps], s_oe[p]) for p, ps in enumerate(slabs)]
        for p, ps in enumerate(slabs):
            o_sw = _bdot(ub_s[sl, ps], rcat(keep_r(u[p]), keep_l(u[p]))) + op[p]
            o_s[sl, ps] = pltpu.roll(o_sw, RWKV_HEAD, 1)
        for p, ps in enumerate(slabs):
            add = _bdot_tn(rcat(u[p], v_s[sl, ps]), rcat(b_s[sl, ps], k_s[sl, ps]))
            state[2 * p] = s0[p][0] * w_all[0:1, ps] + keep_l(add[RWKV_HEAD:, :])
            state[2 * p + 1] = s0[p][1] * w_all[0:1, ps] + keep_r(add[0:RWKV_HEAD, :])

    def finish(gi):
        rs = slice(gi * group, (gi + 1) * group)
        o = o_s[rs, :]
        mu = head_sum(o) * (1.0 / RWKV_HEAD)
        oc = o - mu
        var = head_sum(oc * oc) * (1.0 / RWKV_HEAD)
        o = oc * lax.rsqrt(var + RWKV_LN_EPS) * lnw_ref[...] + lnb_ref[...]
        o_ref[rs, :] = ((o + bon_s[rs, :]) * gate_s[rs, :]).astype(o_ref.dtype)

    for g in range(ngroups):
        phase_a(g)
    for g in range(ngroups):
        for c in range(g * A_UNROLL, (g + 1) * A_UNROLL):
            phase_b(c)
        finish(g)


def _rwkv_branch(proj, b, s, mu, w0, w2, a0, a2, g2, k_k, k_a, r_k, ln_w, ln_b, *, blk=256, hw=WIDTH):
    nblk = s // blk
    nslab = WIDTH // hw
    row = lambda i, p, j: i * nblk + j
    vec = lambda v: v.reshape(1, -1)
    hcol = lambda i, p, j: (0, p)
    mu_r, mu_k, mu_v, mu_l = (mu[0:WIDTH], mu[WIDTH:2 * WIDTH], mu[2 * WIDTH:3 * WIDTH],
                              mu[3 * WIDTH:])
    wblk = _col_block("wr")
    lblk = _col_block("lora")
    return pl.pallas_call(
        functools.partial(_rwkv_kernel, blk=blk, hw=hw),
        grid=(b, nslab, nblk),
        in_specs=[pl.BlockSpec((blk, hw), lambda i, p, j: (row(i, p, j), wblk * nslab + p)),
                  pl.BlockSpec((blk, hw), lambda i, p, j: (row(i, p, j), (wblk + 1) * nslab + p)),
                  pl.BlockSpec((blk, hw), lambda i, p, j: (row(i, p, j), (wblk + 2) * nslab + p)),
                  pl.BlockSpec((blk, RWKV_LORA), lambda i, p, j: (row(i, p, j), lblk)),
                  pl.BlockSpec((1, hw), hcol),
                  pl.BlockSpec((1, hw), hcol),
                  pl.BlockSpec((1, hw), hcol),
                  pl.BlockSpec((1, RWKV_LORA), lambda i, p, j: (0, 0)),
                  pl.BlockSpec((1, hw), hcol),
                  pl.BlockSpec((RWKV_W_LORA, hw), hcol),
                  pl.BlockSpec((1, hw), hcol),
                  pl.BlockSpec((RWKV_A_LORA, hw), hcol),
                  pl.BlockSpec((RWKV_G_LORA, hw), hcol),
                  pl.BlockSpec((1, hw), hcol),
                  pl.BlockSpec((1, hw), hcol),
                  pl.BlockSpec((1, hw), hcol),
                  pl.BlockSpec((1, hw), hcol),
                  pl.BlockSpec((1, hw), hcol)],
        out_specs=pl.BlockSpec((blk, hw), lambda i, p, j: (row(i, p, j), p)),
        out_shape=jax.ShapeDtypeStruct((b * s, WIDTH), BF16),
        scratch_shapes=[pltpu.VMEM((SUBLANE, hw), F32), pltpu.VMEM((SUBLANE, RWKV_LORA), F32)]
        + [pltpu.VMEM((blk, hw), F32)] * 11
        + [pltpu.VMEM((blk // CHUNK, SUBLANE, hw), F32)]
        + [pltpu.VMEM((hw // RWKV_HEAD, RWKV_HEAD, LANE), F32)],
        compiler_params=_compiler_params(("parallel", "parallel", "arbitrary")),
        name="rwkv7_branch",
    )(proj, proj, proj, proj, vec(mu_r), vec(mu_k), vec(mu_v), vec(mu_l), vec(w0), w2.astype(BF16),
      vec(a0), a2.astype(BF16), g2.astype(BF16), vec(k_k), vec(k_a), vec(r_k), vec(ln_w), vec(ln_b))


def _merge_kernel(y0_ref, y1_ref, y2_ref, y3_ref, gl_ref, x_ref, wb_ref, wo_ref, o_ref):
    merged = None
    for m, y_ref in enumerate((y0_ref, y1_ref, y2_ref, y3_ref)):
        br = jnp.dot(y_ref[...], wb_ref[m], preferred_element_type=F32)
        term = jax.nn.sigmoid(gl_ref[:, m * D_MODEL:(m + 1) * D_MODEL].astype(F32)) * br
        merged = term if merged is None else merged + term
    o_ref[...] = x_ref[...] + jnp.dot(merged.astype(BF16), wo_ref[...], preferred_element_type=F32)


def _merge(ys, proj, x2, w_branch, w_out, layer, *, tm=512):
    t = x2.shape[0]
    tok = lambda i: (i, 0)
    return pl.pallas_call(
        _merge_kernel,
        grid=(t // tm,),
        in_specs=[pl.BlockSpec((tm, WIDTH), tok)] * N_BRANCH
        + [pl.BlockSpec((tm, N_BRANCH * D_MODEL), lambda i: (i, _col_block("gates"))),
           pl.BlockSpec((tm, D_MODEL), tok),
           pl.BlockSpec((None, N_BRANCH, WIDTH, D_MODEL), lambda i: (layer, 0, 0, 0)),
           pl.BlockSpec((None, D_MODEL, D_MODEL), lambda i: (layer, 0, 0))],
        out_specs=pl.BlockSpec((tm, D_MODEL), tok),
        out_shape=jax.ShapeDtypeStruct((t, D_MODEL), F32),
        compiler_params=_compiler_params(("parallel",)),
        name="branch_merge",
    )(*ys, proj, x2, w_branch, w_out)


FFN_TC = 256
FFN_AHEAD = 3


def _ffn_kernel(x_ref, g_ref, wu_ref, cw_ref, cb_ref, wd_ref, gf_ref, o_ref, cbuf, tail, *, tm, final):
    @pl.when(pl.program_id(1) == 0)
    def _():
        tail[...] = jnp.zeros_like(tail)

    x = x_ref[...]
    ms = jnp.mean(x * x, axis=-1, keepdims=True)
    hn = (x * lax.rsqrt(ms + EPS) * g_ref[...]).astype(BF16)
    nsteps = D_FF // FFN_TC

    def up_conv(i):
        halves = []
        for half in range(2):
            c0 = half * D_FF + i * FFN_TC
            cs = slice(c0, c0 + FFN_TC)
            buf = cbuf.at[2 * (i % (FFN_AHEAD + 1)) + half]
            buf[0:SUBLANE, :] = tail[:, cs]
            buf[SUBLANE:SUBLANE + tm, :] = jnp.dot(hn, wu_ref[:, cs], preferred_element_type=F32)
            y = cb_ref[:, cs]
            for k in range(FFN_CONV):
                s = FFN_CONV - 1 - k
                y = y + cw_ref[k:k + 1, cs] * buf[SUBLANE - s:SUBLANE - s + tm, :]
            tail[:, cs] = buf[tm:tm + SUBLANE, :]
            halves.append(y)
        return halves

    acc = x
    ups = [up_conv(i) for i in range(FFN_AHEAD)]
    for i in range(nsteps):
        if i + FFN_AHEAD < nsteps:
            ups.append(up_conv(i + FFN_AHEAD))
        val, gt = ups[i]
        acc = acc + jnp.dot((_silu(gt) * val).astype(BF16), wd_ref[i * FFN_TC:(i + 1) * FFN_TC, :],
                            preferred_element_type=F32)
    if final:
        ms = jnp.mean(acc * acc, axis=-1, keepdims=True)
        acc = acc * lax.rsqrt(ms + EPS) * gf_ref[...]
    o_ref[...] = acc


def _conv_ffn(x2, b, s, gain, w_up, conv_w, conv_b, w_down, gain_final, layer, *, final, tm=512):
    nblk = s // tm
    row = lambda i, j: (i * nblk + j, 0)
    const = lambda i, j: (0, 0)
    return pl.pallas_call(
        functools.partial(_ffn_kernel, tm=tm, final=final),
        grid=(b, nblk),
        in_specs=[pl.BlockSpec((tm, D_MODEL), row),
                  pl.BlockSpec((1, D_MODEL), const),
                  pl.BlockSpec((None, D_MODEL, 2 * D_FF), lambda i, j: (layer, 0, 0)),
                  pl.BlockSpec((FFN_CONV, 2 * D_FF), const),
                  pl.BlockSpec((1, 2 * D_FF), const),
                  pl.BlockSpec((None, D_FF, D_MODEL), lambda i, j: (layer, 0, 0)),
                  pl.BlockSpec((1, D_MODEL), const)],
        out_specs=pl.BlockSpec((tm, D_MODEL), row),
        out_shape=jax.ShapeDtypeStruct((b * s, D_MODEL), F32),
        scratch_shapes=[pltpu.VMEM((2 * (FFN_AHEAD + 1), tm + 2 * SUBLANE, FFN_TC), F32),
                        pltpu.VMEM((SUBLANE, 2 * D_FF), F32)],
        compiler_params=_compiler_params(("parallel", "arbitrary")),
        name="conv_ffn",
    )(x2, gain.reshape(1, -1), w_up, conv_w, conv_b.reshape(1, -1), w_down, gain_final.reshape(1, -1))


def _reorder_w_in(w):
    cols = []
    for n in _DST_ORDER:
        off, wd = _SRC[n]
        cols.append(w[..., off:off + wd])
    pad = N_PROJ - _DST["dt"][0] - SSM_HEADS
    cols.append(jnp.zeros(w.shape[:-1] + (pad,), w.dtype))
    return jnp.concatenate(cols, axis=-1)


def kernel(x, norm_mix, w_in, ssm_conv_w, ssm_conv_b, ssm_dt_bias, ssm_a_log, ssm_d, ssm_norm, lru_conv_w, lru_conv_b, lru_w_a, lru_b_a, lru_w_i, lru_b_i, lru_lam, ret_norm, rwkv_mu, rwkv_w0, rwkv_w2, rwkv_a0, rwkv_a2, rwkv_g2, rwkv_k_k, rwkv_k_a, rwkv_r_k, rwkv_ln_w, rwkv_ln_b, w_branch, w_out, norm_ffn, ffn_up, ffn_conv_w, ffn_conv_b, ffn_down, norm_final):
    b, s, d = x.shape
    x2 = x.reshape(b * s, d)
    tables = _ret_tables(s)
    w_in_b = _cast_bf16(w_in)
    w_branch_b, w_out_b = _cast_bf16(w_branch), _cast_bf16(w_out)
    ffn_up_b, ffn_down_b = _cast_bf16(ffn_up), _cast_bf16(ffn_down)
    for l in range(DEPTH):
        proj = _norm_matmul(x2, norm_mix[l], _reorder_w_in(w_in_b[l])[None], 0)
        y_ssd = _ssd_branch(proj, b, s, ssm_conv_w[l], ssm_conv_b[l], ssm_dt_bias[l], ssm_a_log[l],
                            ssm_d[l], ssm_norm[l])
        y_lru = _lru_branch(proj, b, s, lru_conv_w[l], lru_conv_b[l], lru_w_a[l], lru_b_a[l],
                            lru_w_i[l], lru_b_i[l], lru_lam[l])
        y_ret = _ret_branch(proj, b, s, ret_norm[l], tables)
        y_rwkv = _rwkv_branch(proj, b, s, rwkv_mu[l], rwkv_w0[l], rwkv_w2[l], rwkv_a0[l], rwkv_a2[l],
                              rwkv_g2[l], rwkv_k_k[l], rwkv_k_a[l], rwkv_r_k[l], rwkv_ln_w[l],
                              rwkv_ln_b[l])
        x2 = _merge((y_ssd, y_lru, y_ret, y_rwkv), proj, x2, w_branch_b, w_out_b, l)
        x2 = _conv_ffn(x2, b, s, norm_ffn[l], ffn_up_b, ffn_conv_w[l], ffn_conv_b[l], ffn_down_b,
                       norm_final, l, final=(l == DEPTH - 1))
    return x2.reshape(b, s, d)
```

```python
import functools

import jax
import jax.numpy as jnp
from jax import lax
from jax.experimental import pallas as pl
from jax.experimental.pallas import tpu as pltpu

F32 = jnp.float32
BF16 = jnp.bfloat16

D_MODEL = 1024
WIDTH = 1024
DEPTH = 2
CHUNK = 64
EPS = 1e-6

SSM_HEADS = 16
SSM_HEAD_DIM = 64
SSM_GROUPS = 4
SSM_STATE = 128
SSM_CONV = 4
SSM_XBC = WIDTH + 2 * SSM_GROUPS * SSM_STATE

LRU_BLOCKS = 16
LRU_BLOCK = 64
LRU_CONV = 4
LRU_C = 8.0

RET_HEADS = 8
RET_QK_DIM = 64
RET_V_DIM = 128
RET_QK = RET_HEADS * RET_QK_DIM
ROPE_BASE = 10000.0

RWKV_HEAD = 64
RWKV_HEADS = 16
RWKV_W_LORA = 64
RWKV_A_LORA = 64
RWKV_G_LORA = 128
RWKV_LORA = RWKV_W_LORA + RWKV_A_LORA + RWKV_G_LORA
RWKV_LN_EPS = 64e-5
A_UNROLL = 2

D_FF = 2816
FFN_CONV = 3
N_BRANCH = 4

LANE = 128
SUBLANE = 8
VMEM_LIMIT = 56 * 1024 * 1024

_SRC = {}
_o = 0
for _n, _w in (("z", WIDTH), ("xbc", SSM_XBC), ("dt", SSM_HEADS), ("lgate", WIDTH), ("lx", WIDTH),
               ("rq", RET_QK), ("rk", RET_QK), ("rv", WIDTH), ("rg", WIDTH),
               ("wr", WIDTH), ("wk", WIDTH), ("wv", WIDTH), ("lora", RWKV_LORA),
               ("gates", N_BRANCH * D_MODEL)):
    _SRC[_n] = (_o, _w)
    _o += _w
N_IN = _o
_DST_ORDER = ("gates", "xbc", "z", "lgate", "lx", "rv", "rg", "rq", "rk", "wr", "wk", "wv", "lora", "dt")
_DST = {}
_o = 0
for _n in _DST_ORDER:
    _w = _SRC[_n][1]
    _bw = max(_w, LANE)
    assert _o % _bw == 0, (_n, _o, _bw)
    _DST[_n] = (_o, _bw)
    _o += _bw
PROJ_TN = 1024
N_PROJ = -(-_o // PROJ_TN) * PROJ_TN


def _col_block(name):
    off, bw = _DST[name]
    return off // bw


def _softplus(x):
    return jnp.maximum(x, 0.0) + jnp.log1p(jnp.exp(-jnp.abs(x)))


def _silu(x):
    return x * jax.nn.sigmoid(x)


def _bdot(a, b):
    return jnp.dot(a.astype(BF16), b.astype(BF16), preferred_element_type=F32)


def _bdot_nt(a, b):
    return lax.dot_general(a.astype(BF16), b.astype(BF16), (((1,), (1,)), ((), ())),
                           preferred_element_type=F32)


def _bdot_tn(a, b):
    return lax.dot_general(a.astype(BF16), b.astype(BF16), (((0,), (0,)), ((), ())),
                           preferred_element_type=F32)


def _cumsum_rows(x):
    n = x.shape[0]
    row = lax.broadcasted_iota(jnp.int32, (n, n), 0)
    col = lax.broadcasted_iota(jnp.int32, (n, n), 1)
    tri = (col <= row).astype(BF16)
    hi = x.astype(BF16)
    rest = x - hi.astype(F32)
    mid = rest.astype(BF16)
    lo = (rest - mid.astype(F32)).astype(BF16)
    return (jnp.dot(tri, hi, preferred_element_type=F32) + jnp.dot(tri, mid, preferred_element_type=F32)
            + jnp.dot(tri, lo, preferred_element_type=F32))


CONV_HALO = CHUNK


def _shift_matrix(nshift):
    r = lax.broadcasted_iota(jnp.int32, (nshift * CHUNK, CONV_HALO + CHUNK), 0)
    c = lax.broadcasted_iota(jnp.int32, (nshift * CHUNK, CONV_HALO + CHUNK), 1)
    return (c == CONV_HALO + r % CHUNK - (r // CHUNK + 1)).astype(BF16)


def _causal_conv_chunk(xb, c, cw_ref, cb_ref, shifts):
    x2 = xb[c * CHUNK:c * CHUNK + CONV_HALO + CHUNK, :]
    sh = jnp.dot(shifts, x2, preferred_element_type=F32)
    k = cw_ref.shape[0]
    y = cb_ref[...] + cw_ref[k - 1:k, :] * x2[CONV_HALO:, :].astype(F32)
    for s in range(1, k):
        y = y + cw_ref[k - 1 - s:k - s, :] * sh[(s - 1) * CHUNK:s * CHUNK, :]
    return y


def _compiler_params(sem):
    return pltpu.CompilerParams(dimension_semantics=sem, vmem_limit_bytes=VMEM_LIMIT)


CAST_BLOCK_BYTES = 4 * 1024 * 1024


def _cast_kernel(x_ref, o_ref):
    o_ref[...] = x_ref[...].astype(o_ref.dtype)


def _cast_bf16(x):
    x2 = x.reshape(-1, x.shape[-1])
    r, c = x2.shape
    rows = max(n for n in range(16, r + 1, 16) if r % n == 0 and n * c * 4 <= CAST_BLOCK_BYTES)
    out = pl.pallas_call(
        _cast_kernel,
        grid=(r // rows,),
        in_specs=[pl.BlockSpec((rows, c), lambda i: (i, 0))],
        out_specs=pl.BlockSpec((rows, c), lambda i: (i, 0)),
        out_shape=jax.ShapeDtypeStruct((r, c), BF16),
        compiler_params=_compiler_params(("parallel",)),
        name="cast_bf16",
    )(x2)
    return out.reshape(x.shape)


def _norm_matmul_kernel(x_ref, g_ref, w_ref, o_ref, hn_ref):
    @pl.when(pl.program_id(1) == 0)
    def _():
        x = x_ref[...]
        ms = jnp.mean(x * x, axis=-1, keepdims=True)
        hn_ref[...] = (x * lax.rsqrt(ms + EPS) * g_ref[...]).astype(BF16)

    o_ref[...] = jnp.dot(hn_ref[...], w_ref[...], preferred_element_type=F32).astype(o_ref.dtype)


def _norm_matmul(x2, gain, w, layer, *, tm=2048, tn=PROJ_TN):
    t, d = x2.shape
    n = w.shape[2]
    return pl.pallas_call(
        _norm_matmul_kernel,
        grid=(t // tm, n // tn),
        in_specs=[pl.BlockSpec((tm, d), lambda i, j: (i, 0)),
                  pl.BlockSpec((1, d), lambda i, j: (0, 0)),
                  pl.BlockSpec((None, d, tn), lambda i, j: (layer, 0, j))],
        out_specs=pl.BlockSpec((tm, tn), lambda i, j: (i, j)),
        out_shape=jax.ShapeDtypeStruct((t, n), BF16),
        scratch_shapes=[pltpu.VMEM((tm, d), BF16)],
        compiler_params=_compiler_params(("parallel", "arbitrary")),
        name="norm_in_proj",
    )(x2, gain.reshape(1, d), w)


def _ssd_kernel(z_ref, xbc_ref, dt_ref, cw_ref, cb_ref, dtb_ref, alog_ref, dsk_ref, nw_ref, o_ref,
                cbuf, state, *, blk):
    @pl.when(pl.program_id(1) == 0)
    def _():
        cbuf[0:CONV_HALO, :] = jnp.zeros((CONV_HALO, SSM_XBC), BF16)
        state[...] = jnp.zeros_like(state)

    cbuf[CONV_HALO:CONV_HALO + blk, :] = xbc_ref[...]
    shifts = _shift_matrix(SSM_CONV - 1)

    a_neg = -jnp.exp(alog_ref[...])
    row = lax.broadcasted_iota(jnp.int32, (CHUNK, LANE), 0)
    col = lax.broadcasted_iota(jnp.int32, (CHUNK, LANE), 1) % CHUNK
    causal = col <= row
    diag = col == row
    left = lax.broadcasted_iota(jnp.int32, (CHUNK, LANE), 1) < SSM_HEAD_DIM
    head_of = lax.broadcasted_iota(jnp.int32, (LANE, WIDTH), 1) // SSM_HEAD_DIM
    spread = (head_of == lax.broadcasted_iota(jnp.int32, (LANE, WIDTH), 0)).astype(BF16)
    npair = SSM_HEADS // 2
    pair_group = (SSM_HEADS // SSM_GROUPS) // 2
    slabs = [slice(p * LANE, (p + 1) * LANE) for p in range(npair)]

    def per_channel(x):
        hi = x.astype(BF16)
        lo = (x - hi.astype(F32)).astype(BF16)
        return (jnp.dot(hi, spread, preferred_element_type=F32)
                + jnp.dot(lo, spread, preferred_element_type=F32))

    def local(c):
        r0 = c * CHUNK
        act = _silu(_causal_conv_chunk(cbuf, c, cw_ref, cb_ref, shifts))
        dt = _softplus(dt_ref[r0:r0 + CHUNK, :].astype(F32) + dtb_ref[...])
        a_cum = _cumsum_rows(dt * a_neg)
        full = per_channel(jnp.concatenate([dt, a_cum], axis=0))
        dt_c, ac_c = full[0:CHUNK, :], full[CHUNK:, :]
        a_last = ac_c[CHUNK - 1:CHUNK, :]
        xs = act[:, 0:WIDTH]
        xdt = xs * dt_c
        xdt_end = xdt * jnp.exp(a_last - ac_c)
        bm = [act[:, WIDTH + g * SSM_STATE:WIDTH + (g + 1) * SSM_STATE] for g in range(SSM_GROUPS)]
        cm = [act[:, WIDTH + (SSM_GROUPS + g) * SSM_STATE:WIDTH + (SSM_GROUPS + g + 1) * SSM_STATE]
              for g in range(SSM_GROUPS)]
        cb2 = [_bdot_nt(cm[g], jnp.concatenate([bm[g], bm[g]], axis=0)) for g in range(SSM_GROUPS)]
        y_diag, s_new = [], []
        for p, ps in enumerate(slabs):
            g = p // pair_group
            seg_l = ac_c[:, ps]
            seg_s = jnp.sum(jnp.where(diag, seg_l, 0.0), axis=0, keepdims=True)
            m = cb2[g] * jnp.exp(jnp.where(causal, seg_l - seg_s, -jnp.inf))
            xd = xdt[:, ps]
            xd2 = jnp.concatenate([jnp.where(left, xd, 0.0), jnp.where(left, 0.0, xd)],
                                  axis=0)
            y_diag.append(_bdot(m, xd2) + xs[:, ps] * dsk_ref[:, ps])
            s_new.append(_bdot_tn(bm[g], xdt_end[:, ps]))
        return dict(y=y_diag, s=s_new, cm=cm, from_start=jnp.exp(ac_c), decay=jnp.exp(a_last))

    loc = [local(c) for c in range(blk // CHUNK)]
    cbuf[0:CONV_HALO, :] = cbuf[blk:blk + CONV_HALO, :]
    for c, d in enumerate(loc):
        ys = []
        for p, ps in enumerate(slabs):
            prev = state[p]
            ys.append(d["y"][p] + _bdot(d["cm"][p // pair_group], prev) * d["from_start"][:, ps])
            state[p] = prev * d["decay"][:, ps] + d["s"][p]
        y = jnp.concatenate(ys, axis=1) * _silu(z_ref[c * CHUNK:(c + 1) * CHUNK, :].astype(F32))
        ms = jnp.mean(y * y, axis=-1, keepdims=True)
        o_ref[c * CHUNK:(c + 1) * CHUNK, :] = (y * lax.rsqrt(ms + EPS) * nw_ref[...]).astype(o_ref.dtype)


def _ssd_branch(proj, b, s, conv_w, conv_b, dt_bias, a_log, d_skip, norm_w, *, blk=512):
    nblk = s // blk
    pad = lambda v: jnp.pad(v.reshape(1, -1), ((0, 0), (0, LANE - v.shape[0])))
    dsk = jnp.repeat(d_skip, SSM_HEAD_DIM).reshape(1, WIDTH)
    row = lambda i, j: i * nblk + j
    const = lambda i, j: (0, 0)
    return pl.pallas_call(
        functools.partial(_ssd_kernel, blk=blk),
        grid=(b, nblk),
        in_specs=[pl.BlockSpec((blk, WIDTH), lambda i, j: (row(i, j), _col_block("z"))),
                  pl.BlockSpec((blk, SSM_XBC), lambda i, j: (row(i, j), _col_block("xbc"))),
                  pl.BlockSpec((blk, LANE), lambda i, j: (row(i, j), _col_block("dt"))),
                  pl.BlockSpec((SSM_CONV, SSM_XBC), const),
                  pl.BlockSpec((1, SSM_XBC), const),
                  pl.BlockSpec((1, LANE), const),
                  pl.BlockSpec((1, LANE), const),
                  pl.BlockSpec((1, WIDTH), const),
                  pl.BlockSpec((1, WIDTH), const)],
        out_specs=pl.BlockSpec((blk, WIDTH), lambda i, j: (row(i, j), 0)),
        out_shape=jax.ShapeDtypeStruct((b * s, WIDTH), BF16),
        scratch_shapes=[pltpu.VMEM((CONV_HALO + blk, SSM_XBC), BF16),
                        pltpu.VMEM((SSM_HEADS // 2, SSM_STATE, LANE), F32)],
        compiler_params=_compiler_params(("parallel", "arbitrary")),
        name="ssd_branch",
    )(proj, proj, proj, conv_w, conv_b.reshape(1, -1), pad(dt_bias), pad(a_log), dsk,
      norm_w.reshape(1, -1))


def _lru_kernel(gate_ref, x_ref, cw_ref, cb_ref, wa_ref, ba_ref, wi_ref, bi_ref, lam_ref, o_ref,
                cbuf, a_s, u_s, carry_s, *, blk):
    @pl.when(pl.program_id(1) == 0)
    def _():
        cbuf[0:CONV_HALO, :] = jnp.zeros((CONV_HALO, WIDTH), BF16)
        carry_s[...] = jnp.zeros_like(carry_s)

    cbuf[CONV_HALO:CONV_HALO + blk, :] = x_ref[...]
    shifts = _shift_matrix(LRU_CONV - 1)
    xc = jnp.concatenate([_causal_conv_chunk(cbuf, c, cw_ref, cb_ref, shifts) for c in range(blk // CHUNK)],
                         axis=0)
    cbuf[0:CONV_HALO, :] = cbuf[blk:blk + CONV_HALO, :]

    nsl = wa_ref.shape[0]
    wsl = WIDTH // nsl
    xcb = xc.astype(BF16)
    ra = jnp.concatenate([jnp.dot(xcb[:, q * wsl:(q + 1) * wsl], wa_ref[q], preferred_element_type=F32)
                          for q in range(nsl)], axis=1)
    ri = jnp.concatenate([jnp.dot(xcb[:, q * wsl:(q + 1) * wsl], wi_ref[q], preferred_element_type=F32)
                          for q in range(nsl)], axis=1)
    r = jax.nn.sigmoid(ra + ba_ref[...])
    i = jax.nn.sigmoid(ri + bi_ref[...])
    log_a = -LRU_C * r * _softplus(-lam_ref[...])
    a = jnp.exp(log_a)
    a_s[...] = a
    u_s[...] = jnp.sqrt(-jnp.tanh(log_a) * (a * a + 1.0)) * (i * xc)

    rows = lax.broadcasted_iota(jnp.int32, (SUBLANE, WIDTH), 0)

    def group(g, carry):
        r0 = pl.multiple_of(g * SUBLANE, SUBLANE)
        a = a_s[pl.ds(r0, SUBLANE), :]
        u = u_s[pl.ds(r0, SUBLANE), :]
        for k in (1, 2, 4):
            keep = rows >= k
            a_prev = jnp.where(keep, pltpu.roll(a, k, 0), 1.0)
            u_prev = jnp.where(keep, pltpu.roll(u, k, 0), 0.0)
            u = a * u_prev + u
            a = a * a_prev
        h = a * carry + u
        u_s[pl.ds(r0, SUBLANE), :] = h
        return jnp.broadcast_to(h[SUBLANE - 1:SUBLANE, :], (SUBLANE, WIDTH))

    carry_s[...] = lax.fori_loop(0, blk // SUBLANE, group, carry_s[...])
    o_ref[...] = (u_s[...] * jax.nn.gelu(gate_ref[...].astype(F32), approximate=True)).astype(o_ref.dtype)


def _block_diag_slabs(w, per_slab):
    g, n, _ = w.shape
    w = w.reshape(g // per_slab, per_slab, n, n)
    eye = jnp.eye(per_slab, dtype=w.dtype)
    out = jnp.einsum("spij,pq->spiqj", w, eye)
    return out.reshape(g // per_slab, per_slab * n, per_slab * n)


def _lru_branch(proj, b, s, conv_w, conv_b, w_a, b_a, w_i, b_i, lam, *, blk=512):
    nblk = s // blk
    per_slab = 4
    wa = _block_diag_slabs(w_a, per_slab).astype(BF16)
    wi = _block_diag_slabs(w_i, per_slab).astype(BF16)
    row = lambda i, j: i * nblk + j
    const = lambda i, j: (0, 0)
    const3 = lambda i, j: (0, 0, 0)
    vec = lambda v: v.reshape(1, WIDTH)
    return pl.pallas_call(
        functools.partial(_lru_kernel, blk=blk),
        grid=(b, nblk),
        in_specs=[pl.BlockSpec((blk, WIDTH), lambda i, j: (row(i, j), _col_block("lgate"))),
                  pl.BlockSpec((blk, WIDTH), lambda i, j: (row(i, j), _col_block("lx"))),
                  pl.BlockSpec((LRU_CONV, WIDTH), const),
                  pl.BlockSpec((1, WIDTH), const),
                  pl.BlockSpec(wa.shape, const3),
                  pl.BlockSpec((1, WIDTH), const),
                  pl.BlockSpec(wi.shape, const3),
                  pl.BlockSpec((1, WIDTH), const),
                  pl.BlockSpec((1, WIDTH), const)],
        out_specs=pl.BlockSpec((blk, WIDTH), lambda i, j: (row(i, j), 0)),
        out_shape=jax.ShapeDtypeStruct((b * s, WIDTH), BF16),
        scratch_shapes=[pltpu.VMEM((CONV_HALO + blk, WIDTH), BF16),
                        pltpu.VMEM((blk, WIDTH), F32),
                        pltpu.VMEM((blk, WIDTH), F32),
                        pltpu.VMEM((SUBLANE, WIDTH), F32)],
        compiler_params=_compiler_params(("parallel", "arbitrary")),
        name="rglru_branch",
    )(proj, proj, conv_w, vec(conv_b), wa, vec(b_a), wi, vec(b_i), vec(lam))


def _ret_kernel(q_ref, k_ref, v_ref, g_ref, cos_ref, sin_ref, inner_ref, kte_ref, qfs_ref, cdec_ref,
                nw_ref, o_ref, state, ybuf, *, blk):
    @pl.when(pl.program_id(1) == 0)
    def _():
        state[...] = jnp.zeros_like(state)

    lane = lax.broadcasted_iota(jnp.int32, (CHUNK, RET_QK), 1)
    first_half = (lane % RET_QK_DIM) < (RET_QK_DIM // 2)
    half = RET_QK_DIM // 2

    def rotary(x, cos, sin):
        swapped = jnp.where(first_half, pltpu.roll(x, RET_QK - half, 1), pltpu.roll(x, half, 1))
        return x * cos + swapped * sin

    left = lax.broadcasted_iota(jnp.int32, (CHUNK, LANE), 1) < RET_QK_DIM
    npair = RET_HEADS // 2

    def halves(x):
        return jnp.concatenate([jnp.where(left, x, 0.0), jnp.where(left, 0.0, x)], axis=0)

    def chunk(c, carry):
        rows = slice(c * CHUNK, (c + 1) * CHUNK)
        cos = cos_ref[rows, :]
        sin = sin_ref[rows, :]
        q = rotary(q_ref[rows, :].astype(F32), cos, sin)
        k = rotary(k_ref[rows, :].astype(F32), cos, sin) * (RET_QK_DIM ** -0.5)
        k_end = k * kte_ref[...]
        q_start = q * qfs_ref[...]
        slabs = [slice(p * LANE, (p + 1) * LANE) for p in range(npair)]
        v2 = [v_ref[rows, 2 * p * RET_V_DIM:(2 * p + 2) * RET_V_DIM] for p in range(npair)]
        v2 = [jnp.concatenate([v[:, 0:RET_V_DIM], v[:, RET_V_DIM:]], axis=0) for v in v2]
        prev = [state[p] for p in range(npair)]
        s2 = [_bdot_nt(q[:, ps], halves(k[:, ps])) * inner_ref[p] for p, ps in enumerate(slabs)]
        kv = [_bdot_tn(halves(k_end[:, ps]), v2[p]) for p, ps in enumerate(slabs)]
        for p, ps in enumerate(slabs):
            rhs = jnp.concatenate([v2[p], prev[p]], axis=0)
            for e in range(2):
                keep = left if e == 0 else jnp.logical_not(left)
                lhs = jnp.concatenate([jnp.where(keep, s2[p], 0.0), jnp.where(keep, q_start[:, ps], 0.0)],
                                      axis=1)
                y = _bdot(lhs, rhs)
                mu = jnp.mean(y, axis=-1, keepdims=True)
                yc = y - mu
                var = jnp.mean(yc * yc, axis=-1, keepdims=True)
                h = 2 * p + e
                ybuf[rows, h * RET_V_DIM:(h + 1) * RET_V_DIM] = yc * lax.rsqrt(var + EPS)
            state[p] = prev[p] * cdec_ref[p] + kv[p]
        return carry

    for c in range(blk // CHUNK):
        chunk(c, 0)
        rows = slice(c * CHUNK, (c + 1) * CHUNK)
        o_ref[rows, :] = (_silu(g_ref[rows, :].astype(F32)) * (ybuf[rows, :] * nw_ref[...])).astype(o_ref.dtype)


def _ret_tables(s):
    pos_s = jnp.arange(s, dtype=F32)
    inv_freq = ROPE_BASE ** (-jnp.arange(0, RET_QK_DIM, 2, dtype=F32) / RET_QK_DIM)
    ang = pos_s[:, None] * inv_freq[None, :]
    cos, sin = lax.optimization_barrier((jnp.cos(ang), jnp.sin(ang)))
    cos_f = jnp.tile(jnp.concatenate([cos, cos], axis=1), (1, RET_HEADS))
    sin_f = jnp.tile(jnp.concatenate([-sin, sin], axis=1), (1, RET_HEADS))
    log_gamma = jnp.log1p(-jnp.exp2(-5.0 - jnp.arange(RET_HEADS, dtype=F32)))
    pos = jnp.arange(CHUNK, dtype=F32)
    inner = jnp.exp(log_gamma[:, None, None] * jnp.abs(pos[:, None] - pos[None, :]))
    k_to_end = jnp.exp(log_gamma[:, None] * (CHUNK - 1.0 - pos))
    q_from_start = jnp.exp(log_gamma[:, None] * (pos + 1.0))
    kte = jnp.repeat(k_to_end.T, RET_QK_DIM, axis=1)
    qfs = jnp.repeat(q_from_start.T, RET_QK_DIM, axis=1)
    inner2 = inner.reshape(RET_HEADS // 2, 2, CHUNK, CHUNK).transpose(0, 2, 1, 3).reshape(
        RET_HEADS // 2, CHUNK, 2 * CHUNK)
    cdec = jnp.broadcast_to(jnp.repeat(jnp.exp(log_gamma * CHUNK), RET_QK_DIM).reshape(
        RET_HEADS // 2, 2 * RET_QK_DIM, 1), (RET_HEADS // 2, 2 * RET_QK_DIM, RET_V_DIM))
    return cos_f, sin_f, inner2, kte, qfs, cdec


def _ret_branch(proj, b, s, norm_w, tables, *, blk=256):
    nblk = s // blk
    cos_f, sin_f, inner, kte, qfs, cdec = tables
    row = lambda i, j: i * nblk + j
    const = lambda i, j: (0, 0)
    return pl.pallas_call(
        functools.partial(_ret_kernel, blk=blk),
        grid=(b, nblk),
        in_specs=[pl.BlockSpec((blk, RET_QK), lambda i, j: (row(i, j), _col_block("rq"))),
                  pl.BlockSpec((blk, RET_QK), lambda i, j: (row(i, j), _col_block("rk"))),
                  pl.BlockSpec((blk, WIDTH), lambda i, j: (row(i, j), _col_block("rv"))),
                  pl.BlockSpec((blk, WIDTH), lambda i, j: (row(i, j), _col_block("rg"))),
                  pl.BlockSpec((blk, RET_QK), lambda i, j: (j, 0)),
                  pl.BlockSpec((blk, RET_QK), lambda i, j: (j, 0)),
                  pl.BlockSpec(inner.shape, lambda i, j: (0, 0, 0)),
                  pl.BlockSpec((CHUNK, RET_QK), const),
                  pl.BlockSpec((CHUNK, RET_QK), const),
                  pl.BlockSpec(cdec.shape, lambda i, j: (0, 0, 0)),
                  pl.BlockSpec((1, WIDTH), const)],
        out_specs=pl.BlockSpec((blk, WIDTH), lambda i, j: (row(i, j), 0)),
        out_shape=jax.ShapeDtypeStruct((b * s, WIDTH), BF16),
        scratch_shapes=[pltpu.VMEM((RET_HEADS // 2, 2 * RET_QK_DIM, RET_V_DIM), F32),
                        pltpu.VMEM((blk, WIDTH), F32)],
        compiler_params=_compiler_params(("parallel", "arbitrary")),
        name="retention_branch",
    )(proj, proj, proj, proj, cos_f, sin_f, inner, kte, qfs, cdec, norm_w.reshape(1, -1))


def _rwkv_kernel(r_ref, k_ref, v_ref, l_ref, mur_ref, muk_ref, muv_ref, mul_ref, w0_ref, w2_ref,
                 a0_ref, a2_ref, g2_ref, kk_ref, ka_ref, rk_ref, lnw_ref, lnb_ref, o_ref,
                 tail, tail_l, r_s, k_s, v_s, kk_s, b_s, lw_s, o_s, ub_s, op_s, gate_s, bon_s, wall_s, state,
                 *, blk, hw):
    nh = hw // RWKV_HEAD

    @pl.when(pl.program_id(2) == 0)
    def _():
        tail[...] = jnp.zeros_like(tail)
        tail_l[...] = jnp.zeros_like(tail_l)
        state[...] = jnp.zeros_like(state)

    group = A_UNROLL * CHUNK
    ngroups = blk // group
    rows = lax.broadcasted_iota(jnp.int32, (group, 1), 0)

    def shift_mix(x, prev_row, mu):
        prev = jnp.where(rows == 0, prev_row, pltpu.roll(x, 1, 0))
        return x + (prev - x) * mu

    lane = lax.broadcasted_iota(jnp.int32, (LANE, LANE), 1) // RWKV_HEAD
    lrow = lax.broadcasted_iota(jnp.int32, (LANE, LANE), 0) // RWKV_HEAD
    head_ones = (lane == lrow).astype(BF16)

    def head_sum(x):
        hi = x.astype(BF16)
        lo = (x - hi.astype(F32)).astype(BF16)
        return jnp.concatenate(
            [jnp.dot(hi[:, q:q + LANE], head_ones, preferred_element_type=F32)
             + jnp.dot(lo[:, q:q + LANE], head_ones, preferred_element_type=F32)
             for q in range(0, hw, LANE)], axis=1)

    def prep(gi, last):
        rs = slice(gi * group, (gi + 1) * group)
        r_raw, k_raw, v_raw, l_raw = (ref[rs, :].astype(F32) for ref in (r_ref, k_ref, v_ref, l_ref))
        r = shift_mix(r_raw, last[0], mur_ref[...])
        k = shift_mix(k_raw, last[1], muk_ref[...])
        v = shift_mix(v_raw, last[2], muv_ref[...])
        lo_ = shift_mix(l_raw, last[3], mul_ref[...])
        wl = lo_[:, 0:RWKV_W_LORA]
        al = lo_[:, RWKV_W_LORA:RWKV_W_LORA + RWKV_A_LORA]
        gl = lo_[:, RWKV_W_LORA + RWKV_A_LORA:]
        wz = w0_ref[...] + _bdot(jnp.tanh(wl), w2_ref[...])
        w = jnp.minimum(wz, 0.0) - jnp.log(1.0 + jnp.exp(-jnp.abs(wz))) - 0.5
        a = jax.nn.sigmoid(a0_ref[...] + _bdot(al, a2_ref[...]))
        gate_s[rs, :] = _bdot(jax.nn.sigmoid(gl), g2_ref[...])
        kk = k * kk_ref[...]
        kk = kk * lax.rsqrt(jnp.maximum(head_sum(kk * kk), 1e-24))
        k = k * (1.0 + (a - 1.0) * ka_ref[...])
        bon_s[rs, :] = head_sum(r * k * rk_ref[...]) * v
        r_s[rs, :] = r
        k_s[rs, :] = k
        v_s[rs, :] = v
        kk_s[rs, :] = kk
        b_s[rs, :] = kk * a
        lw_s[rs, :] = -jnp.exp(w)
        return [x[group - 1:group, :] for x in (r_raw, k_raw, v_raw, l_raw)]

    last = [tail[0:1, :], tail[1:2, :], tail[2:3, :], tail_l[0:1, :]]
    for gi in range(ngroups):
        last = prep(gi, last)
    tail[0:1, :], tail[1:2, :], tail[2:3, :], tail_l[0:1, :] = last

    trow = lax.broadcasted_iota(jnp.int32, (CHUNK, 2 * CHUNK), 0)
    tcol = lax.broadcasted_iota(jnp.int32, (CHUNK, 2 * CHUNK), 1) % CHUNK
    strict = tcol < trow
    incl = tcol <= trow
    left = lax.broadcasted_iota(jnp.int32, (CHUNK, LANE), 1) < RWKV_HEAD
    left2 = lax.broadcasted_iota(jnp.int32, (2 * CHUNK, LANE), 1) < RWKV_HEAD

    eye2 = (tcol == trow).astype(F32)
    zeros = jnp.zeros((CHUNK, LANE), F32)
    slabs = [slice(p * LANE, (p + 1) * LANE) for p in range(nh // 2)]
    lcat = lambda a, b: jnp.concatenate([a, b], axis=1)
    rcat = lambda *a: jnp.concatenate(a, axis=0)
    keep_l = lambda a: jnp.where(left, a, 0.0)
    keep_r = lambda a: jnp.where(left, 0.0, a)

    def phase_a(g):
        cs = [g * A_UNROLL + i for i in range(A_UNROLL)]
        sls = [slice(c * CHUNK, (c + 1) * CHUNK) for c in cs]
        pre = []
        for sl in sls:
            lw = lw_s[sl, :]
            cum = _cumsum_rows(lw)
            cum_last = cum[CHUNK - 1:CHUNK, :]
            w_inv = jnp.exp(-cum)
            to_end = jnp.exp(cum_last - cum)
            pre.append(dict(rt=r_s[sl, :] * jnp.exp(cum), kt=k_s[sl, :] * w_inv, bt=b_s[sl, :] * w_inv,
                            kap=kk_s[sl, :] * jnp.exp(cum - lw), k_end=k_s[sl, :] * to_end,
                            b_end=b_s[sl, :] * to_end, w_all=jnp.exp(cum_last), vv=v_s[sl, :]))
        items = [(i, ps) for i in range(A_UNROLL) for ps in slabs]
        pm2 = []
        for i, ps in items:
            d = pre[i]
            lhs = rcat(d["kap"][:, ps], d["rt"][:, ps])
            rhs_e = jnp.where(left2, rcat(d["bt"][:, ps], d["kt"][:, ps]), 0.0)
            rhs_o = jnp.where(left2, 0.0, rcat(d["kt"][:, ps], d["bt"][:, ps]))
            pm2.append(_bdot_nt(lhs, rcat(rhs_e, rhs_o)))
        top = [(jnp.where(strict, m[0:CHUNK, 0:LANE], 0.0), jnp.where(strict, m[0:CHUNK, LANE:], 0.0))
               for m in pm2]
        bot = [(jnp.where(incl, m[CHUNK:, 0:LANE], 0.0), jnp.where(incl, m[CHUNK:, LANE:], 0.0))
               for m in pm2]
        v_swp = [pltpu.roll(pre[i]["vv"][:, ps], RWKV_HEAD, 1) for i, ps in items]
        v_oe = [rcat(keep_l(v), keep_r(v)) for v in v_swp]
        av = [_bdot(jnp.where(left, to, te), v_oe[n]) for n, (te, to) in enumerate(top)]
        op = [_bdot(jnp.where(left, bo, be), v_oe[n]) for n, (be, bo) in enumerate(bot)]
        z = [(jnp.where(left, -te, eye2), jnp.where(left, eye2, -to)) for te, to in top]

        def step(ze, zo):
            return _bdot(jnp.where(left, ze, zo), rcat(lcat(ze, zeros), lcat(zeros, zo)))

        span = 1
        while 2 * span < CHUNK:
            res = [step(ze, zo) for ze, zo in z]
            z = [(r[:, 0:LANE] + keep_r(ze), r[:, LANE:] + keep_l(zo)) for r, (ze, zo) in zip(res, z)]
            span *= 2
        res = [step(ze, zo) for ze, zo in z]
        t_oe = [jnp.where(left, r[:, LANE:] + zo, r[:, 0:LANE] + ze) for r, (ze, zo) in zip(res, z)]
        for n, (i, ps) in enumerate(items):
            kap = pre[i]["kap"][:, ps]
            prod = _bdot(t_oe[n], lcat(rcat(keep_l(av[n]), keep_r(av[n])), rcat(keep_r(kap), keep_l(kap))))
            sl = sls[i]
            lw_s[sl, ps] = prod[:, 0:LANE]
            kk_s[sl, ps] = prod[:, LANE:]
            ub_s[sl, ps] = jnp.where(left, bot[n][0], bot[n][1])
            op_s[sl, ps] = op[n]
            v_s[sl, ps] = v_swp[n]
        for i, sl in enumerate(sls):
            r_s[sl, :] = pre[i]["rt"]
            k_s[sl, :] = pre[i]["k_end"]
            b_s[sl, :] = pre[i]["b_end"]
            wall_s[cs[i]] = jnp.broadcast_to(pre[i]["w_all"], (SUBLANE, hw))

    def phase_b(c):
        sl = slice(c * CHUNK, (c + 1) * CHUNK)
        w_all = wall_s[c]
        s0 = [(state[2 * p], state[2 * p + 1]) for p in range(nh // 2)]
        s_oe = [rcat(so, se) for se, so in s0]
        u = [-(lw_s[sl, ps] + _bdot_nt(kk_s[sl, ps], s_oe[p])) for p, ps in enumerate(slabs)]
        op = [op_s[sl, ps] + _bdot_nt(r_s[sl, ps], s_oe[p]) for p, ps in enumerate(slabs)]
        for p, ps in enumerate(slabs):
            o_sw = _bdot(ub_s[sl, ps], rcat(keep_r(u[p]), keep_l(u[p]))) + op[p]
            o_s[sl, ps] = pltpu.roll(o_sw, RWKV_HEAD, 1)
        for p, ps in enumerate(slabs):
            add = _bdot_tn(rcat(u[p], v_s[sl, ps]), rcat(b_s[sl, ps], k_s[sl, ps]))
            state[2 * p] = s0[p][0] * w_all[0:1, ps] + keep_l(add[RWKV_HEAD:, :])
            state[2 * p + 1] = s0[p][1] * w_all[0:1, ps] + keep_r(add[0:RWKV_HEAD, :])

    def finish(gi):
        rs = slice(gi * group, (gi + 1) * group)
        o = o_s[rs, :]
        mu = head_sum(o) * (1.0 / RWKV_HEAD)
        oc = o - mu
        var = head_sum(oc * oc) * (1.0 / RWKV_HEAD)
        o = oc * lax.rsqrt(var + RWKV_LN_EPS) * lnw_ref[...] + lnb_ref[...]
        o_ref[rs, :] = ((o + bon_s[rs, :]) * gate_s[rs, :]).astype(o_ref.dtype)

    for g in range(ngroups):
        phase_a(g)
    for g in range(ngroups):
        for c in range(g * A_UNROLL, (g + 1) * A_UNROLL):
            phase_b(c)
        finish(g)


def _rwkv_branch(proj, b, s, mu, w0, w2, a0, a2, g2, k_k, k_a, r_k, ln_w, ln_b, *, blk=512, hw=WIDTH):
    nblk = s // blk
    nslab = WIDTH // hw
    row = lambda i, p, j: i * nblk + j
    vec = lambda v: v.reshape(1, -1)
    hcol = lambda i, p, j: (0, p)
    mu_r, mu_k, mu_v, mu_l = (mu[0:WIDTH], mu[WIDTH:2 * WIDTH], mu[2 * WIDTH:3 * WIDTH],
                              mu[3 * WIDTH:])
    wblk = _col_block("wr")
    lblk = _col_block("lora")
    return pl.pallas_call(
        functools.partial(_rwkv_kernel, blk=blk, hw=hw),
        grid=(b, nslab, nblk),
        in_specs=[pl.BlockSpec((blk, hw), lambda i, p, j: (row(i, p, j), wblk * nslab + p)),
                  pl.BlockSpec((blk, hw), lambda i, p, j: (row(i, p, j), (wblk + 1) * nslab + p)),
                  pl.BlockSpec((blk, hw), lambda i, p, j: (row(i, p, j), (wblk + 2) * nslab + p)),
                  pl.BlockSpec((blk, RWKV_LORA), lambda i, p, j: (row(i, p, j), lblk)),
                  pl.BlockSpec((1, hw), hcol),
                  pl.BlockSpec((1, hw), hcol),
                  pl.BlockSpec((1, hw), hcol),
                  pl.BlockSpec((1, RWKV_LORA), lambda i, p, j: (0, 0)),
                  pl.BlockSpec((1, hw), hcol),
                  pl.BlockSpec((RWKV_W_LORA, hw), hcol),
                  pl.BlockSpec((1, hw), hcol),
                  pl.BlockSpec((RWKV_A_LORA, hw), hcol),
                  pl.BlockSpec((RWKV_G_LORA, hw), hcol),
                  pl.BlockSpec((1, hw), hcol),
                  pl.BlockSpec((1, hw), hcol),
                  pl.BlockSpec((1, hw), hcol),
                  pl.BlockSpec((1, hw), hcol),
                  pl.BlockSpec((1, hw), hcol)],
        out_specs=pl.BlockSpec((blk, hw), lambda i, p, j: (row(i, p, j), p)),
        out_shape=jax.ShapeDtypeStruct((b * s, WIDTH), BF16),
        scratch_shapes=[pltpu.VMEM((SUBLANE, hw), F32), pltpu.VMEM((SUBLANE, RWKV_LORA), F32)]
        + [pltpu.VMEM((blk, hw), F32)] * 11
        + [pltpu.VMEM((blk // CHUNK, SUBLANE, hw), F32)]
        + [pltpu.VMEM((hw // RWKV_HEAD, RWKV_HEAD, LANE), F32)],
        compiler_params=_compiler_params(("parallel", "parallel", "arbitrary")),
        name="rwkv7_branch",
    )(proj, proj, proj, proj, vec(mu_r), vec(mu_k), vec(mu_v), vec(mu_l), vec(w0), w2.astype(BF16),
      vec(a0), a2.astype(BF16), g2.astype(BF16), vec(k_k), vec(k_a), vec(r_k), vec(ln_w), vec(ln_b))


def _merge_kernel(y0_ref, y1_ref, y2_ref, y3_ref, gl_ref, x_ref, wb_ref, wo_ref, o_ref):
    merged = None
    for m, y_ref in enumerate((y0_ref, y1_ref, y2_ref, y3_ref)):
        br = jnp.dot(y_ref[...], wb_ref[m], preferred_element_type=F32)
        term = jax.nn.sigmoid(gl_ref[:, m * D_MODEL:(m + 1) * D_MODEL].astype(F32)) * br
        merged = term if merged is None else merged + term
    o_ref[...] = x_ref[...] + jnp.dot(merged.astype(BF16), wo_ref[...], preferred_element_type=F32)


def _merge(ys, proj, x2, w_branch, w_out, layer, *, tm=512):
    t = x2.shape[0]
    tok = lambda i: (i, 0)
    return pl.pallas_call(
        _merge_kernel,
        grid=(t // tm,),
        in_specs=[pl.BlockSpec((tm, WIDTH), tok)] * N_BRANCH
        + [pl.BlockSpec((tm, N_BRANCH * D_MODEL), lambda i: (i, _col_block("gates"))),
           pl.BlockSpec((tm, D_MODEL), tok),
           pl.BlockSpec((None, N_BRANCH, WIDTH, D_MODEL), lambda i: (layer, 0, 0, 0)),
           pl.BlockSpec((None, D_MODEL, D_MODEL), lambda i: (layer, 0, 0))],
        out_specs=pl.BlockSpec((tm, D_MODEL), tok),
        out_shape=jax.ShapeDtypeStruct((t, D_MODEL), F32),
        compiler_params=_compiler_params(("parallel",)),
        name="branch_merge",
    )(*ys, proj, x2, w_branch, w_out)


FFN_TC = 256
FFN_AHEAD = 3


def _ffn_kernel(x_ref, g_ref, wu_ref, cw_ref, cb_ref, wd_ref, gf_ref, o_ref, cbuf, tail, *, tm, final):
    @pl.when(pl.program_id(1) == 0)
    def _():
        tail[...] = jnp.zeros_like(tail)

    x = x_ref[...]
    ms = jnp.mean(x * x, axis=-1, keepdims=True)
    hn = (x * lax.rsqrt(ms + EPS) * g_ref[...]).astype(BF16)
    nsteps = D_FF // FFN_TC

    def up_conv(i):
        halves = []
        for half in range(2):
            c0 = half * D_FF + i * FFN_TC
            cs = slice(c0, c0 + FFN_TC)
            buf = cbuf.at[2 * (i % (FFN_AHEAD + 1)) + half]
            buf[0:SUBLANE, :] = tail[:, cs]
            buf[SUBLANE:SUBLANE + tm, :] = jnp.dot(hn, wu_ref[:, cs], preferred_element_type=F32)
            y = cb_ref[:, cs]
            for k in range(FFN_CONV):
                s = FFN_CONV - 1 - k
                y = y + cw_ref[k:k + 1, cs] * buf[SUBLANE - s:SUBLANE - s + tm, :]
            tail[:, cs] = buf[tm:tm + SUBLANE, :]
            halves.append(y)
        return halves

    acc = x
    ups = [up_conv(i) for i in range(FFN_AHEAD)]
    for i in range(nsteps):
        if i + FFN_AHEAD < nsteps:
            ups.append(up_conv(i + FFN_AHEAD))
        val, gt = ups[i]
        acc = acc + jnp.dot((_silu(gt) * val).astype(BF16), wd_ref[i * FFN_TC:(i + 1) * FFN_TC, :],
                            preferred_element_type=F32)
    if final:
        ms = jnp.mean(acc * acc, axis=-1, keepdims=True)
        acc = acc * lax.rsqrt(ms + EPS) * gf_ref[...]
    o_ref[...] = acc


def _conv_ffn(x2, b, s, gain, w_up, conv_w, conv_b, w_down, gain_final, layer, *, final, tm=512):
    nblk = s // tm
    row = lambda i, j: (i * nblk + j, 0)
    const = lambda i, j: (0, 0)
    return pl.pallas_call(
        functools.partial(_ffn_kernel, tm=tm, final=final),
        grid=(b, nblk),
        in_specs=[pl.BlockSpec((tm, D_MODEL), row),
                  pl.BlockSpec((1, D_MODEL), const),
                  pl.BlockSpec((None, D_MODEL, 2 * D_FF), lambda i, j: (layer, 0, 0)),
                  pl.BlockSpec((FFN_CONV, 2 * D_FF), const),
                  pl.BlockSpec((1, 2 * D_FF), const),
                  pl.BlockSpec((None, D_FF, D_MODEL), lambda i, j: (layer, 0, 0)),
                  pl.BlockSpec((1, D_MODEL), const)],
        out_specs=pl.BlockSpec((tm, D_MODEL), row),
        out_shape=jax.ShapeDtypeStruct((b * s, D_MODEL), F32),
        scratch_shapes=[pltpu.VMEM((2 * (FFN_AHEAD + 1), tm + 2 * SUBLANE, FFN_TC), F32),
                        pltpu.VMEM((SUBLANE, 2 * D_FF), F32)],
        compiler_params=_compiler_params(("parallel", "arbitrary")),
        name="conv_ffn",
    )(x2, gain.reshape(1, -1), w_up, conv_w, conv_b.reshape(1, -1), w_down, gain_final.reshape(1, -1))


def _reorder_w_in(w):
    cols = []
    for n in _DST_ORDER:
        off, wd = _SRC[n]
        cols.append(w[..., off:off + wd].astype(BF16))
    pad = N_PROJ - _DST["dt"][0] - SSM_HEADS
    cols.append(jnp.zeros(w.shape[:-1] + (pad,), BF16))
    return jnp.concatenate(cols, axis=-1)


def kernel(x, norm_mix, w_in, ssm_conv_w, ssm_conv_b, ssm_dt_bias, ssm_a_log, ssm_d, ssm_norm, lru_conv_w, lru_conv_b, lru_w_a, lru_b_a, lru_w_i, lru_b_i, lru_lam, ret_norm, rwkv_mu, rwkv_w0, rwkv_w2, rwkv_a0, rwkv_a2, rwkv_g2, rwkv_k_k, rwkv_k_a, rwkv_r_k, rwkv_ln_w, rwkv_ln_b, w_branch, w_out, norm_ffn, ffn_up, ffn_conv_w, ffn_conv_b, ffn_down, norm_final):
    b, s, d = x.shape
    x2 = x.reshape(b * s, d)
    tables = _ret_tables(s)
    w_branch_b, w_out_b = _cast_bf16(w_branch), _cast_bf16(w_out)
    ffn_up_b, ffn_down_b = _cast_bf16(ffn_up), _cast_bf16(ffn_down)
    for l in range(DEPTH):
        proj = _norm_matmul(x2, norm_mix[l], _reorder_w_in(w_in[l])[None], 0)
        y_ssd = _ssd_branch(proj, b, s, ssm_conv_w[l], ssm_conv_b[l], ssm_dt_bias[l], ssm_a_log[l],
                            ssm_d[l], ssm_norm[l])
        y_lru = _lru_branch(proj, b, s, lru_conv_w[l], lru_conv_b[l], lru_w_a[l], lru_b_a[l],
                            lru_w_i[l], lru_b_i[l], lru_lam[l])
        y_ret = _ret_branch(proj, b, s, ret_norm[l], tables)
        y_rwkv = _rwkv_branch(proj, b, s, rwkv_mu[l], rwkv_w0[l], rwkv_w2[l], rwkv_a0[l], rwkv_a2[l],
                              rwkv_g2[l], rwkv_k_k[l], rwkv_k_a[l], rwkv_r_k[l], rwkv_ln_w[l],
                              rwkv_ln_b[l])
        x2 = _merge((y_ssd, y_lru, y_ret, y_rwkv), proj, x2, w_branch_b, w_out_b, l)
        x2 = _conv_ffn(x2, b, s, norm_ffn[l], ffn_up_b, ffn_conv_w[l], ffn_conv_b[l], ffn_down_b,
                       norm_final, l, final=(l == DEPTH - 1))
    return x2.reshape(b, s, d)
```

```python
import functools

import jax
import jax.numpy as jnp
from jax import lax
from jax.experimental import pallas as pl
from jax.experimental.pallas import tpu as pltpu

F32 = jnp.float32
BF16 = jnp.bfloat16

D_MODEL = 1024
WIDTH = 1024
DEPTH = 2
CHUNK = 64
EPS = 1e-6

SSM_HEADS = 16
SSM_HEAD_DIM = 64
SSM_GROUPS = 4
SSM_STATE = 128
SSM_CONV = 4
SSM_XBC = WIDTH + 2 * SSM_GROUPS * SSM_STATE

LRU_BLOCKS = 16
LRU_BLOCK = 64
LRU_CONV = 4
LRU_C = 8.0

RET_HEADS = 8
RET_QK_DIM = 64
RET_V_DIM = 128
RET_QK = RET_HEADS * RET_QK_DIM
ROPE_BASE = 10000.0

RWKV_HEAD = 64
RWKV_HEADS = 16
RWKV_W_LORA = 64
RWKV_A_LORA = 64
RWKV_G_LORA = 128
RWKV_LORA = RWKV_W_LORA + RWKV_A_LORA + RWKV_G_LORA
RWKV_LN_EPS = 64e-5
A_UNROLL = 2

D_FF = 2816
FFN_CONV = 3
N_BRANCH = 4

LANE = 128
SUBLANE = 8
VMEM_LIMIT = 56 * 1024 * 1024

_SRC = {}
_o = 0
for _n, _w in (("z", WIDTH), ("xbc", SSM_XBC), ("dt", SSM_HEADS), ("lgate", WIDTH), ("lx", WIDTH),
               ("rq", RET_QK), ("rk", RET_QK), ("rv", WIDTH), ("rg", WIDTH),
               ("wr", WIDTH), ("wk", WIDTH), ("wv", WIDTH), ("lora", RWKV_LORA),
               ("gates", N_BRANCH * D_MODEL)):
    _SRC[_n] = (_o, _w)
    _o += _w
N_IN = _o
_DST_ORDER = ("gates", "xbc", "z", "lgate", "lx", "rv", "rg", "rq", "rk", "wr", "wk", "wv", "lora", "dt")
_DST = {}
_o = 0
for _n in _DST_ORDER:
    _w = _SRC[_n][1]
    _bw = max(_w, LANE)
    assert _o % _bw == 0, (_n, _o, _bw)
    _DST[_n] = (_o, _bw)
    _o += _bw
PROJ_TN = 2048
N_PROJ = -(-_o // PROJ_TN) * PROJ_TN


def _col_block(name):
    off, bw = _DST[name]
    return off // bw


def _softplus(x):
    return jnp.maximum(x, 0.0) + jnp.log1p(jnp.exp(-jnp.abs(x)))


def _silu(x):
    return x * jax.nn.sigmoid(x)


def _bdot(a, b):
    return jnp.dot(a.astype(BF16), b.astype(BF16), preferred_element_type=F32)


def _bdot_nt(a, b):
    return lax.dot_general(a.astype(BF16), b.astype(BF16), (((1,), (1,)), ((), ())),
                           preferred_element_type=F32)


def _bdot_tn(a, b):
    return lax.dot_general(a.astype(BF16), b.astype(BF16), (((0,), (0,)), ((), ())),
                           preferred_element_type=F32)


def _cumsum_rows(x):
    n = x.shape[0]
    row = lax.broadcasted_iota(jnp.int32, (n, n), 0)
    col = lax.broadcasted_iota(jnp.int32, (n, n), 1)
    tri = (col <= row).astype(BF16)
    hi = x.astype(BF16)
    rest = x - hi.astype(F32)
    mid = rest.astype(BF16)
    lo = (rest - mid.astype(F32)).astype(BF16)
    return (jnp.dot(tri, hi, preferred_element_type=F32) + jnp.dot(tri, mid, preferred_element_type=F32)
            + jnp.dot(tri, lo, preferred_element_type=F32))


CONV_HALO = CHUNK


def _shift_matrix(nshift):
    r = lax.broadcasted_iota(jnp.int32, (nshift * CHUNK, CONV_HALO + CHUNK), 0)
    c = lax.broadcasted_iota(jnp.int32, (nshift * CHUNK, CONV_HALO + CHUNK), 1)
    return (c == CONV_HALO + r % CHUNK - (r // CHUNK + 1)).astype(BF16)


def _causal_conv_chunk(xb, c, cw_ref, cb_ref, shifts):
    x2 = xb[c * CHUNK:c * CHUNK + CONV_HALO + CHUNK, :]
    sh = jnp.dot(shifts, x2, preferred_element_type=F32)
    k = cw_ref.shape[0]
    y = cb_ref[...] + cw_ref[k - 1:k, :] * x2[CONV_HALO:, :].astype(F32)
    for s in range(1, k):
        y = y + cw_ref[k - 1 - s:k - s, :] * sh[(s - 1) * CHUNK:s * CHUNK, :]
    return y


def _compiler_params(sem):
    return pltpu.CompilerParams(dimension_semantics=sem, vmem_limit_bytes=VMEM_LIMIT)


CAST_BLOCK_BYTES = 4 * 1024 * 1024


def _cast_kernel(x_ref, o_ref):
    o_ref[...] = x_ref[...].astype(o_ref.dtype)


def _cast_bf16(x):
    x2 = x.reshape(-1, x.shape[-1])
    r, c = x2.shape
    rows = max(n for n in range(16, r + 1, 16) if r % n == 0 and n * c * 4 <= CAST_BLOCK_BYTES)
    out = pl.pallas_call(
        _cast_kernel,
        grid=(r // rows,),
        in_specs=[pl.BlockSpec((rows, c), lambda i: (i, 0))],
        out_specs=pl.BlockSpec((rows, c), lambda i: (i, 0)),
        out_shape=jax.ShapeDtypeStruct((r, c), BF16),
        compiler_params=_compiler_params(("parallel",)),
        name="cast_bf16",
    )(x2)
    return out.reshape(x.shape)


def _norm_matmul_kernel(x_ref, g_ref, w_ref, o_ref, hn_ref):
    @pl.when(pl.program_id(1) == 0)
    def _():
        x = x_ref[...]
        ms = jnp.mean(x * x, axis=-1, keepdims=True)
        hn_ref[...] = (x * lax.rsqrt(ms + EPS) * g_ref[...]).astype(BF16)

    o_ref[...] = jnp.dot(hn_ref[...], w_ref[...], preferred_element_type=F32).astype(o_ref.dtype)


def _norm_matmul(x2, gain, w, layer, *, tm=2048, tn=PROJ_TN):
    t, d = x2.shape
    n = w.shape[2]
    return pl.pallas_call(
        _norm_matmul_kernel,
        grid=(t // tm, n // tn),
        in_specs=[pl.BlockSpec((tm, d), lambda i, j: (i, 0)),
                  pl.BlockSpec((1, d), lambda i, j: (0, 0)),
                  pl.BlockSpec((None, d, tn), lambda i, j: (layer, 0, j))],
        out_specs=pl.BlockSpec((tm, tn), lambda i, j: (i, j)),
        out_shape=jax.ShapeDtypeStruct((t, n), BF16),
        scratch_shapes=[pltpu.VMEM((tm, d), BF16)],
        compiler_params=_compiler_params(("parallel", "arbitrary")),
        name="norm_in_proj",
    )(x2, gain.reshape(1, d), w)


def _ssd_kernel(z_ref, xbc_ref, dt_ref, cw_ref, cb_ref, dtb_ref, alog_ref, dsk_ref, nw_ref, o_ref,
                cbuf, state, *, blk):
    @pl.when(pl.program_id(1) == 0)
    def _():
        cbuf[0:CONV_HALO, :] = jnp.zeros((CONV_HALO, SSM_XBC), BF16)
        state[...] = jnp.zeros_like(state)

    cbuf[CONV_HALO:CONV_HALO + blk, :] = xbc_ref[...]
    shifts = _shift_matrix(SSM_CONV - 1)

    a_neg = -jnp.exp(alog_ref[...])
    row = lax.broadcasted_iota(jnp.int32, (CHUNK, LANE), 0)
    col = lax.broadcasted_iota(jnp.int32, (CHUNK, LANE), 1) % CHUNK
    causal = col <= row
    diag = col == row
    left = lax.broadcasted_iota(jnp.int32, (CHUNK, LANE), 1) < SSM_HEAD_DIM
    head_of = lax.broadcasted_iota(jnp.int32, (LANE, WIDTH), 1) // SSM_HEAD_DIM
    spread = (head_of == lax.broadcasted_iota(jnp.int32, (LANE, WIDTH), 0)).astype(BF16)
    npair = SSM_HEADS // 2
    pair_group = (SSM_HEADS // SSM_GROUPS) // 2
    slabs = [slice(p * LANE, (p + 1) * LANE) for p in range(npair)]

    def per_channel(x):
        hi = x.astype(BF16)
        lo = (x - hi.astype(F32)).astype(BF16)
        return (jnp.dot(hi, spread, preferred_element_type=F32)
                + jnp.dot(lo, spread, preferred_element_type=F32))

    def local(c):
        r0 = c * CHUNK
        act = _silu(_causal_conv_chunk(cbuf, c, cw_ref, cb_ref, shifts))
        dt = _softplus(dt_ref[r0:r0 + CHUNK, :].astype(F32) + dtb_ref[...])
        a_cum = _cumsum_rows(dt * a_neg)
        full = per_channel(jnp.concatenate([dt, a_cum], axis=0))
        dt_c, ac_c = full[0:CHUNK, :], full[CHUNK:, :]
        a_last = ac_c[CHUNK - 1:CHUNK, :]
        xs = act[:, 0:WIDTH]
        xdt = xs * dt_c
        xdt_end = xdt * jnp.exp(a_last - ac_c)
        bm = [act[:, WIDTH + g * SSM_STATE:WIDTH + (g + 1) * SSM_STATE] for g in range(SSM_GROUPS)]
        cm = [act[:, WIDTH + (SSM_GROUPS + g) * SSM_STATE:WIDTH + (SSM_GROUPS + g + 1) * SSM_STATE]
              for g in range(SSM_GROUPS)]
        cb2 = [_bdot_nt(cm[g], jnp.concatenate([bm[g], bm[g]], axis=0)) for g in range(SSM_GROUPS)]
        y_diag, s_new = [], []
        for p, ps in enumerate(slabs):
            g = p // pair_group
            seg_l = ac_c[:, ps]
            seg_s = jnp.sum(jnp.where(diag, seg_l, 0.0), axis=0, keepdims=True)
            m = cb2[g] * jnp.exp(jnp.where(causal, seg_l - seg_s, -jnp.inf))
            xd = xdt[:, ps]
            xd2 = jnp.concatenate([jnp.where(left, xd, 0.0), jnp.where(left, 0.0, xd)],
                                  axis=0)
            y_diag.append(_bdot(m, xd2) + xs[:, ps] * dsk_ref[:, ps])
            s_new.append(_bdot_tn(bm[g], xdt_end[:, ps]))
        return dict(y=y_diag, s=s_new, cm=cm, from_start=jnp.exp(ac_c), decay=jnp.exp(a_last))

    loc = [local(c) for c in range(blk // CHUNK)]
    cbuf[0:CONV_HALO, :] = cbuf[blk:blk + CONV_HALO, :]
    for c, d in enumerate(loc):
        ys = []
        for p, ps in enumerate(slabs):
            prev = state[p]
            ys.append(d["y"][p] + _bdot(d["cm"][p // pair_group], prev) * d["from_start"][:, ps])
            state[p] = prev * d["decay"][:, ps] + d["s"][p]
        y = jnp.concatenate(ys, axis=1) * _silu(z_ref[c * CHUNK:(c + 1) * CHUNK, :].astype(F32))
        ms = jnp.mean(y * y, axis=-1, keepdims=True)
        o_ref[c * CHUNK:(c + 1) * CHUNK, :] = (y * lax.rsqrt(ms + EPS) * nw_ref[...]).astype(o_ref.dtype)


def _ssd_branch(proj, b, s, conv_w, conv_b, dt_bias, a_log, d_skip, norm_w, *, blk=512):
    nblk = s // blk
    pad = lambda v: jnp.pad(v.reshape(1, -1), ((0, 0), (0, LANE - v.shape[0])))
    dsk = jnp.repeat(d_skip, SSM_HEAD_DIM).reshape(1, WIDTH)
    row = lambda i, j: i * nblk + j
    const = lambda i, j: (0, 0)
    return pl.pallas_call(
        functools.partial(_ssd_kernel, blk=blk),
        grid=(b, nblk),
        in_specs=[pl.BlockSpec((blk, WIDTH), lambda i, j: (row(i, j), _col_block("z"))),
                  pl.BlockSpec((blk, SSM_XBC), lambda i, j: (row(i, j), _col_block("xbc"))),
                  pl.BlockSpec((blk, LANE), lambda i, j: (row(i, j), _col_block("dt"))),
                  pl.BlockSpec((SSM_CONV, SSM_XBC), const),
                  pl.BlockSpec((1, SSM_XBC), const),
                  pl.BlockSpec((1, LANE), const),
                  pl.BlockSpec((1, LANE), const),
                  pl.BlockSpec((1, WIDTH), const),
                  pl.BlockSpec((1, WIDTH), const)],
        out_specs=pl.BlockSpec((blk, WIDTH), lambda i, j: (row(i, j), 0)),
        out_shape=jax.ShapeDtypeStruct((b * s, WIDTH), BF16),
        scratch_shapes=[pltpu.VMEM((CONV_HALO + blk, SSM_XBC), BF16),
                        pltpu.VMEM((SSM_HEADS // 2, SSM_STATE, LANE), F32)],
        compiler_params=_compiler_params(("parallel", "arbitrary")),
        name="ssd_branch",
    )(proj, proj, proj, conv_w, conv_b.reshape(1, -1), pad(dt_bias), pad(a_log), dsk,
      norm_w.reshape(1, -1))


def _lru_kernel(gate_ref, x_ref, cw_ref, cb_ref, wa_ref, ba_ref, wi_ref, bi_ref, lam_ref, o_ref,
                cbuf, a_s, u_s, carry_s, *, blk):
    @pl.when(pl.program_id(1) == 0)
    def _():
        cbuf[0:CONV_HALO, :] = jnp.zeros((CONV_HALO, WIDTH), BF16)
        carry_s[...] = jnp.zeros_like(carry_s)

    cbuf[CONV_HALO:CONV_HALO + blk, :] = x_ref[...]
    shifts = _shift_matrix(LRU_CONV - 1)
    xc = jnp.concatenate([_causal_conv_chunk(cbuf, c, cw_ref, cb_ref, shifts) for c in range(blk // CHUNK)],
                         axis=0)
    cbuf[0:CONV_HALO, :] = cbuf[blk:blk + CONV_HALO, :]

    nsl = wa_ref.shape[0]
    wsl = WIDTH // nsl
    xcb = xc.astype(BF16)
    ra = jnp.concatenate([jnp.dot(xcb[:, q * wsl:(q + 1) * wsl], wa_ref[q], preferred_element_type=F32)
                          for q in range(nsl)], axis=1)
    ri = jnp.concatenate([jnp.dot(xcb[:, q * wsl:(q + 1) * wsl], wi_ref[q], preferred_element_type=F32)
                          for q in range(nsl)], axis=1)
    r = jax.nn.sigmoid(ra + ba_ref[...])
    i = jax.nn.sigmoid(ri + bi_ref[...])
    log_a = -LRU_C * r * _softplus(-lam_ref[...])
    a = jnp.exp(log_a)
    a_s[...] = a
    u_s[...] = jnp.sqrt(-jnp.tanh(log_a) * (a * a + 1.0)) * (i * xc)

    rows = lax.broadcasted_iota(jnp.int32, (SUBLANE, WIDTH), 0)

    def group(g, carry):
        r0 = pl.multiple_of(g * SUBLANE, SUBLANE)
        a = a_s[pl.ds(r0, SUBLANE), :]
        u = u_s[pl.ds(r0, SUBLANE), :]
        for k in (1, 2, 4):
            keep = rows >= k
            a_prev = jnp.where(keep, pltpu.roll(a, k, 0), 1.0)
            u_prev = jnp.where(keep, pltpu.roll(u, k, 0), 0.0)
            u = a * u_prev + u
            a = a * a_prev
        h = a * carry + u
        u_s[pl.ds(r0, SUBLANE), :] = h
        return jnp.broadcast_to(h[SUBLANE - 1:SUBLANE, :], (SUBLANE, WIDTH))

    carry_s[...] = lax.fori_loop(0, blk // SUBLANE, group, carry_s[...], unroll=4)
    o_ref[...] = (u_s[...] * jax.nn.gelu(gate_ref[...].astype(F32), approximate=True)).astype(o_ref.dtype)


def _block_diag_slabs(w, per_slab):
    g, n, _ = w.shape
    w = w.reshape(g // per_slab, per_slab, n, n)
    eye = jnp.eye(per_slab, dtype=w.dtype)
    out = jnp.einsum("spij,pq->spiqj", w, eye)
    return out.reshape(g // per_slab, per_slab * n, per_slab * n)


def _lru_branch(proj, b, s, conv_w, conv_b, w_a, b_a, w_i, b_i, lam, *, blk=512):
    nblk = s // blk
    per_slab = 4
    wa = _block_diag_slabs(w_a, per_slab).astype(BF16)
    wi = _block_diag_slabs(w_i, per_slab).astype(BF16)
    row = lambda i, j: i * nblk + j
    const = lambda i, j: (0, 0)
    const3 = lambda i, j: (0, 0, 0)
    vec = lambda v: v.reshape(1, WIDTH)
    return pl.pallas_call(
        functools.partial(_lru_kernel, blk=blk),
        grid=(b, nblk),
        in_specs=[pl.BlockSpec((blk, WIDTH), lambda i, j: (row(i, j), _col_block("lgate"))),
                  pl.BlockSpec((blk, WIDTH), lambda i, j: (row(i, j), _col_block("lx"))),
                  pl.BlockSpec((LRU_CONV, WIDTH), const),
                  pl.BlockSpec((1, WIDTH), const),
                  pl.BlockSpec(wa.shape, const3),
                  pl.BlockSpec((1, WIDTH), const),
                  pl.BlockSpec(wi.shape, const3),
                  pl.BlockSpec((1, WIDTH), const),
                  pl.BlockSpec((1, WIDTH), const)],
        out_specs=pl.BlockSpec((blk, WIDTH), lambda i, j: (row(i, j), 0)),
        out_shape=jax.ShapeDtypeStruct((b * s, WIDTH), BF16),
        scratch_shapes=[pltpu.VMEM((CONV_HALO + blk, WIDTH), BF16),
                        pltpu.VMEM((blk, WIDTH), F32),
                        pltpu.VMEM((blk, WIDTH), F32),
                        pltpu.VMEM((SUBLANE, WIDTH), F32)],
        compiler_params=_compiler_params(("parallel", "arbitrary")),
        name="rglru_branch",
    )(proj, proj, conv_w, vec(conv_b), wa, vec(b_a), wi, vec(b_i), vec(lam))


def _ret_kernel(q_ref, k_ref, v_ref, g_ref, cos_ref, sin_ref, inner_ref, kte_ref, qfs_ref, cdec_ref,
                nw_ref, o_ref, state, ybuf, *, blk):
    @pl.when(pl.program_id(1) == 0)
    def _():
        state[...] = jnp.zeros_like(state)

    lane = lax.broadcasted_iota(jnp.int32, (CHUNK, RET_QK), 1)
    first_half = (lane % RET_QK_DIM) < (RET_QK_DIM // 2)
    half = RET_QK_DIM // 2

    def rotary(x, cos, sin):
        swapped = jnp.where(first_half, pltpu.roll(x, RET_QK - half, 1), pltpu.roll(x, half, 1))
        return x * cos + swapped * sin

    left = lax.broadcasted_iota(jnp.int32, (CHUNK, LANE), 1) < RET_QK_DIM
    npair = RET_HEADS // 2

    def halves(x):
        return jnp.concatenate([jnp.where(left, x, 0.0), jnp.where(left, 0.0, x)], axis=0)

    def chunk(c, carry):
        rows = slice(c * CHUNK, (c + 1) * CHUNK)
        cos = cos_ref[rows, :]
        sin = sin_ref[rows, :]
        q = rotary(q_ref[rows, :].astype(F32), cos, sin)
        k = rotary(k_ref[rows, :].astype(F32), cos, sin) * (RET_QK_DIM ** -0.5)
        k_end = k * kte_ref[...]
        q_start = q * qfs_ref[...]
        slabs = [slice(p * LANE, (p + 1) * LANE) for p in range(npair)]
        v2 = [v_ref[rows, 2 * p * RET_V_DIM:(2 * p + 2) * RET_V_DIM] for p in range(npair)]
        v2 = [jnp.concatenate([v[:, 0:RET_V_DIM], v[:, RET_V_DIM:]], axis=0) for v in v2]
        prev = [state[p] for p in range(npair)]
        s2 = [_bdot_nt(q[:, ps], halves(k[:, ps])) * inner_ref[p] for p, ps in enumerate(slabs)]
        kv = [_bdot_tn(halves(k_end[:, ps]), v2[p]) for p, ps in enumerate(slabs)]
        for p, ps in enumerate(slabs):
            rhs = jnp.concatenate([v2[p], prev[p]], axis=0)
            for e in range(2):
                keep = left if e == 0 else jnp.logical_not(left)
                lhs = jnp.concatenate([jnp.where(keep, s2[p], 0.0), jnp.where(keep, q_start[:, ps], 0.0)],
                                      axis=1)
                y = _bdot(lhs, rhs)
                mu = jnp.mean(y, axis=-1, keepdims=True)
                yc = y - mu
                var = jnp.mean(yc * yc, axis=-1, keepdims=True)
                h = 2 * p + e
                ybuf[rows, h * RET_V_DIM:(h + 1) * RET_V_DIM] = yc * lax.rsqrt(var + EPS)
            state[p] = prev[p] * cdec_ref[p] + kv[p]
        return carry

    for c in range(blk // CHUNK):
        chunk(c, 0)
        rows = slice(c * CHUNK, (c + 1) * CHUNK)
        o_ref[rows, :] = (_silu(g_ref[rows, :].astype(F32)) * (ybuf[rows, :] * nw_ref[...])).astype(o_ref.dtype)


def _ret_tables(s):
    pos_s = jnp.arange(s, dtype=F32)
    inv_freq = ROPE_BASE ** (-jnp.arange(0, RET_QK_DIM, 2, dtype=F32) / RET_QK_DIM)
    ang = pos_s[:, None] * inv_freq[None, :]
    cos, sin = lax.optimization_barrier((jnp.cos(ang), jnp.sin(ang)))
    cos_f = jnp.tile(jnp.concatenate([cos, cos], axis=1), (1, RET_HEADS))
    sin_f = jnp.tile(jnp.concatenate([-sin, sin], axis=1), (1, RET_HEADS))
    log_gamma = jnp.log1p(-jnp.exp2(-5.0 - jnp.arange(RET_HEADS, dtype=F32)))
    pos = jnp.arange(CHUNK, dtype=F32)
    inner = jnp.exp(log_gamma[:, None, None] * jnp.abs(pos[:, None] - pos[None, :]))
    k_to_end = jnp.exp(log_gamma[:, None] * (CHUNK - 1.0 - pos))
    q_from_start = jnp.exp(log_gamma[:, None] * (pos + 1.0))
    kte = jnp.repeat(k_to_end.T, RET_QK_DIM, axis=1)
    qfs = jnp.repeat(q_from_start.T, RET_QK_DIM, axis=1)
    inner2 = inner.reshape(RET_HEADS // 2, 2, CHUNK, CHUNK).transpose(0, 2, 1, 3).reshape(
        RET_HEADS // 2, CHUNK, 2 * CHUNK)
    cdec = jnp.broadcast_to(jnp.repeat(jnp.exp(log_gamma * CHUNK), RET_QK_DIM).reshape(
        RET_HEADS // 2, 2 * RET_QK_DIM, 1), (RET_HEADS // 2, 2 * RET_QK_DIM, RET_V_DIM))
    return cos_f, sin_f, inner2, kte, qfs, cdec


def _ret_branch(proj, b, s, norm_w, tables, *, blk=256):
    nblk = s // blk
    cos_f, sin_f, inner, kte, qfs, cdec = tables
    row = lambda i, j: i * nblk + j
    const = lambda i, j: (0, 0)
    return pl.pallas_call(
        functools.partial(_ret_kernel, blk=blk),
        grid=(b, nblk),
        in_specs=[pl.BlockSpec((blk, RET_QK), lambda i, j: (row(i, j), _col_block("rq"))),
                  pl.BlockSpec((blk, RET_QK), lambda i, j: (row(i, j), _col_block("rk"))),
                  pl.BlockSpec((blk, WIDTH), lambda i, j: (row(i, j), _col_block("rv"))),
                  pl.BlockSpec((blk, WIDTH), lambda i, j: (row(i, j), _col_block("rg"))),
                  pl.BlockSpec((blk, RET_QK), lambda i, j: (j, 0)),
                  pl.BlockSpec((blk, RET_QK), lambda i, j: (j, 0)),
                  pl.BlockSpec(inner.shape, lambda i, j: (0, 0, 0)),
                  pl.BlockSpec((CHUNK, RET_QK), const),
                  pl.BlockSpec((CHUNK, RET_QK), const),
                  pl.BlockSpec(cdec.shape, lambda i, j: (0, 0, 0)),
                  pl.BlockSpec((1, WIDTH), const)],
        out_specs=pl.BlockSpec((blk, WIDTH), lambda i, j: (row(i, j), 0)),
        out_shape=jax.ShapeDtypeStruct((b * s, WIDTH), BF16),
        scratch_shapes=[pltpu.VMEM((RET_HEADS // 2, 2 * RET_QK_DIM, RET_V_DIM), F32),
                        pltpu.VMEM((blk, WIDTH), F32)],
        compiler_params=_compiler_params(("parallel", "arbitrary")),
        name="retention_branch",
    )(proj, proj, proj, proj, cos_f, sin_f, inner, kte, qfs, cdec, norm_w.reshape(1, -1))


def _rwkv_kernel(r_ref, k_ref, v_ref, l_ref, mur_ref, muk_ref, muv_ref, mul_ref, w0_ref, w2_ref,
                 a0_ref, a2_ref, g2_ref, kk_ref, ka_ref, rk_ref, lnw_ref, lnb_ref, o_ref,
                 tail, tail_l, r_s, k_s, v_s, kk_s, b_s, lw_s, o_s, ub_s, op_s, gate_s, bon_s, wall_s, state,
                 *, blk, hw):
    nh = hw // RWKV_HEAD

    @pl.when(pl.program_id(2) == 0)
    def _():
        tail[...] = jnp.zeros_like(tail)
        tail_l[...] = jnp.zeros_like(tail_l)
        state[...] = jnp.zeros_like(state)

    group = A_UNROLL * CHUNK
    ngroups = blk // group
    rows = lax.broadcasted_iota(jnp.int32, (group, 1), 0)

    def shift_mix(x, prev_row, mu):
        prev = jnp.where(rows == 0, prev_row, pltpu.roll(x, 1, 0))
        return x + (prev - x) * mu

    lane = lax.broadcasted_iota(jnp.int32, (LANE, LANE), 1) // RWKV_HEAD
    lrow = lax.broadcasted_iota(jnp.int32, (LANE, LANE), 0) // RWKV_HEAD
    head_ones = (lane == lrow).astype(BF16)

    def head_sum(x):
        hi = x.astype(BF16)
        lo = (x - hi.astype(F32)).astype(BF16)
        return jnp.concatenate(
            [jnp.dot(hi[:, q:q + LANE], head_ones, preferred_element_type=F32)
             + jnp.dot(lo[:, q:q + LANE], head_ones, preferred_element_type=F32)
             for q in range(0, hw, LANE)], axis=1)

    def prep(gi, last):
        rs = slice(gi * group, (gi + 1) * group)
        r_raw, k_raw, v_raw, l_raw = (ref[rs, :].astype(F32) for ref in (r_ref, k_ref, v_ref, l_ref))
        r = shift_mix(r_raw, last[0], mur_ref[...])
        k = shift_mix(k_raw, last[1], muk_ref[...])
        v = shift_mix(v_raw, last[2], muv_ref[...])
        lo_ = shift_mix(l_raw, last[3], mul_ref[...])
        wl = lo_[:, 0:RWKV_W_LORA]
        al = lo_[:, RWKV_W_LORA:RWKV_W_LORA + RWKV_A_LORA]
        gl = lo_[:, RWKV_W_LORA + RWKV_A_LORA:]
        wz = w0_ref[...] + _bdot(jnp.tanh(wl), w2_ref[...])
        w = jnp.minimum(wz, 0.0) - jnp.log(1.0 + jnp.exp(-jnp.abs(wz))) - 0.5
        a = jax.nn.sigmoid(a0_ref[...] + _bdot(al, a2_ref[...]))
        gate_s[rs, :] = _bdot(jax.nn.sigmoid(gl), g2_ref[...])
        kk = k * kk_ref[...]
        kk = kk * lax.rsqrt(jnp.maximum(head_sum(kk * kk), 1e-24))
        k = k * (1.0 + (a - 1.0) * ka_ref[...])
        bon_s[rs, :] = head_sum(r * k * rk_ref[...]) * v
        r_s[rs, :] = r
        k_s[rs, :] = k
        v_s[rs, :] = v
        kk_s[rs, :] = kk
        b_s[rs, :] = kk * a
        lw_s[rs, :] = -jnp.exp(w)
        return [x[group - 1:group, :] for x in (r_raw, k_raw, v_raw, l_raw)]

    last = [tail[0:1, :], tail[1:2, :], tail[2:3, :], tail_l[0:1, :]]
    for gi in range(ngroups):
        last = prep(gi, last)
    tail[0:1, :], tail[1:2, :], tail[2:3, :], tail_l[0:1, :] = last

    trow = lax.broadcasted_iota(jnp.int32, (CHUNK, 2 * CHUNK), 0)
    tcol = lax.broadcasted_iota(jnp.int32, (CHUNK, 2 * CHUNK), 1) % CHUNK
    strict = tcol < trow
    incl = tcol <= trow
    left = lax.broadcasted_iota(jnp.int32, (CHUNK, LANE), 1) < RWKV_HEAD
    left2 = lax.broadcasted_iota(jnp.int32, (2 * CHUNK, LANE), 1) < RWKV_HEAD

    eye2 = (tcol == trow).astype(F32)
    zeros = jnp.zeros((CHUNK, LANE), F32)
    slabs = [slice(p * LANE, (p + 1) * LANE) for p in range(nh // 2)]
    lcat = lambda a, b: jnp.concatenate([a, b], axis=1)
    rcat = lambda *a: jnp.concatenate(a, axis=0)
    keep_l = lambda a: jnp.where(left, a, 0.0)
    keep_r = lambda a: jnp.where(left, 0.0, a)

    def phase_a(g):
        cs = [g * A_UNROLL + i for i in range(A_UNROLL)]
        sls = [slice(c * CHUNK, (c + 1) * CHUNK) for c in cs]
        pre = []
        for sl in sls:
            lw = lw_s[sl, :]
            cum = _cumsum_rows(lw)
            cum_last = cum[CHUNK - 1:CHUNK, :]
            w_inv = jnp.exp(-cum)
            to_end = jnp.exp(cum_last - cum)
            pre.append(dict(rt=r_s[sl, :] * jnp.exp(cum), kt=k_s[sl, :] * w_inv, bt=b_s[sl, :] * w_inv,
                            kap=kk_s[sl, :] * jnp.exp(cum - lw), k_end=k_s[sl, :] * to_end,
                            b_end=b_s[sl, :] * to_end, w_all=jnp.exp(cum_last), vv=v_s[sl, :]))
        items = [(i, ps) for i in range(A_UNROLL) for ps in slabs]
        pm2 = []
        for i, ps in items:
            d = pre[i]
            lhs = rcat(d["kap"][:, ps], d["rt"][:, ps])
            rhs_e = jnp.where(left2, rcat(d["bt"][:, ps], d["kt"][:, ps]), 0.0)
            rhs_o = jnp.where(left2, 0.0, rcat(d["kt"][:, ps], d["bt"][:, ps]))
            pm2.append(_bdot_nt(lhs, rcat(rhs_e, rhs_o)))
        top = [(jnp.where(strict, m[0:CHUNK, 0:LANE], 0.0), jnp.where(strict, m[0:CHUNK, LANE:], 0.0))
               for m in pm2]
        bot = [(jnp.where(incl, m[CHUNK:, 0:LANE], 0.0), jnp.where(incl, m[CHUNK:, LANE:], 0.0))
               for m in pm2]
        v_swp = [pltpu.roll(pre[i]["vv"][:, ps], RWKV_HEAD, 1) for i, ps in items]
        v_oe = [rcat(keep_l(v), keep_r(v)) for v in v_swp]
        av = [_bdot(jnp.where(left, to, te), v_oe[n]) for n, (te, to) in enumerate(top)]
        op = [_bdot(jnp.where(left, bo, be), v_oe[n]) for n, (be, bo) in enumerate(bot)]
        z = [(jnp.where(left, -te, eye2), jnp.where(left, eye2, -to)) for te, to in top]

        def step(ze, zo):
            return _bdot(jnp.where(left, ze, zo), rcat(lcat(ze, zeros), lcat(zeros, zo)))

        span = 1
        while 2 * span < CHUNK:
            res = [step(ze, zo) for ze, zo in z]
            z = [(r[:, 0:LANE] + keep_r(ze), r[:, LANE:] + keep_l(zo)) for r, (ze, zo) in zip(res, z)]
            span *= 2
        res = [step(ze, zo) for ze, zo in z]
        t_oe = [jnp.where(left, r[:, LANE:] + zo, r[:, 0:LANE] + ze) for r, (ze, zo) in zip(res, z)]
        for n, (i, ps) in enumerate(items):
            kap = pre[i]["kap"][:, ps]
            prod = _bdot(t_oe[n], lcat(rcat(keep_l(av[n]), keep_r(av[n])), rcat(keep_r(kap), keep_l(kap))))
            sl = sls[i]
            lw_s[sl, ps] = prod[:, 0:LANE]
            kk_s[sl, ps] = prod[:, LANE:]
            ub_s[sl, ps] = jnp.where(left, bot[n][0], bot[n][1])
            op_s[sl, ps] = op[n]
            v_s[sl, ps] = v_swp[n]
        for i, sl in enumerate(sls):
            r_s[sl, :] = pre[i]["rt"]
            k_s[sl, :] = pre[i]["k_end"]
            b_s[sl, :] = pre[i]["b_end"]
            wall_s[cs[i]] = jnp.broadcast_to(pre[i]["w_all"], (SUBLANE, hw))

    def phase_b(c):
        sl = slice(c * CHUNK, (c + 1) * CHUNK)
        w_all = wall_s[c]
        s0 = [(state[2 * p], state[2 * p + 1]) for p in range(nh // 2)]
        s_oe = [rcat(so, se) for se, so in s0]
        u = [-(lw_s[sl, ps] + _bdot_nt(kk_s[sl, ps], s_oe[p])) for p, ps in enumerate(slabs)]
        op = [op_s[sl, ps] + _bdot_nt(r_s[sl, ps], s_oe[p]) for p, ps in enumerate(slabs)]
        for p, ps in enumerate(slabs):
            o_sw = _bdot(ub_s[sl, ps], rcat(keep_r(u[p]), keep_l(u[p]))) + op[p]
            o_s[sl, ps] = pltpu.roll(o_sw, RWKV_HEAD, 1)
        for p, ps in enumerate(slabs):
            add = _bdot_tn(rcat(u[p], v_s[sl, ps]), rcat(b_s[sl, ps], k_s[sl, ps]))
            state[2 * p] = s0[p][0] * w_all[0:1, ps] + keep_l(add[RWKV_HEAD:, :])
            state[2 * p + 1] = s0[p][1] * w_all[0:1, ps] + keep_r(add[0:RWKV_HEAD, :])

    def finish(gi):
        rs = slice(gi * group, (gi + 1) * group)
        o = o_s[rs, :]
        mu = head_sum(o) * (1.0 / RWKV_HEAD)
        oc = o - mu
        var = head_sum(oc * oc) * (1.0 / RWKV_HEAD)
        o = oc * lax.rsqrt(var + RWKV_LN_EPS) * lnw_ref[...] + lnb_ref[...]
        o_ref[rs, :] = ((o + bon_s[rs, :]) * gate_s[rs, :]).astype(o_ref.dtype)

    for g in range(ngroups):
        phase_a(g)
    for g in range(ngroups):
        for c in range(g * A_UNROLL, (g + 1) * A_UNROLL):
            phase_b(c)
        finish(g)


def _rwkv_branch(proj, b, s, mu, w0, w2, a0, a2, g2, k_k, k_a, r_k, ln_w, ln_b, *, blk=512, hw=WIDTH):
    nblk = s // blk
    nslab = WIDTH // hw
    row = lambda i, p, j: i * nblk + j
    vec = lambda v: v.reshape(1, -1)
    hcol = lambda i, p, j: (0, p)
    mu_r, mu_k, mu_v, mu_l = (mu[0:WIDTH], mu[WIDTH:2 * WIDTH], mu[2 * WIDTH:3 * WIDTH],
                              mu[3 * WIDTH:])
    wblk = _col_block("wr")
    lblk = _col_block("lora")
    return pl.pallas_call(
        functools.partial(_rwkv_kernel, blk=blk, hw=hw),
        grid=(b, nslab, nblk),
        in_specs=[pl.BlockSpec((blk, hw), lambda i, p, j: (row(i, p, j), wblk * nslab + p)),
                  pl.BlockSpec((blk, hw), lambda i, p, j: (row(i, p, j), (wblk + 1) * nslab + p)),
                  pl.BlockSpec((blk, hw), lambda i, p, j: (row(i, p, j), (wblk + 2) * nslab + p)),
                  pl.BlockSpec((blk, RWKV_LORA), lambda i, p, j: (row(i, p, j), lblk)),
                  pl.BlockSpec((1, hw), hcol),
                  pl.BlockSpec((1, hw), hcol),
                  pl.BlockSpec((1, hw), hcol),
                  pl.BlockSpec((1, RWKV_LORA), lambda i, p, j: (0, 0)),
                  pl.BlockSpec((1, hw), hcol),
                  pl.BlockSpec((RWKV_W_LORA, hw), hcol),
                  pl.BlockSpec((1, hw), hcol),
                  pl.BlockSpec((RWKV_A_LORA, hw), hcol),
                  pl.BlockSpec((RWKV_G_LORA, hw), hcol),
                  pl.BlockSpec((1, hw), hcol),
                  pl.BlockSpec((1, hw), hcol),
                  pl.BlockSpec((1, hw), hcol),
                  pl.BlockSpec((1, hw), hcol),
                  pl.BlockSpec((1, hw), hcol)],
        out_specs=pl.BlockSpec((blk, hw), lambda i, p, j: (row(i, p, j), p)),
        out_shape=jax.ShapeDtypeStruct((b * s, WIDTH), BF16),
        scratch_shapes=[pltpu.VMEM((SUBLANE, hw), F32), pltpu.VMEM((SUBLANE, RWKV_LORA), F32)]
        + [pltpu.VMEM((blk, hw), F32)] * 11
        + [pltpu.VMEM((blk // CHUNK, SUBLANE, hw), F32)]
        + [pltpu.VMEM((hw // RWKV_HEAD, RWKV_HEAD, LANE), F32)],
        compiler_params=_compiler_params(("parallel", "parallel", "arbitrary")),
        name="rwkv7_branch",
    )(proj, proj, proj, proj, vec(mu_r), vec(mu_k), vec(mu_v), vec(mu_l), vec(w0), w2.astype(BF16),
      vec(a0), a2.astype(BF16), g2.astype(BF16), vec(k_k), vec(k_a), vec(r_k), vec(ln_w), vec(ln_b))


def _merge_kernel(y0_ref, y1_ref, y2_ref, y3_ref, gl_ref, x_ref, wb_ref, wo_ref, o_ref):
    merged = None
    for m, y_ref in enumerate((y0_ref, y1_ref, y2_ref, y3_ref)):
        br = jnp.dot(y_ref[...], wb_ref[m], preferred_element_type=F32)
        term = jax.nn.sigmoid(gl_ref[:, m * D_MODEL:(m + 1) * D_MODEL].astype(F32)) * br
        merged = term if merged is None else merged + term
    o_ref[...] = x_ref[...] + jnp.dot(merged.astype(BF16), wo_ref[...], preferred_element_type=F32)


def _merge(ys, proj, x2, w_branch, w_out, layer, *, tm=512):
    t = x2.shape[0]
    tok = lambda i: (i, 0)
    return pl.pallas_call(
        _merge_kernel,
        grid=(t // tm,),
        in_specs=[pl.BlockSpec((tm, WIDTH), tok)] * N_BRANCH
        + [pl.BlockSpec((tm, N_BRANCH * D_MODEL), lambda i: (i, _col_block("gates"))),
           pl.BlockSpec((tm, D_MODEL), tok),
           pl.BlockSpec((None, N_BRANCH, WIDTH, D_MODEL), lambda i: (layer, 0, 0, 0)),
           pl.BlockSpec((None, D_MODEL, D_MODEL), lambda i: (layer, 0, 0))],
        out_specs=pl.BlockSpec((tm, D_MODEL), tok),
        out_shape=jax.ShapeDtypeStruct((t, D_MODEL), F32),
        compiler_params=_compiler_params(("parallel",)),
        name="branch_merge",
    )(*ys, proj, x2, w_branch, w_out)


FFN_TC = 256
FFN_AHEAD = 3


def _ffn_kernel(x_ref, g_ref, wu_ref, cw_ref, cb_ref, wd_ref, gf_ref, o_ref, cbuf, tail, *, tm, final):
    @pl.when(pl.program_id(1) == 0)
    def _():
        tail[...] = jnp.zeros_like(tail)

    x = x_ref[...]
    ms = jnp.mean(x * x, axis=-1, keepdims=True)
    hn = (x * lax.rsqrt(ms + EPS) * g_ref[...]).astype(BF16)
    nsteps = D_FF // FFN_TC

    def up_conv(i):
        halves = []
        for half in range(2):
            c0 = half * D_FF + i * FFN_TC
            cs = slice(c0, c0 + FFN_TC)
            buf = cbuf.at[2 * (i % (FFN_AHEAD + 1)) + half]
            buf[0:SUBLANE, :] = tail[:, cs]
            buf[SUBLANE:SUBLANE + tm, :] = jnp.dot(hn, wu_ref[:, cs], preferred_element_type=F32)
            y = cb_ref[:, cs]
            for k in range(FFN_CONV):
                s = FFN_CONV - 1 - k
                y = y + cw_ref[k:k + 1, cs] * buf[SUBLANE - s:SUBLANE - s + tm, :]
            tail[:, cs] = buf[tm:tm + SUBLANE, :]
            halves.append(y)
        return halves

    acc = x
    ups = [up_conv(i) for i in range(FFN_AHEAD)]
    for i in range(nsteps):
        if i + FFN_AHEAD < nsteps:
            ups.append(up_conv(i + FFN_AHEAD))
        val, gt = ups[i]
        acc = acc + jnp.dot((_silu(gt) * val).astype(BF16), wd_ref[i * FFN_TC:(i + 1) * FFN_TC, :],
                            preferred_element_type=F32)
    if final:
        ms = jnp.mean(acc * acc, axis=-1, keepdims=True)
        acc = acc * lax.rsqrt(ms + EPS) * gf_ref[...]
    o_ref[...] = acc


def _conv_ffn(x2, b, s, gain, w_up, conv_w, conv_b, w_down, gain_final, layer, *, final, tm=512):
    nblk = s // tm
    row = lambda i, j: (i * nblk + j, 0)
    const = lambda i, j: (0, 0)
    return pl.pallas_call(
        functools.partial(_ffn_kernel, tm=tm, final=final),
        grid=(b, nblk),
        in_specs=[pl.BlockSpec((tm, D_MODEL), row),
                  pl.BlockSpec((1, D_MODEL), const),
                  pl.BlockSpec((None, D_MODEL, 2 * D_FF), lambda i, j: (layer, 0, 0)),
                  pl.BlockSpec((FFN_CONV, 2 * D_FF), const),
                  pl.BlockSpec((1, 2 * D_FF), const),
                  pl.BlockSpec((None, D_FF, D_MODEL), lambda i, j: (layer, 0, 0)),
                  pl.BlockSpec((1, D_MODEL), const)],
        out_specs=pl.BlockSpec((tm, D_MODEL), row),
        out_shape=jax.ShapeDtypeStruct((b * s, D_MODEL), F32),
        scratch_shapes=[pltpu.VMEM((2 * (FFN_AHEAD + 1), tm + 2 * SUBLANE, FFN_TC), F32),
                        pltpu.VMEM((SUBLANE, 2 * D_FF), F32)],
        compiler_params=_compiler_params(("parallel", "arbitrary")),
        name="conv_ffn",
    )(x2, gain.reshape(1, -1), w_up, conv_w, conv_b.reshape(1, -1), w_down, gain_final.reshape(1, -1))


def _reorder_w_in(w):
    cols = []
    for n in _DST_ORDER:
        off, wd = _SRC[n]
        cols.append(w[..., off:off + wd].astype(BF16))
    pad = N_PROJ - _DST["dt"][0] - SSM_HEADS
    cols.append(jnp.zeros(w.shape[:-1] + (pad,), BF16))
    return jnp.concatenate(cols, axis=-1)


def kernel(x, norm_mix, w_in, ssm_conv_w, ssm_conv_b, ssm_dt_bias, ssm_a_log, ssm_d, ssm_norm, lru_conv_w, lru_conv_b, lru_w_a, lru_b_a, lru_w_i, lru_b_i, lru_lam, ret_norm, rwkv_mu, rwkv_w0, rwkv_w2, rwkv_a0, rwkv_a2, rwkv_g2, rwkv_k_k, rwkv_k_a, rwkv_r_k, rwkv_ln_w, rwkv_ln_b, w_branch, w_out, norm_ffn, ffn_up, ffn_conv_w, ffn_conv_b, ffn_down, norm_final):
    b, s, d = x.shape
    x2 = x.reshape(b * s, d)
    tables = _ret_tables(s)
    w_branch_b, w_out_b = _cast_bf16(w_branch), _cast_bf16(w_out)
    ffn_up_b, ffn_down_b = _cast_bf16(ffn_up), _cast_bf16(ffn_down)
    for l in range(DEPTH):
        proj = _norm_matmul(x2, norm_mix[l], _reorder_w_in(w_in[l])[None], 0)
        y_ssd = _ssd_branch(proj, b, s, ssm_conv_w[l], ssm_conv_b[l], ssm_dt_bias[l], ssm_a_log[l],
                            ssm_d[l], ssm_norm[l])
        y_lru = _lru_branch(proj, b, s, lru_conv_w[l], lru_conv_b[l], lru_w_a[l], lru_b_a[l],
                            lru_w_i[l], lru_b_i[l], lru_lam[l])
        y_ret = _ret_branch(proj, b, s, ret_norm[l], tables)
        y_rwkv = _rwkv_branch(proj, b, s, rwkv_mu[l], rwkv_w0[l], rwkv_w2[l], rwkv_a0[l], rwkv_a2[l],
                              rwkv_g2[l], rwkv_k_k[l], rwkv_k_a[l], rwkv_r_k[l], rwkv_ln_w[l],
                              rwkv_ln_b[l])
        x2 = _merge((y_ssd, y_lru, y_ret, y_rwkv), proj, x2, w_branch_b, w_out_b, l)
        x2 = _conv_ffn(x2, b, s, norm_ffn[l], ffn_up_b, ffn_conv_w[l], ffn_conv_b[l], ffn_down_b,
                       norm_final, l, final=(l == DEPTH - 1))
    return x2.reshape(b, s, d)
```
